```python
import math
import jax, jax.numpy as jnp
from jax import lax
import numpy as np

D_MODEL = 1024
BATCH = 8
SEQ = 2048
DEPTH = 1
DEC_BATCH = 128
DEC_SEQ = 8
PAST_LEN = 16384
PAGE_SIZE = 128

MIX_WIDTH = D_MODEL
S5_WIDTH = MIX_WIDTH // 2
S5_GROUP = 16
S5_GROUPS = S5_WIDTH // S5_GROUP
S5_STATE = 64
HG_WIDTH = MIX_WIDTH - S5_WIDTH
HG_HEAD_DIM = 128
HG_HEADS = HG_WIDTH // HG_HEAD_DIM
HG_CHUNK = 64
IN_COLS = S5_WIDTH + 4 * HG_WIDTH
D_FF = ((-(-8 * D_MODEL // 3) + 255) // 256) * 256
EPS = 1e-6
DT_MIN = 1e-3
DT_MAX = 1e-1

kernel_name = "hymba_s5_hgrn2_decode_step"


def rmsnorm(x, g):
    xf = x.astype(jnp.float32)
    r = xf * lax.rsqrt(jnp.mean(xf * xf, axis=-1, keepdims=True) + EPS)
    return (r * g.astype(jnp.float32)).astype(x.dtype)


def s5_discretize(a_re, a_im, log_dt, b_re, b_im):
    f32 = jnp.float32
    a_re = a_re.astype(f32); a_im = a_im.astype(f32)
    dt = jnp.exp(log_dt.astype(f32))[:, None]
    mag = jnp.exp(a_re * dt)
    ab_re = mag * jnp.cos(a_im * dt)
    ab_im = mag * jnp.sin(a_im * dt)
    den = a_re * a_re + a_im * a_im
    nr = ab_re - 1.0
    ni = ab_im
    f_re = (nr * a_re + ni * a_im) / den
    f_im = (ni * a_re - nr * a_im) / den
    b_re = b_re.astype(f32); b_im = b_im.astype(f32)
    bb_re = f_re[..., None] * b_re - f_im[..., None] * b_im
    bb_im = f_re[..., None] * b_im + f_im[..., None] * b_re
    return ab_re, ab_im, bb_re, bb_im


def s5_combine(e1, e2):
    a1r, a1i, b1r, b1i = e1
    a2r, a2i, b2r, b2i = e2
    return (a2r * a1r - a2i * a1i,
            a2r * a1i + a2i * a1r,
            a2r * b1r - a2i * b1i + b2r,
            a2r * b1i + a2i * b1r + b2i)


def s5_mixer(u, h0_re, h0_im, a_re, a_im, log_dt, b_re, b_im, c_re, c_im, d_skip, w_glu):
    f32 = jnp.float32
    B_, T, _ = u.shape
    ab_re, ab_im, bb_re, bb_im = s5_discretize(a_re, a_im, log_dt, b_re, b_im)
    uf = u.astype(f32)
    ug = uf.reshape(B_, T, S5_GROUPS, S5_GROUP)
    bu_re = jnp.einsum('btgc,gnc->btgn', ug, bb_re)
    bu_im = jnp.einsum('btgc,gnc->btgn', ug, bb_im)
    h0_re = h0_re.astype(f32); h0_im = h0_im.astype(f32)
    bu_re = bu_re.at[:, 0].add(ab_re * h0_re - ab_im * h0_im)
    bu_im = bu_im.at[:, 0].add(ab_re * h0_im + ab_im * h0_re)
    a_r = jnp.broadcast_to(ab_re, bu_re.shape)
    a_i = jnp.broadcast_to(ab_im, bu_im.shape)
    _, _, h_re, h_im = lax.associative_scan(s5_combine, (a_r, a_i, bu_re, bu_im), axis=1)
    y = (jnp.einsum('btgn,gcn->btgc', h_re, c_re.astype(f32))
         - jnp.einsum('btgn,gcn->btgc', h_im, c_im.astype(f32)))
    y = y.reshape(B_, T, S5_WIDTH) + d_skip.astype(f32) * uf
    g = jax.nn.gelu(y)
    out = g * jax.nn.sigmoid(g @ w_glu.astype(f32))
    return out.astype(u.dtype), h_re[:, -1], h_im[:, -1]


def hgrn2_mixer(q, fz, iv, og, lb, s0, g_norm):
    f32 = jnp.float32
    B_, T, _ = q.shape
    f = lb + (1.0 - lb) * jax.nn.sigmoid(fz.astype(f32))
    log_f = jnp.log(f)
    k = 1.0 - f
    chunk = math.gcd(HG_CHUNK, T)
    n_chunks = T // chunk

    def to_chunks(t):
        t = t.astype(f32).reshape(B_, n_chunks, chunk, HG_HEADS, HG_HEAD_DIM)
        return t.transpose(1, 0, 3, 2, 4)

    qc, kc, vc, lc = (to_chunks(t) for t in (q, k, iv, log_f))
    causal = jnp.tril(jnp.ones((chunk, chunk), dtype=bool))[:, :, None]

    def step(S, xs):
        qb, kb, vb, lfb = xs
        G = jnp.cumsum(lfb, axis=-2)
        diff = G[..., :, None, :] - G[..., None, :, :]
        decay = jnp.where(causal, jnp.exp(jnp.where(causal, diff, 0.0)), 0.0)
        att = jnp.einsum('bhtk,bhsk,bhtsk->bhts', qb, kb, decay)
        o = (jnp.einsum('bhts,bhsv->bhtv', att, vb)
             + jnp.einsum('bhtk,bhkv->bhtv', qb * jnp.exp(G), S))
        g_last = G[..., -1:, :]
        S_new = (jnp.exp(g_last[..., 0, :])[..., None] * S
                 + jnp.einsum('bhsk,bhsv->bhkv', kb * jnp.exp(g_last - G), vb))
        return S_new, o

    S_last, o = lax.scan(step, s0.astype(f32), (qc, kc, vc, lc))
    o = o.transpose(1, 0, 3, 2, 4).reshape(B_, T, HG_HEADS, HG_HEAD_DIM)
    o = o * lax.rsqrt(jnp.mean(o * o, axis=-1, keepdims=True) + EPS) * g_norm.astype(f32)
    gate = jax.nn.silu(og.astype(f32)).reshape(B_, T, HG_HEADS, HG_HEAD_DIM)
    o = (o * gate).reshape(B_, T, HG_WIDTH)
    return o.astype(q.dtype), S_last


def trunk(x, st_re, st_im, st_hg, lb_param, norm_mix, w_in, s5_a_re, s5_a_im, s5_log_dt,
          s5_b_re, s5_b_im, s5_c_re, s5_c_im, s5_d, s5_w_glu, hg_norm, w_out,
          norm_ffn, w_gate, w_up, w_down, norm_final):
    lb_all = jnp.cumsum(jax.nn.softmax(lb_param.astype(jnp.float32), axis=0), axis=0)
    new_re, new_im, new_hg = [], [], []
    for l in range(DEPTH):
        h = rmsnorm(x, norm_mix[l])
        proj = h @ w_in[l]
        u = proj[..., :S5_WIDTH]
        q = proj[..., S5_WIDTH:S5_WIDTH + HG_WIDTH]
        fz = proj[..., S5_WIDTH + HG_WIDTH:S5_WIDTH + 2 * HG_WIDTH]
        iv = proj[..., S5_WIDTH + 2 * HG_WIDTH:S5_WIDTH + 3 * HG_WIDTH]
        og = proj[..., S5_WIDTH + 3 * HG_WIDTH:]
        y5, h_re, h_im = s5_mixer(u, st_re[l], st_im[l], s5_a_re[l], s5_a_im[l], s5_log_dt[l],
                                  s5_b_re[l], s5_b_im[l], s5_c_re[l], s5_c_im[l], s5_d[l], s5_w_glu[l])
        yh, S_hg = hgrn2_mixer(q, fz, iv, og, lb_all[l], st_hg[l], hg_norm[l])
        x = x + jnp.concatenate([y5, yh], axis=-1) @ w_out[l]
        h2 = rmsnorm(x, norm_ffn[l])
        x = x + (jax.nn.silu(h2 @ w_gate[l]) * (h2 @ w_up[l])) @ w_down[l]
        new_re.append(h_re); new_im.append(h_im); new_hg.append(S_hg)
    y = rmsnorm(x, norm_final)
    return y, jnp.stack(new_re), jnp.stack(new_im), jnp.stack(new_hg)


def setup_inputs(seed: int = 0) -> dict:
    key = jax.random.key(seed)
    ks = jax.random.split(key, 32)
    f32 = jnp.float32
    nrm = lambda k, s, sc: jax.random.normal(k, s, f32) * sc
    n_idx = jnp.arange(S5_STATE, dtype=f32)
    a_re = -0.5 * jnp.exp(nrm(ks[5], (DEPTH, S5_GROUPS, S5_STATE), 0.02))
    a_im = jnp.pi * n_idx + nrm(ks[6], (DEPTH, S5_GROUPS, S5_STATE), 0.01)
    log_dt = jax.random.uniform(ks[7], (DEPTH, S5_GROUPS), f32, math.log(DT_MIN), math.log(DT_MAX))
    return {
        "x_prompt": nrm(ks[0], (BATCH, SEQ, D_MODEL), 1.0),
        "x_sample": nrm(ks[1], (DEC_BATCH, DEC_SEQ, D_MODEL), 1.0),
        "state_s5_re": nrm(ks[2], (DEPTH, DEC_BATCH, S5_GROUPS, S5_STATE), 0.5),
        "state_s5_im": nrm(ks[3], (DEPTH, DEC_BATCH, S5_GROUPS, S5_STATE), 0.5),
        "state_hgrn": nrm(ks[4], (DEPTH, DEC_BATCH, HG_HEADS, HG_HEAD_DIM, HG_HEAD_DIM), 0.5),
        "lb_param": nrm(ks[8], (DEPTH + 1, HG_WIDTH), 0.5),
        "norm_mix": 1.0 + nrm(ks[9], (DEPTH, D_MODEL), 0.02),
        "w_in": nrm(ks[10], (DEPTH, D_MODEL, IN_COLS), D_MODEL ** -0.5),
        "s5_a_re": a_re,
        "s5_a_im": a_im,
        "s5_log_dt": log_dt,
        "s5_b_re": nrm(ks[11], (DEPTH, S5_GROUPS, S5_STATE, S5_GROUP), (2 * S5_GROUP) ** -0.5),
        "s5_b_im": nrm(ks[12], (DEPTH, S5_GROUPS, S5_STATE, S5_GROUP), (2 * S5_GROUP) ** -0.5),
        "s5_c_re": nrm(ks[13], (DEPTH, S5_GROUPS, S5_GROUP, S5_STATE), (2 * S5_STATE) ** -0.5),
        "s5_c_im": nrm(ks[14], (DEPTH, S5_GROUPS, S5_GROUP, S5_STATE), (2 * S5_STATE) ** -0.5),
        "s5_d": nrm(ks[15], (DEPTH, S5_WIDTH), 1.0),
        "s5_w_glu": nrm(ks[16], (DEPTH, S5_WIDTH, S5_WIDTH), S5_WIDTH ** -0.5),
        "hg_norm": 1.0 + nrm(ks[17], (DEPTH, HG_HEAD_DIM), 0.02),
        "w_out": nrm(ks[18], (DEPTH, MIX_WIDTH, D_MODEL), MIX_WIDTH ** -0.5),
        "norm_ffn": 1.0 + nrm(ks[19], (DEPTH, D_MODEL), 0.02),
        "w_gate": nrm(ks[20], (DEPTH, D_MODEL, D_FF), D_MODEL ** -0.5),
        "w_up": nrm(ks[21], (DEPTH, D_MODEL, D_FF), D_MODEL ** -0.5),
        "w_down": nrm(ks[22], (DEPTH, D_FF, D_MODEL), D_FF ** -0.5),
        "norm_final": 1.0 + nrm(ks[23], (D_MODEL,), 0.02),
    }


def reference(x_prompt, x_sample, state_s5_re, state_s5_im, state_hgrn, lb_param, norm_mix, w_in,
              s5_a_re, s5_a_im, s5_log_dt, s5_b_re, s5_b_im, s5_c_re, s5_c_im, s5_d, s5_w_glu,
              hg_norm, w_out, norm_ffn, w_gate, w_up, w_down, norm_final):
    f32 = jnp.float32
    bp = x_prompt.shape[0]
    zero_re = jnp.zeros((DEPTH, bp, S5_GROUPS, S5_STATE), f32)
    zero_hg = jnp.zeros((DEPTH, bp, HG_HEADS, HG_HEAD_DIM, HG_HEAD_DIM), f32)
    y_prompt, p_re, p_im, p_hg = trunk(
        x_prompt, zero_re, zero_re, zero_hg, lb_param, norm_mix, w_in, s5_a_re, s5_a_im, s5_log_dt,
        s5_b_re, s5_b_im, s5_c_re, s5_c_im, s5_d, s5_w_glu, hg_norm, w_out,
        norm_ffn, w_gate, w_up, w_down, norm_final)
    y_sample, s_re, s_im, s_hg = trunk(
        x_sample, state_s5_re, state_s5_im, state_hgrn, lb_param, norm_mix, w_in, s5_a_re, s5_a_im,
        s5_log_dt, s5_b_re, s5_b_im, s5_c_re, s5_c_im, s5_d, s5_w_glu, hg_norm, w_out,
        norm_ffn, w_gate, w_up, w_down, norm_final)
    return (y_prompt, y_sample, p_re, p_im, p_hg, s_re, s_im, s_hg)
```

```python
import functools
import math

import jax
import jax.numpy as jnp
from jax import lax
from jax.experimental import pallas as pl
from jax.experimental.pallas import tpu as pltpu

F32 = jnp.float32
BF16 = jnp.bfloat16
HIGHEST = lax.Precision.HIGHEST

D_MODEL = 1024
S5_WIDTH = 512
S5_GROUP = 16
S5_GROUPS = 32
S5_STATE = 64
S5_FLAT = S5_GROUPS * S5_STATE
HG_WIDTH = 512
HG_DIM = 128
HG_HEADS = 4
IN_COLS = S5_WIDTH + 4 * HG_WIDTH
EPS = 1e-6

LANES = 128
SUBLANES = 8
CHUNK = 64
N_SLABS = S5_FLAT // LANES
HALF_U = S5_WIDTH // 2
HALF_SLABS = N_SLABS // 2
SUB = 16
VMEM_LIMIT = 56 * 1024 * 1024


def _dot(a, b):
    return jnp.dot(a, b, preferred_element_type=F32)


def _dot_nt(a, b):
    return lax.dot_general(a, b, (((1,), (1,)), ((), ())), preferred_element_type=F32)


def _dot_tn(a, b, precision=None):
    return lax.dot_general(a, b, (((0,), (0,)), ((), ())), precision=precision,
                           preferred_element_type=F32)


def _rms(x, g):
    return x * lax.rsqrt(jnp.mean(x * x, axis=-1, keepdims=True) + EPS) * g


def _sigmoid(x):
    return jax.nn.sigmoid(x)


def _gelu_tanh(x):
    c = math.sqrt(2.0 / math.pi)
    return 0.5 * x * (1.0 + jnp.tanh(c * (x + 0.044715 * (x * x * x))))


def _mixer_kernel(*refs, nb, tt, has_state):
    it = iter(refs)
    x_ref = next(it)
    if has_state:
        h0re_ref, h0im_ref, s0_ref = next(it), next(it), next(it)
    (norm_ref, win_ref, wb_ref, are_ref, aim_ref, wc_ref, dskip_ref, wglu_ref, lb_ref,
     gnorm_ref, wout_ref, mask_ref) = (next(it) for _ in range(12))
    x1_ref, hre_ref, him_ref, s_ref = (next(it) for _ in range(4))
    proj_scr, bre_scr, bim_scr = (next(it) for _ in range(3))

    bpc = CHUNK // tt
    nchunks = nb // bpc
    pitch = tt + SUBLANES if tt == CHUNK else tt
    cpitch = pitch * bpc
    ngroups = nb // SUBLANES

    tstep = pl.program_id(1)

    @pl.when(tstep == 0)
    def _init():
        if has_state:
            hre_ref[...] = h0re_ref[...]
            him_ref[...] = h0im_ref[...]
        else:
            hre_ref[...] = jnp.zeros_like(hre_ref)
            him_ref[...] = jnp.zeros_like(him_ref)
            s_ref[...] = jnp.zeros_like(s_ref)

    def phase1(c, carry):
        xc = x_ref[pl.ds(c * bpc, bpc)].reshape(CHUNK, D_MODEL)
        hn = _rms(xc, norm_ref[...]).astype(BF16)
        proj = _dot(hn, win_ref[...])
        r0 = pl.multiple_of(c * CHUNK, CHUNK)
        proj_scr[pl.ds(r0, CHUNK), :] = proj
        ub = proj[:, :S5_WIDTH].astype(BF16)
        s0 = pl.multiple_of(c * cpitch, SUBLANES)
        for hf in range(2):
            bu = _dot(ub[:, hf * HALF_U:(hf + 1) * HALF_U], wb_ref[hf])
            for jj in range(HALF_SLABS):
                j = hf * HALF_SLABS + jj
                bre_scr[j, pl.ds(s0, CHUNK), :] = bu[:, jj * LANES:(jj + 1) * LANES]
                bim_scr[j, pl.ds(s0, CHUNK), :] = bu[:, (HALF_SLABS + jj) * LANES:
                                                     (HALF_SLABS + jj + 1) * LANES]
        return carry

    lax.fori_loop(0, nchunks, phase1, 0)

    slabs_per_pass = 4
    for g in range(ngroups):
        base = g * SUBLANES * pitch
        for sp in range(N_SLABS // slabs_per_pass):
            js = [sp * slabs_per_pass + k for k in range(slabs_per_pass)]
            a_re = [jnp.broadcast_to(are_ref[:, j * LANES:(j + 1) * LANES], (SUBLANES, LANES)) for j in js]
            a_im = [jnp.broadcast_to(aim_ref[:, j * LANES:(j + 1) * LANES], (SUBLANES, LANES)) for j in js]
            h_re = tuple(hre_ref[g * SUBLANES:(g + 1) * SUBLANES, j * LANES:(j + 1) * LANES] for j in js)
            h_im = tuple(him_ref[g * SUBLANES:(g + 1) * SUBLANES, j * LANES:(j + 1) * LANES] for j in js)

            def step(t, hs, js=js, a_re=a_re, a_im=a_im, base=base):
                h_re, h_im = hs
                n_re, n_im = [], []
                for k, j in enumerate(js):
                    idx = pl.ds(base + t, SUBLANES, stride=pitch)
                    b_re = bre_scr[j, idx, :]
                    b_im = bim_scr[j, idx, :]
                    r = a_re[k] * h_re[k] - a_im[k] * h_im[k] + b_re
                    i = a_re[k] * h_im[k] + a_im[k] * h_re[k] + b_im
                    bre_scr[j, idx, :] = r
                    bim_scr[j, idx, :] = i
                    n_re.append(r)
                    n_im.append(i)
                return tuple(n_re), tuple(n_im)

            h_re, h_im = lax.fori_loop(0, tt, step, (h_re, h_im), unroll=8)
            for k, j in enumerate(js):
                hre_ref[g * SUBLANES:(g + 1) * SUBLANES, j * LANES:(j + 1) * LANES] = h_re[k]
                him_ref[g * SUBLANES:(g + 1) * SUBLANES, j * LANES:(j + 1) * LANES] = h_im[k]

    lb = lb_ref[...]
    seg_mask = mask_ref[...] > 0.5
    ones_t = jnp.ones((tt, LANES), F32)

    def phase3(c, carry):
        r0 = pl.multiple_of(c * CHUNK, CHUNK)
        s0 = pl.multiple_of(c * cpitch, SUBLANES)
        rows = pl.ds(r0, CHUNK)

        ys = []
        for hf in range(2):
            parts = [bre_scr[hf * HALF_SLABS + jj, pl.ds(s0, CHUNK), :] for jj in range(HALF_SLABS)]
            parts += [bim_scr[hf * HALF_SLABS + jj, pl.ds(s0, CHUNK), :] for jj in range(HALF_SLABS)]
            hcat = jnp.concatenate(parts, axis=1).astype(BF16)
            ys.append(_dot(hcat, wc_ref[hf]))
        u = proj_scr[rows, 0:S5_WIDTH]
        y = jnp.concatenate(ys, axis=1) + dskip_ref[...] * u
        gact = _gelu_tanh(y)
        out5 = gact * _sigmoid(_dot(gact.astype(BF16), wglu_ref[...]))

        q_all = proj_scr[rows, S5_WIDTH:S5_WIDTH + HG_WIDTH]
        fz = proj_scr[rows, S5_WIDTH + HG_WIDTH:S5_WIDTH + 2 * HG_WIDTH]
        iv_all = proj_scr[rows, S5_WIDTH + 2 * HG_WIDTH:S5_WIDTH + 3 * HG_WIDTH]
        og_all = proj_scr[rows, S5_WIDTH + 3 * HG_WIDTH:S5_WIDTH + 4 * HG_WIDTH]
        f = lb + (1.0 - lb) * _sigmoid(fz)
        lf_all = jnp.log(f)
        k_all = 1.0 - f
        g_all = jnp.dot(mask_ref[...], lf_all, precision=HIGHEST, preferred_element_type=F32)

        yh = []
        for h in range(HG_HEADS):
            cs = slice(h * HG_DIM, (h + 1) * HG_DIM)
            gh, qh, kh, vh, lfh = g_all[:, cs], q_all[:, cs], k_all[:, cs], iv_all[:, cs], lf_all[:, cs]
            vb = vh.astype(BF16)

            if tt == CHUNK:
                o_rows = []
                for blk in range(CHUNK // SUB):
                    lo, hi = blk * SUB, (blk + 1) * SUB
                    ref_row = gh[lo + SUB // 2:lo + SUB // 2 + 1, :]
                    qs = (qh[lo:hi] * jnp.exp(gh[lo:hi] - ref_row)).astype(BF16)
                    ks = (kh[:hi] * jnp.exp(ref_row - gh[:hi])).astype(BF16)
                    causal = (lax.broadcasted_iota(jnp.int32, (SUB, hi), 1)
                              <= lax.broadcasted_iota(jnp.int32, (SUB, hi), 0) + lo)
                    att = jnp.where(causal, _dot_nt(qs, ks), 0.0)
                    o_rows.append(_dot(att.astype(BF16), vb[:hi]))
                o_intra = jnp.concatenate(o_rows, axis=0)
            else:
                g3 = gh.reshape(bpc, tt, HG_DIM)
                ref3 = jnp.broadcast_to(g3[:, tt // 2:tt // 2 + 1, :], (bpc, tt, HG_DIM))
                ref_rows = ref3.reshape(CHUNK, HG_DIM)
                qs = (qh * jnp.exp(gh - ref_rows)).astype(BF16)
                ks = (kh * jnp.exp(ref_rows - gh)).astype(BF16)
                att = jnp.where(seg_mask, _dot_nt(qs, ks), 0.0)
                o_intra = _dot(att.astype(BF16), vb)

            qg = (qh * jnp.exp(gh)).astype(BF16)
            o_inter = []
            for bb in range(bpc):
                rs = slice(bb * tt, (bb + 1) * tt)
                b_idx = c * bpc + bb
                if has_state:
                    s_old = s0_ref[b_idx, h]
                else:
                    s_old = s_ref[b_idx, h]
                o_inter.append(_dot(qg[rs], s_old.astype(BF16)))
                g_last = gh[bb * tt + tt - 1:bb * tt + tt, :]
                kd = (kh[rs] * jnp.exp(g_last - gh[rs])).astype(BF16)
                dec = _dot_tn(lfh[rs], ones_t, precision=HIGHEST)
                s_ref[b_idx, h] = jnp.exp(dec) * s_old + _dot_tn(kd, vb[rs])
            o_inter = o_inter[0] if bpc == 1 else jnp.concatenate(o_inter, axis=0)

            o = o_intra + o_inter
            o = _rms(o, gnorm_ref[...])
            ogh = og_all[:, cs]
            yh.append(o * (ogh * _sigmoid(ogh)))

        mix = jnp.concatenate([out5] + yh, axis=1).astype(BF16)
        xc = x_ref[pl.ds(c * bpc, bpc)].reshape(CHUNK, D_MODEL)
        x1 = xc + _dot(mix, wout_ref[...])
        x1_ref[pl.ds(c * bpc, bpc)] = x1.reshape(bpc, tt, D_MODEL)
        return carry

    lax.fori_loop(0, nchunks, phase3, 0)


def _const_spec(shape):
    nd = len(shape)
    return pl.BlockSpec(shape, lambda i, j, _nd=nd: (0,) * _nd, pipeline_mode=pl.Buffered(1))


def _mixer(x, state, w, *, nb, tt):
    B, T, _ = x.shape
    has_state = state is not None
    grid = (B // nb, T // tt)
    bpc = CHUNK // tt
    nchunks = nb // bpc
    pitch = tt + SUBLANES if tt == CHUNK else tt
    scr_rows = nchunks * pitch * bpc

    consts = [w["norm_mix"], w["w_in"], w["wb"], w["a_re"], w["a_im"], w["wc"], w["d_skip"],
              w["w_glu"], w["lb"], w["g_norm"], w["w_out"], w["mask_" + str(tt)]]
    in_specs = [pl.BlockSpec((nb, tt, D_MODEL), lambda i, j: (i, j, 0))]
    args = [x]
    if has_state:
        in_specs += [pl.BlockSpec((nb, S5_FLAT), lambda i, j: (i, 0)),
                     pl.BlockSpec((nb, S5_FLAT), lambda i, j: (i, 0)),
                     pl.BlockSpec((nb, HG_HEADS, HG_DIM, HG_DIM), lambda i, j: (i, 0, 0, 0))]
        args += list(state)
    in_specs += [_const_spec(c.shape) for c in consts]
    args += consts

    out_shape = (jax.ShapeDtypeStruct((B, T, D_MODEL), F32),
                 jax.ShapeDtypeStruct((B, S5_FLAT), F32),
                 jax.ShapeDtypeStruct((B, S5_FLAT), F32),
                 jax.ShapeDtypeStruct((B, HG_HEADS, HG_DIM, HG_DIM), F32))
    out_specs = (pl.BlockSpec((nb, tt, D_MODEL), lambda i, j: (i, j, 0)),
                 pl.BlockSpec((nb, S5_FLAT), lambda i, j: (i, 0)),
                 pl.BlockSpec((nb, S5_FLAT), lambda i, j: (i, 0)),
                 pl.BlockSpec((nb, HG_HEADS, HG_DIM, HG_DIM), lambda i, j: (i, 0, 0, 0)))
    scratch = [pltpu.VMEM((nb * tt, IN_COLS), F32),
               pltpu.VMEM((N_SLABS, scr_rows, LANES), F32),
               pltpu.VMEM((N_SLABS, scr_rows, LANES), F32)]
    return pl.pallas_call(
        functools.partial(_mixer_kernel, nb=nb, tt=tt, has_state=has_state),
        grid=grid, in_specs=in_specs, out_specs=out_specs, out_shape=out_shape,
        scratch_shapes=scratch,
        compiler_params=pltpu.CompilerParams(
            dimension_semantics=("arbitrary", "arbitrary"), vmem_limit_bytes=VMEM_LIMIT),
        name="mixer_t%d" % tt,
    )(*args)


FF_CHUNK = 256


def _ffn_kernel(x_ref, nffn_ref, wg_ref, wu_ref, wd_ref, nfin_ref, y_ref):
    x = x_ref[...]
    h2 = _rms(x, nffn_ref[...]).astype(BF16)
    d_ff = wg_ref.shape[1]
    acc = x
    for f0 in range(0, d_ff, FF_CHUNK):
        gate = _dot(h2, wg_ref[:, f0:f0 + FF_CHUNK])
        up = _dot(h2, wu_ref[:, f0:f0 + FF_CHUNK])
        act = (gate * _sigmoid(gate) * up).astype(BF16)
        acc = acc + _dot(act, wd_ref[f0:f0 + FF_CHUNK, :])
    y_ref[...] = _rms(acc, nfin_ref[...])


def _ffn(x2d, w, *, tm):
    n = x2d.shape[0]
    d_ff = w["w_gate"].shape[1]

    def cspec(shape):
        return pl.BlockSpec(shape, lambda i: (0, 0), pipeline_mode=pl.Buffered(1))

    return pl.pallas_call(
        _ffn_kernel,
        grid=(n // tm,),
        in_specs=[pl.BlockSpec((tm, D_MODEL), lambda i: (i, 0)),
                  cspec((1, D_MODEL)), cspec((D_MODEL, d_ff)), cspec((D_MODEL, d_ff)),
                  cspec((d_ff, D_MODEL)), cspec((1, D_MODEL))],
        out_specs=pl.BlockSpec((tm, D_MODEL), lambda i: (i, 0)),
        out_shape=jax.ShapeDtypeStruct((n, D_MODEL), F32),
        compiler_params=pltpu.CompilerParams(
            dimension_semantics=("arbitrary",), vmem_limit_bytes=VMEM_LIMIT),
        name="ffn",
    )(x2d, w["norm_ffn"], w["w_gate"], w["w_up"], w["w_down"], w["norm_final"])


def _block_diag_halves(blocks):
    g, r, c = blocks.shape
    half = g // 2
    eye = jnp.eye(half, dtype=blocks.dtype)
    b = blocks.reshape(2, half, r, c)
    return jnp.einsum("hgrc,gk->hgrkc", b, eye).reshape(2, half * r, half * c)


def _segment_mask(tt):
    r = jnp.arange(CHUNK)
    same = (r[:, None] // tt) == (r[None, :] // tt)
    return (same & (r[:, None] >= r[None, :])).astype(F32)


def _prepare(lb_param, norm_mix, w_in, s5_a_re, s5_a_im, s5_log_dt, s5_b_re, s5_b_im, s5_c_re,
             s5_c_im, s5_d, s5_w_glu, hg_norm, w_out, norm_ffn, w_gate, w_up, w_down, norm_final):
    l = 0
    a_re = s5_a_re[l].astype(F32)
    a_im = s5_a_im[l].astype(F32)
    dt = jnp.exp(s5_log_dt[l].astype(F32))[:, None]
    mag = jnp.exp(a_re * dt)
    ab_re = mag * jnp.cos(a_im * dt)
    ab_im = mag * jnp.sin(a_im * dt)
    den = a_re * a_re + a_im * a_im
    nr = ab_re - 1.0
    ni = ab_im
    f_re = (nr * a_re + ni * a_im) / den
    f_im = (ni * a_re - nr * a_im) / den
    b_re = s5_b_re[l].astype(F32)
    b_im = s5_b_im[l].astype(F32)
    bb_re = f_re[..., None] * b_re - f_im[..., None] * b_im
    bb_im = f_re[..., None] * b_im + f_im[..., None] * b_re
    wb = jnp.concatenate([_block_diag_halves(bb_re.transpose(0, 2, 1)),
                          _block_diag_halves(bb_im.transpose(0, 2, 1))], axis=2).astype(BF16)
    wc = jnp.concatenate([_block_diag_halves(s5_c_re[l].astype(F32).transpose(0, 2, 1)),
                          _block_diag_halves(-s5_c_im[l].astype(F32).transpose(0, 2, 1))],
                         axis=1).astype(BF16)
    lb_all = jnp.cumsum(jax.nn.softmax(lb_param.astype(F32), axis=0), axis=0)
    return {
        "norm_mix": norm_mix[l].reshape(1, D_MODEL).astype(F32),
        "w_in": w_in[l].astype(BF16),
        "wb": wb,
        "a_re": ab_re.reshape(1, S5_FLAT),
        "a_im": ab_im.reshape(1, S5_FLAT),
        "wc": wc,
        "d_skip": s5_d[l].reshape(1, S5_WIDTH).astype(F32),
        "w_glu": s5_w_glu[l].astype(BF16),
        "lb": lb_all[l].reshape(1, HG_WIDTH),
        "g_norm": hg_norm[l].reshape(1, HG_DIM).astype(F32),
        "w_out": w_out[l].astype(BF16),
        "mask_64": _segment_mask(64),
        "mask_8": _segment_mask(8),
        "norm_ffn": norm_ffn[l].reshape(1, D_MODEL).astype(F32),
        "w_gate": w_gate[l].astype(BF16),
        "w_up": w_up[l].astype(BF16),
        "w_down": w_down[l].astype(BF16),
        "norm_final": norm_final.reshape(1, D_MODEL).astype(F32),
    }


def _trunk(x, state, w, *, nb, tt, tm):
    B, T, _ = x.shape
    x1, h_re, h_im, s_new = _mixer(x, state, w, nb=nb, tt=tt)
    y = _ffn(x1.reshape(B * T, D_MODEL), w, tm=tm).reshape(B, T, D_MODEL)
    return (y, h_re.reshape(1, B, S5_GROUPS, S5_STATE), h_im.reshape(1, B, S5_GROUPS, S5_STATE),
            s_new.reshape(1, B, HG_HEADS, HG_DIM, HG_DIM))


def kernel(x_prompt, x_sample, state_s5_re, state_s5_im, state_hgrn, lb_param, norm_mix, w_in,
           s5_a_re, s5_a_im, s5_log_dt, s5_b_re, s5_b_im, s5_c_re, s5_c_im, s5_d, s5_w_glu,
           hg_norm, w_out, norm_ffn, w_gate, w_up, w_down, norm_final):
    assert norm_mix.shape[0] == 1, "single-layer trunk"
    w = _prepare(lb_param, norm_mix, w_in, s5_a_re, s5_a_im, s5_log_dt, s5_b_re, s5_b_im,
                 s5_c_re, s5_c_im, s5_d, s5_w_glu, hg_norm, w_out, norm_ffn, w_gate, w_up,
                 w_down, norm_final)
    bs = x_sample.shape[0]
    y_p, p_re, p_im, p_hg = _trunk(x_prompt, None, w, nb=8, tt=64, tm=512)
    st = (state_s5_re[0].reshape(bs, S5_FLAT), state_s5_im[0].reshape(bs, S5_FLAT), state_hgrn[0])
    y_s, s_re, s_im, s_hg = _trunk(x_sample, st, w, nb=16, tt=8, tm=512)
    return (y_p, y_s, p_re, p_im, p_hg, s_re, s_im, s_hg)
```

```python
import functools
import math

import jax
import jax.numpy as jnp
from jax import lax
from jax.experimental import pallas as pl
from jax.experimental.pallas import tpu as pltpu

F32 = jnp.float32
BF16 = jnp.bfloat16
HIGHEST = lax.Precision.HIGHEST

D_MODEL = 1024
S5_WIDTH = 512
S5_GROUP = 16
S5_GROUPS = 32
S5_STATE = 64
S5_FLAT = S5_GROUPS * S5_STATE
HG_WIDTH = 512
HG_DIM = 128
HG_HEADS = 4
IN_COLS = S5_WIDTH + 4 * HG_WIDTH
EPS = 1e-6

LANES = 128
SUBLANES = 8
CHUNK = 64
N_SLABS = S5_FLAT // LANES
HALF_U = S5_WIDTH // 2
HALF_SLABS = N_SLABS // 2
SUB = 16
PROJ_COLS = 512
VMEM_LIMIT = 56 * 1024 * 1024


def _dot(a, b):
    return jnp.dot(a, b, preferred_element_type=F32)


def _dot_nt(a, b):
    return lax.dot_general(a, b, (((1,), (1,)), ((), ())), preferred_element_type=F32)


def _dot_tn(a, b, precision=None):
    return lax.dot_general(a, b, (((0,), (0,)), ((), ())), precision=precision,
                           preferred_element_type=F32)


def _rms(x, g):
    return x * lax.rsqrt(jnp.mean(x * x, axis=-1, keepdims=True) + EPS) * g


def _sigmoid(x):
    return jax.nn.sigmoid(x)


def _gelu_tanh(x):
    c = math.sqrt(2.0 / math.pi)
    return 0.5 * x * (1.0 + jnp.tanh(c * (x + 0.044715 * (x * x * x))))


def _mixer_kernel(*refs, nb, tt, has_state):
    it = iter(refs)
    x_ref = next(it)
    if has_state:
        h0re_ref, h0im_ref, s0_ref = next(it), next(it), next(it)
    (norm_ref, win_ref, wb_ref, are_ref, aim_ref, wc_ref, dskip_ref, wglu_ref, lb_ref,
     gnorm_ref, wout_ref, mask_ref) = (next(it) for _ in range(12))
    x1_ref, hre_ref, him_ref, s_ref = (next(it) for _ in range(4))
    proj_scr, bre_scr, bim_scr, mix_scr = (next(it) for _ in range(4))

    bpc = CHUNK // tt
    nchunks = nb // bpc
    pitch = tt + SUBLANES if tt == CHUNK else tt
    cpitch = pitch * bpc
    ngroups = nb // SUBLANES

    tstep = pl.program_id(1)

    @pl.when(tstep == 0)
    def _init():
        if has_state:
            hre_ref[...] = h0re_ref[...]
            him_ref[...] = h0im_ref[...]
        else:
            hre_ref[...] = jnp.zeros_like(hre_ref)
            him_ref[...] = jnp.zeros_like(him_ref)
            s_ref[...] = jnp.zeros_like(s_ref)

    rows_tile = nb * tt
    scr_row0 = [c * cpitch for c in range(nchunks)]

    hn = _rms(x_ref[...].reshape(rows_tile, D_MODEL), norm_ref[...]).astype(BF16)
    for n0 in range(0, IN_COLS, PROJ_COLS):
        proj_scr[:, n0:n0 + PROJ_COLS] = _dot(hn, win_ref[:, n0:n0 + PROJ_COLS])
    ub = proj_scr[:, 0:S5_WIDTH].astype(BF16)
    for hf in range(2):
        ubh = ub[:, hf * HALF_U:(hf + 1) * HALF_U]
        for part, scr in enumerate((bre_scr, bim_scr)):
            cols = slice(part * HALF_SLABS * LANES, (part + 1) * HALF_SLABS * LANES)
            bu = _dot(ubh, wb_ref[hf, :, cols])
            for jj in range(HALF_SLABS):
                for c in range(nchunks):
                    scr[hf * HALF_SLABS + jj, scr_row0[c]:scr_row0[c] + CHUNK, :] = (
                        bu[c * CHUNK:(c + 1) * CHUNK, jj * LANES:(jj + 1) * LANES])

    slabs_per_pass = 4
    for g in range(ngroups):
        base = g * SUBLANES * pitch
        for sp in range(N_SLABS // slabs_per_pass):
            js = [sp * slabs_per_pass + k for k in range(slabs_per_pass)]
            a_re = [jnp.broadcast_to(are_ref[:, j * LANES:(j + 1) * LANES], (SUBLANES, LANES)) for j in js]
            a_im = [jnp.broadcast_to(aim_ref[:, j * LANES:(j + 1) * LANES], (SUBLANES, LANES)) for j in js]
            h_re = tuple(hre_ref[g * SUBLANES:(g + 1) * SUBLANES, j * LANES:(j + 1) * LANES] for j in js)
            h_im = tuple(him_ref[g * SUBLANES:(g + 1) * SUBLANES, j * LANES:(j + 1) * LANES] for j in js)

            def step(t, hs, js=js, a_re=a_re, a_im=a_im, base=base):
                h_re, h_im = hs
                n_re, n_im = [], []
                for k, j in enumerate(js):
                    idx = pl.ds(base + t, SUBLANES, stride=pitch)
                    b_re = bre_scr[j, idx, :]
                    b_im = bim_scr[j, idx, :]
                    r = a_re[k] * h_re[k] - a_im[k] * h_im[k] + b_re
                    i = a_re[k] * h_im[k] + a_im[k] * h_re[k] + b_im
                    bre_scr[j, idx, :] = r
                    bim_scr[j, idx, :] = i
                    n_re.append(r)
                    n_im.append(i)
                return tuple(n_re), tuple(n_im)

            h_re, h_im = lax.fori_loop(0, tt, step, (h_re, h_im), unroll=8)
            for k, j in enumerate(js):
                hre_ref[g * SUBLANES:(g + 1) * SUBLANES, j * LANES:(j + 1) * LANES] = h_re[k]
                him_ref[g * SUBLANES:(g + 1) * SUBLANES, j * LANES:(j + 1) * LANES] = h_im[k]

    ys = []
    for hf in range(2):
        parts = []
        for scr in (bre_scr, bim_scr):
            for jj in range(HALF_SLABS):
                j = hf * HALF_SLABS + jj
                parts.append(jnp.concatenate(
                    [scr[j, scr_row0[c]:scr_row0[c] + CHUNK, :] for c in range(nchunks)], axis=0))
        hcat = jnp.concatenate(parts, axis=1).astype(BF16)
        ys.append(_dot(hcat, wc_ref[hf]))
    y = jnp.concatenate(ys, axis=1) + dskip_ref[...] * proj_scr[:, 0:S5_WIDTH]
    gact = _gelu_tanh(y)
    out5 = gact * _sigmoid(_dot(gact.astype(BF16), wglu_ref[...]))
    mix_scr[:, 0:S5_WIDTH] = out5.astype(BF16)

    lb = lb_ref[...]
    seg_mask = mask_ref[...] > 0.5
    ones_t = jnp.ones((tt, LANES), F32)
    heads = range(HG_HEADS)

    def phase3(c, carry):
        r0 = pl.multiple_of(c * CHUNK, CHUNK)
        rows = pl.ds(r0, CHUNK)
        q_all = proj_scr[rows, S5_WIDTH:S5_WIDTH + HG_WIDTH]
        fz = proj_scr[rows, S5_WIDTH + HG_WIDTH:S5_WIDTH + 2 * HG_WIDTH]
        iv_all = proj_scr[rows, S5_WIDTH + 2 * HG_WIDTH:S5_WIDTH + 3 * HG_WIDTH]
        og_all = proj_scr[rows, S5_WIDTH + 3 * HG_WIDTH:S5_WIDTH + 4 * HG_WIDTH]
        f = lb + (1.0 - lb) * _sigmoid(fz)
        lf_all = jnp.log(f)
        k_all = 1.0 - f
        g_all = jnp.dot(mask_ref[...], lf_all, precision=HIGHEST, preferred_element_type=F32)
        vb_all = iv_all.astype(BF16)
        qg_all = (q_all * jnp.exp(g_all)).astype(BF16)
        hs = [slice(h * HG_DIM, (h + 1) * HG_DIM) for h in heads]

        qs, ks = [], []
        for h in heads:
            gh, qh, kh = g_all[:, hs[h]], q_all[:, hs[h]], k_all[:, hs[h]]
            if tt == CHUNK:
                row_id = lax.broadcasted_iota(jnp.int32, (CHUNK, HG_DIM), 0)
                for blk in range(CHUNK // SUB):
                    lo, hi = blk * SUB, (blk + 1) * SUB
                    ref_row = gh[lo + SUB // 2:lo + SUB // 2 + 1, :]
                    qs.append((qh[lo:hi] * jnp.exp(gh[lo:hi] - ref_row)).astype(BF16))
                    ks.append((kh * jnp.exp(jnp.where(row_id < hi, ref_row - gh, 0.0))).astype(BF16))
            else:
                g3 = gh.reshape(bpc, tt, HG_DIM)
                ref3 = jnp.broadcast_to(g3[:, tt // 2:tt // 2 + 1, :], (bpc, tt, HG_DIM))
                ref_rows = ref3.reshape(CHUNK, HG_DIM)
                qs.append((qh * jnp.exp(gh - ref_rows)).astype(BF16))
                ks.append((kh * jnp.exp(ref_rows - gh)).astype(BF16))

        att = [_dot_nt(a, b) for a, b in zip(qs, ks)]

        o_inter = []
        for h in heads:
            gh, kh, lfh = g_all[:, hs[h]], k_all[:, hs[h]], lf_all[:, hs[h]]
            oi = []
            for bb in range(bpc):
                rs = slice(bb * tt, (bb + 1) * tt)
                b_idx = c * bpc + bb
                s_old = s0_ref[b_idx, h] if has_state else s_ref[b_idx, h]
                oi.append(_dot(qg_all[rs, hs[h]], s_old.astype(BF16)))
                g_last = gh[bb * tt + tt - 1:bb * tt + tt, :]
                kd = (kh[rs] * jnp.exp(g_last - gh[rs])).astype(BF16)
                dec = _dot_tn(lfh[rs], ones_t, precision=HIGHEST)
                s_ref[b_idx, h] = jnp.exp(dec) * s_old + _dot_tn(kd, vb_all[rs, hs[h]])
            o_inter.append(oi[0] if bpc == 1 else jnp.concatenate(oi, axis=0))

        nblk = len(att) // HG_HEADS
        for h in heads:
            sc = att[h * nblk:(h + 1) * nblk]
            sc = sc[0] if nblk == 1 else jnp.concatenate(sc, axis=0)
            sc = jnp.where(seg_mask, sc, 0.0).astype(BF16)
            o = _dot(sc, vb_all[:, hs[h]]) + o_inter[h]
            o = _rms(o, gnorm_ref[...])
            ogh = og_all[:, hs[h]]
            mix_scr[rows, S5_WIDTH + h * HG_DIM:S5_WIDTH + (h + 1) * HG_DIM] = (
                o * (ogh * _sigmoid(ogh))).astype(BF16)
        return carry

    lax.fori_loop(0, nchunks, phase3, 0)

    x1 = x_ref[...].reshape(rows_tile, D_MODEL) + _dot(mix_scr[...], wout_ref[...])
    x1_ref[...] = x1.reshape(nb, tt, D_MODEL)


def _const_spec(shape):
    nd = len(shape)
    return pl.BlockSpec(shape, lambda i, j, _nd=nd: (0,) * _nd, pipeline_mode=pl.Buffered(1))


def _mixer(x, state, w, *, nb, tt):
    B, T, _ = x.shape
    has_state = state is not None
    grid = (B // nb, T // tt)
    bpc = CHUNK // tt
    nchunks = nb // bpc
    pitch = tt + SUBLANES if tt == CHUNK else tt
    scr_rows = nchunks * pitch * bpc

    consts = [w["norm_mix"], w["w_in"], w["wb"], w["a_re"], w["a_im"], w["wc"], w["d_skip"],
              w["w_glu"], w["lb"], w["g_norm"], w["w_out"], w["mask_" + str(tt)]]
    in_specs = [pl.BlockSpec((nb, tt, D_MODEL), lambda i, j: (i, j, 0))]
    args = [x]
    if has_state:
        in_specs += [pl.BlockSpec((nb, S5_FLAT), lambda i, j: (i, 0)),
                     pl.BlockSpec((nb, S5_FLAT), lambda i, j: (i, 0)),
                     pl.BlockSpec((nb, HG_HEADS, HG_DIM, HG_DIM), lambda i, j: (i, 0, 0, 0))]
        args += list(state)
    in_specs += [_const_spec(c.shape) for c in consts]
    args += consts

    out_shape = (jax.ShapeDtypeStruct((B, T, D_MODEL), F32),
                 jax.ShapeDtypeStruct((B, S5_FLAT), F32),
                 jax.ShapeDtypeStruct((B, S5_FLAT), F32),
                 jax.ShapeDtypeStruct((B, HG_HEADS, HG_DIM, HG_DIM), F32))
    out_specs = (pl.BlockSpec((nb, tt, D_MODEL), lambda i, j: (i, j, 0)),
                 pl.BlockSpec((nb, S5_FLAT), lambda i, j: (i, 0)),
                 pl.BlockSpec((nb, S5_FLAT), lambda i, j: (i, 0)),
                 pl.BlockSpec((nb, HG_HEADS, HG_DIM, HG_DIM), lambda i, j: (i, 0, 0, 0)))
    scratch = [pltpu.VMEM((nb * tt, IN_COLS), F32),
               pltpu.VMEM((N_SLABS, scr_rows, LANES), F32),
               pltpu.VMEM((N_SLABS, scr_rows, LANES), F32),
               pltpu.VMEM((nb * tt, D_MODEL), BF16)]
    return pl.pallas_call(
        functools.partial(_mixer_kernel, nb=nb, tt=tt, has_state=has_state),
        grid=grid, in_specs=in_specs, out_specs=out_specs, out_shape=out_shape,
        scratch_shapes=scratch,
        compiler_params=pltpu.CompilerParams(
            dimension_semantics=("arbitrary", "arbitrary"), vmem_limit_bytes=VMEM_LIMIT),
        name="mixer_t%d" % tt,
    )(*args)


FF_CHUNK = 256


def _ffn_kernel(x_ref, nffn_ref, wg_ref, wu_ref, wd_ref, nfin_ref, y_ref):
    x = x_ref[...]
    h2 = _rms(x, nffn_ref[...]).astype(BF16)
    d_ff = wg_ref.shape[1]
    acc = x
    for f0 in range(0, d_ff, FF_CHUNK):
        gate = _dot(h2, wg_ref[:, f0:f0 + FF_CHUNK])
        up = _dot(h2, wu_ref[:, f0:f0 + FF_CHUNK])
        act = (gate * _sigmoid(gate) * up).astype(BF16)
        acc = acc + _dot(act, wd_ref[f0:f0 + FF_CHUNK, :])
    y_ref[...] = _rms(acc, nfin_ref[...])


def _ffn(x2d, w, *, tm):
    n = x2d.shape[0]
    d_ff = w["w_gate"].shape[1]

    def cspec(shape):
        return pl.BlockSpec(shape, lambda i: (0, 0), pipeline_mode=pl.Buffered(1))

    return pl.pallas_call(
        _ffn_kernel,
        grid=(n // tm,),
        in_specs=[pl.BlockSpec((tm, D_MODEL), lambda i: (i, 0)),
                  cspec((1, D_MODEL)), cspec((D_MODEL, d_ff)), cspec((D_MODEL, d_ff)),
                  cspec((d_ff, D_MODEL)), cspec((1, D_MODEL))],
        out_specs=pl.BlockSpec((tm, D_MODEL), lambda i: (i, 0)),
        out_shape=jax.ShapeDtypeStruct((n, D_MODEL), F32),
        compiler_params=pltpu.CompilerParams(
            dimension_semantics=("arbitrary",), vmem_limit_bytes=VMEM_LIMIT),
        name="ffn",
    )(x2d, w["norm_ffn"], w["w_gate"], w["w_up"], w["w_down"], w["norm_final"])


def _block_diag_halves(blocks):
    g, r, c = blocks.shape
    half = g // 2
    eye = jnp.eye(half, dtype=blocks.dtype)
    b = blocks.reshape(2, half, r, c)
    return jnp.einsum("hgrc,gk->hgrkc", b, eye).reshape(2, half * r, half * c)


def _segment_mask(tt):
    r = jnp.arange(CHUNK)
    same = (r[:, None] // tt) == (r[None, :] // tt)
    return (same & (r[:, None] >= r[None, :])).astype(F32)


def _prepare(lb_param, norm_mix, w_in, s5_a_re, s5_a_im, s5_log_dt, s5_b_re, s5_b_im, s5_c_re,
             s5_c_im, s5_d, s5_w_glu, hg_norm, w_out, norm_ffn, w_gate, w_up, w_down, norm_final):
    l = 0
    a_re = s5_a_re[l].astype(F32)
    a_im = s5_a_im[l].astype(F32)
    dt = jnp.exp(s5_log_dt[l].astype(F32))[:, None]
    mag = jnp.exp(a_re * dt)
    ab_re = mag * jnp.cos(a_im * dt)
    ab_im = mag * jnp.sin(a_im * dt)
    den = a_re * a_re + a_im * a_im
    nr = ab_re - 1.0
    ni = ab_im
    f_re = (nr * a_re + ni * a_im) / den
    f_im = (ni * a_re - nr * a_im) / den
    b_re = s5_b_re[l].astype(F32)
    b_im = s5_b_im[l].astype(F32)
    bb_re = f_re[..., None] * b_re - f_im[..., None] * b_im
    bb_im = f_re[..., None] * b_im + f_im[..., None] * b_re
    wb = jnp.concatenate([_block_diag_halves(bb_re.transpose(0, 2, 1)),
                          _block_diag_halves(bb_im.transpose(0, 2, 1))], axis=2).astype(BF16)
    wc = jnp.concatenate([_block_diag_halves(s5_c_re[l].astype(F32).transpose(0, 2, 1)),
                          _block_diag_halves(-s5_c_im[l].astype(F32).transpose(0, 2, 1))],
                         axis=1).astype(BF16)
    lb_all = jnp.cumsum(jax.nn.softmax(lb_param.astype(F32), axis=0), axis=0)
    return {
        "norm_mix": norm_mix[l].reshape(1, D_MODEL).astype(F32),
        "w_in": w_in[l].astype(BF16),
        "wb": wb,
        "a_re": ab_re.reshape(1, S5_FLAT),
        "a_im": ab_im.reshape(1, S5_FLAT),
        "wc": wc,
        "d_skip": s5_d[l].reshape(1, S5_WIDTH).astype(F32),
        "w_glu": s5_w_glu[l].astype(BF16),
        "lb": lb_all[l].reshape(1, HG_WIDTH),
        "g_norm": hg_norm[l].reshape(1, HG_DIM).astype(F32),
        "w_out": w_out[l].astype(BF16),
        "mask_64": _segment_mask(64),
        "mask_8": _segment_mask(8),
        "norm_ffn": norm_ffn[l].reshape(1, D_MODEL).astype(F32),
        "w_gate": w_gate[l].astype(BF16),
        "w_up": w_up[l].astype(BF16),
        "w_down": w_down[l].astype(BF16),
        "norm_final": norm_final.reshape(1, D_MODEL).astype(F32),
    }


def _trunk(x, state, w, *, nb, tt, tm):
    B, T, _ = x.shape
    x1, h_re, h_im, s_new = _mixer(x, state, w, nb=nb, tt=tt)
    y = _ffn(x1.reshape(B * T, D_MODEL), w, tm=tm).reshape(B, T, D_MODEL)
    return (y, h_re.reshape(1, B, S5_GROUPS, S5_STATE), h_im.reshape(1, B, S5_GROUPS, S5_STATE),
            s_new.reshape(1, B, HG_HEADS, HG_DIM, HG_DIM))


def kernel(x_prompt, x_sample, state_s5_re, state_s5_im, state_hgrn, lb_param, norm_mix, w_in,
           s5_a_re, s5_a_im, s5_log_dt, s5_b_re, s5_b_im, s5_c_re, s5_c_im, s5_d, s5_w_glu,
           hg_norm, w_out, norm_ffn, w_gate, w_up, w_down, norm_final):
    assert norm_mix.shape[0] == 1, "single-layer trunk"
    w = _prepare(lb_param, norm_mix, w_in, s5_a_re, s5_a_im, s5_log_dt, s5_b_re, s5_b_im,
                 s5_c_re, s5_c_im, s5_d, s5_w_glu, hg_norm, w_out, norm_ffn, w_gate, w_up,
                 w_down, norm_final)
    bs = x_sample.shape[0]
    y_p, p_re, p_im, p_hg = _trunk(x_prompt, None, w, nb=8, tt=64, tm=512)
    st = (state_s5_re[0].reshape(bs, S5_FLAT), state_s5_im[0].reshape(bs, S5_FLAT), state_hgrn[0])
    y_s, s_re, s_im, s_hg = _trunk(x_sample, st, w, nb=16, tt=8, tm=512)
    return (y_p, y_s, p_re, p_im, p_hg, s_re, s_im, s_hg)
```

```python
import functools
import math

import jax
import jax.numpy as jnp
from jax import lax
from jax.experimental import pallas as pl
from jax.experimental.pallas import tpu as pltpu

F32 = jnp.float32
BF16 = jnp.bfloat16
HIGHEST = lax.Precision.HIGHEST

D_MODEL = 1024
S5_WIDTH = 512
S5_GROUP = 16
S5_GROUPS = 32
S5_STATE = 64
S5_FLAT = S5_GROUPS * S5_STATE
HG_WIDTH = 512
HG_DIM = 128
HG_HEADS = 4
IN_COLS = S5_WIDTH + 4 * HG_WIDTH
EPS = 1e-6

LANES = 128
SUBLANES = 8
CHUNK = 64
N_SLABS = S5_FLAT // LANES
HALF_U = S5_WIDTH // 2
HALF_SLABS = N_SLABS // 2
SUB = 16
PROJ_COLS = 512
VMEM_LIMIT = 56 * 1024 * 1024


def _dot(a, b):
    return jnp.dot(a, b, preferred_element_type=F32)


def _dot_nt(a, b):
    return lax.dot_general(a, b, (((1,), (1,)), ((), ())), preferred_element_type=F32)


def _dot_tn(a, b, precision=None):
    return lax.dot_general(a, b, (((0,), (0,)), ((), ())), precision=precision,
                           preferred_element_type=F32)


def _dot_exact01(m, x):
    hi = x.astype(BF16)
    r1 = x - hi.astype(F32)
    mid = r1.astype(BF16)
    lo = (r1 - mid.astype(F32)).astype(BF16)
    return _dot(m, hi) + _dot(m, mid) + _dot(m, lo)


def _rms(x, g):
    return x * lax.rsqrt(jnp.mean(x * x, axis=-1, keepdims=True) + EPS) * g


def _sigmoid(x):
    return jax.nn.sigmoid(x)


def _gelu_tanh(x):
    c = math.sqrt(2.0 / math.pi)
    return 0.5 * x * (1.0 + jnp.tanh(c * (x + 0.044715 * (x * x * x))))


def _mixer_kernel(*refs, nb, tt, has_state):
    it = iter(refs)
    x_ref = next(it)
    if has_state:
        h0re_ref, h0im_ref, s0_ref = next(it), next(it), next(it)
    (norm_ref, win_ref, wb_ref, are_ref, aim_ref, wc_ref, dskip_ref, wglu_ref, lb_ref,
     gnorm_ref, wout_ref, mask_ref) = (next(it) for _ in range(12))
    x1_ref, hre_ref, him_ref, s_ref = (next(it) for _ in range(4))
    proj_scr, bre_scr, bim_scr, mix_scr = (next(it) for _ in range(4))

    bpc = CHUNK // tt
    nchunks = nb // bpc
    pitch = tt + SUBLANES if tt == CHUNK else tt
    cpitch = pitch * bpc
    ngroups = nb // SUBLANES
    state_t = tt == CHUNK
    assert not (state_t and has_state)

    tstep = pl.program_id(1)

    @pl.when(tstep == 0)
    def _init():
        if has_state:
            hre_ref[...] = h0re_ref[...]
            him_ref[...] = h0im_ref[...]
        else:
            hre_ref[...] = jnp.zeros_like(hre_ref)
            him_ref[...] = jnp.zeros_like(him_ref)
            s_ref[...] = jnp.zeros_like(s_ref)

    rows_tile = nb * tt
    scr_row0 = [c * cpitch for c in range(nchunks)]

    hn = _rms(x_ref[...].reshape(rows_tile, D_MODEL), norm_ref[...]).astype(BF16)
    for n0 in range(0, IN_COLS, PROJ_COLS):
        proj_scr[:, n0:n0 + PROJ_COLS] = _dot(hn, win_ref[:, n0:n0 + PROJ_COLS])
    ub = proj_scr[:, 0:S5_WIDTH].astype(BF16)
    for hf in range(2):
        ubh = ub[:, hf * HALF_U:(hf + 1) * HALF_U]
        for part, scr in enumerate((bre_scr, bim_scr)):
            cols = slice(part * HALF_SLABS * LANES, (part + 1) * HALF_SLABS * LANES)
            bu = _dot(ubh, wb_ref[hf, :, cols])
            for jj in range(HALF_SLABS):
                for c in range(nchunks):
                    scr[hf * HALF_SLABS + jj, scr_row0[c]:scr_row0[c] + CHUNK, :] = (
                        bu[c * CHUNK:(c + 1) * CHUNK, jj * LANES:(jj + 1) * LANES])

    slabs_per_pass = 8
    for g in range(ngroups):
        base = g * SUBLANES * pitch
        for sp in range(N_SLABS // slabs_per_pass):
            js = [sp * slabs_per_pass + k for k in range(slabs_per_pass)]
            a_re = [jnp.broadcast_to(are_ref[:, j * LANES:(j + 1) * LANES], (SUBLANES, LANES)) for j in js]
            a_im = [jnp.broadcast_to(aim_ref[:, j * LANES:(j + 1) * LANES], (SUBLANES, LANES)) for j in js]
            h_re = tuple(hre_ref[g * SUBLANES:(g + 1) * SUBLANES, j * LANES:(j + 1) * LANES] for j in js)
            h_im = tuple(him_ref[g * SUBLANES:(g + 1) * SUBLANES, j * LANES:(j + 1) * LANES] for j in js)

            def step(t, hs, js=js, a_re=a_re, a_im=a_im, base=base):
                h_re, h_im = hs
                n_re, n_im = [], []
                for k, j in enumerate(js):
                    idx = pl.ds(base + t, SUBLANES, stride=pitch)
                    b_re = bre_scr[j, idx, :]
                    b_im = bim_scr[j, idx, :]
                    r = a_re[k] * h_re[k] - a_im[k] * h_im[k] + b_re
                    i = a_re[k] * h_im[k] + a_im[k] * h_re[k] + b_im
                    bre_scr[j, idx, :] = r
                    bim_scr[j, idx, :] = i
                    n_re.append(r)
                    n_im.append(i)
                return tuple(n_re), tuple(n_im)

            h_re, h_im = lax.fori_loop(0, tt, step, (h_re, h_im), unroll=8)
            for k, j in enumerate(js):
                hre_ref[g * SUBLANES:(g + 1) * SUBLANES, j * LANES:(j + 1) * LANES] = h_re[k]
                him_ref[g * SUBLANES:(g + 1) * SUBLANES, j * LANES:(j + 1) * LANES] = h_im[k]

    ys = []
    for hf in range(2):
        parts = []
        for scr in (bre_scr, bim_scr):
            for jj in range(HALF_SLABS):
                j = hf * HALF_SLABS + jj
                parts.append(jnp.concatenate(
                    [scr[j, scr_row0[c]:scr_row0[c] + CHUNK, :] for c in range(nchunks)], axis=0))
        hcat = jnp.concatenate(parts, axis=1).astype(BF16)
        ys.append(_dot(hcat, wc_ref[hf]))
    y = jnp.concatenate(ys, axis=1) + dskip_ref[...] * proj_scr[:, 0:S5_WIDTH]
    gact = _gelu_tanh(y)
    out5 = gact * _sigmoid(_dot(gact.astype(BF16), wglu_ref[...]))
    mix_scr[:, 0:S5_WIDTH] = out5.astype(BF16)

    lb = lb_ref[...]
    seg_mask = mask_ref[...].astype(F32) > 0.5
    ones_t = jnp.ones((tt, LANES), F32)
    heads = range(HG_HEADS)

    def phase3(c, carry):
        r0 = pl.multiple_of(c * CHUNK, CHUNK)
        rows = pl.ds(r0, CHUNK)
        q_all = proj_scr[rows, S5_WIDTH:S5_WIDTH + HG_WIDTH]
        fz = proj_scr[rows, S5_WIDTH + HG_WIDTH:S5_WIDTH + 2 * HG_WIDTH]
        iv_all = proj_scr[rows, S5_WIDTH + 2 * HG_WIDTH:S5_WIDTH + 3 * HG_WIDTH]
        og_all = proj_scr[rows, S5_WIDTH + 3 * HG_WIDTH:S5_WIDTH + 4 * HG_WIDTH]
        f = lb + (1.0 - lb) * _sigmoid(fz)
        lf_all = jnp.log(f)
        k_all = 1.0 - f
        g_all = _dot_exact01(mask_ref[...], lf_all)
        vb_all = iv_all.astype(BF16)
        qg_all = (q_all * jnp.exp(g_all)).astype(BF16)
        hs = [slice(h * HG_DIM, (h + 1) * HG_DIM) for h in heads]

        qs, ks = [], []
        for h in heads:
            gh, qh, kh = g_all[:, hs[h]], q_all[:, hs[h]], k_all[:, hs[h]]
            if tt == CHUNK:
                row_id = lax.broadcasted_iota(jnp.int32, (CHUNK, HG_DIM), 0)
                for blk in range(CHUNK // SUB):
                    lo, hi = blk * SUB, (blk + 1) * SUB
                    ref_row = gh[lo + SUB // 2:lo + SUB // 2 + 1, :]
                    qs.append((qh[lo:hi] * jnp.exp(gh[lo:hi] - ref_row)).astype(BF16))
                    ks.append((kh * jnp.exp(jnp.where(row_id < hi, ref_row - gh, 0.0))).astype(BF16))
            else:
                g3 = gh.reshape(bpc, tt, HG_DIM)
                ref3 = jnp.broadcast_to(g3[:, tt // 2:tt // 2 + 1, :], (bpc, tt, HG_DIM))
                ref_rows = ref3.reshape(CHUNK, HG_DIM)
                qs.append((qh * jnp.exp(gh - ref_rows)).astype(BF16))
                ks.append((kh * jnp.exp(ref_rows - gh)).astype(BF16))

        att = [_dot_nt(a, b) for a, b in zip(qs, ks)]

        o_inter = []
        for h in heads:
            gh, kh, lfh = g_all[:, hs[h]], k_all[:, hs[h]], lf_all[:, hs[h]]
            oi = []
            for bb in range(bpc):
                rs = slice(bb * tt, (bb + 1) * tt)
                b_idx = c * bpc + bb
                s_old = s0_ref[b_idx, h] if has_state else s_ref[b_idx, h]
                g_last = gh[bb * tt + tt - 1:bb * tt + tt, :]
                kd = (kh[rs] * jnp.exp(g_last - gh[rs])).astype(BF16)
                if state_t:
                    oi.append(_dot_nt(qg_all[rs, hs[h]], s_old.astype(BF16)))
                    s_ref[b_idx, h] = jnp.exp(g_last) * s_old + _dot_tn(vb_all[rs, hs[h]], kd)
                else:
                    oi.append(_dot(qg_all[rs, hs[h]], s_old.astype(BF16)))
                    dec = _dot_tn(lfh[rs], ones_t, precision=HIGHEST)
                    s_ref[b_idx, h] = jnp.exp(dec) * s_old + _dot_tn(kd, vb_all[rs, hs[h]])
            o_inter.append(oi[0] if bpc == 1 else jnp.concatenate(oi, axis=0))

        nblk = len(att) // HG_HEADS
        for h in heads:
            sc = att[h * nblk:(h + 1) * nblk]
            sc = sc[0] if nblk == 1 else jnp.concatenate(sc, axis=0)
            sc = jnp.where(seg_mask, sc, 0.0).astype(BF16)
            o = _dot(sc, vb_all[:, hs[h]]) + o_inter[h]
            o = _rms(o, gnorm_ref[...])
            ogh = og_all[:, hs[h]]
            mix_scr[rows, S5_WIDTH + h * HG_DIM:S5_WIDTH + (h + 1) * HG_DIM] = (
                o * (ogh * _sigmoid(ogh))).astype(BF16)
        return carry

    lax.fori_loop(0, nchunks, phase3, 0, unroll=2 if state_t else 1)

    if state_t:
        @pl.when(tstep == pl.num_programs(1) - 1)
        def _untranspose_state():
            for b in range(nb):
                for h in heads:
                    s_ref[b, h] = s_ref[b, h].T

    x1 = x_ref[...].reshape(rows_tile, D_MODEL) + _dot(mix_scr[...], wout_ref[...])
    x1_ref[...] = x1.reshape(nb, tt, D_MODEL)


def _const_spec(shape):
    nd = len(shape)
    return pl.BlockSpec(shape, lambda i, j, _nd=nd: (0,) * _nd, pipeline_mode=pl.Buffered(1))


def _mixer(x, state, w, *, nb, tt):
    B, T, _ = x.shape
    has_state = state is not None
    grid = (B // nb, T // tt)
    bpc = CHUNK // tt
    nchunks = nb // bpc
    pitch = tt + SUBLANES if tt == CHUNK else tt
    scr_rows = nchunks * pitch * bpc

    consts = [w["norm_mix"], w["w_in"], w["wb"], w["a_re"], w["a_im"], w["wc"], w["d_skip"],
              w["w_glu"], w["lb"], w["g_norm"], w["w_out"], w["mask_" + str(tt)]]
    in_specs = [pl.BlockSpec((nb, tt, D_MODEL), lambda i, j: (i, j, 0))]
    args = [x]
    if has_state:
        in_specs += [pl.BlockSpec((nb, S5_FLAT), lambda i, j: (i, 0)),
                     pl.BlockSpec((nb, S5_FLAT), lambda i, j: (i, 0)),
                     pl.BlockSpec((nb, HG_HEADS, HG_DIM, HG_DIM), lambda i, j: (i, 0, 0, 0))]
        args += list(state)
    in_specs += [_const_spec(c.shape) for c in consts]
    args += consts

    out_shape = (jax.ShapeDtypeStruct((B, T, D_MODEL), F32),
                 jax.ShapeDtypeStruct((B, S5_FLAT), F32),
                 jax.ShapeDtypeStruct((B, S5_FLAT), F32),
                 jax.ShapeDtypeStruct((B, HG_HEADS, HG_DIM, HG_DIM), F32))
    out_specs = (pl.BlockSpec((nb, tt, D_MODEL), lambda i, j: (i, j, 0)),
                 pl.BlockSpec((nb, S5_FLAT), lambda i, j: (i, 0)),
                 pl.BlockSpec((nb, S5_FLAT), lambda i, j: (i, 0)),
                 pl.BlockSpec((nb, HG_HEADS, HG_DIM, HG_DIM), lambda i, j: (i, 0, 0, 0)))
    scratch = [pltpu.VMEM((nb * tt, IN_COLS), F32),
               pltpu.VMEM((N_SLABS, scr_rows, LANES), F32),
               pltpu.VMEM((N_SLABS, scr_rows, LANES), F32),
               pltpu.VMEM((nb * tt, D_MODEL), BF16)]
    return pl.pallas_call(
        functools.partial(_mixer_kernel, nb=nb, tt=tt, has_state=has_state),
        grid=grid, in_specs=in_specs, out_specs=out_specs, out_shape=out_shape,
        scratch_shapes=scratch,
        compiler_params=pltpu.CompilerParams(
            dimension_semantics=("arbitrary", "arbitrary"), vmem_limit_bytes=VMEM_LIMIT),
        name="mixer_t%d" % tt,
    )(*args)


FF_CHUNK = 256


def _ffn_kernel(x_ref, nffn_ref, wg_ref, wu_ref, wd_ref, nfin_ref, y_ref):
    x = x_ref[...]
    h2 = _rms(x, nffn_ref[...]).astype(BF16)
    d_ff = wg_ref.shape[1]
    acc = x
    for f0 in range(0, d_ff, FF_CHUNK):
        gate = _dot(h2, wg_ref[:, f0:f0 + FF_CHUNK])
        up = _dot(h2, wu_ref[:, f0:f0 + FF_CHUNK])
        act = (gate * _sigmoid(gate) * up).astype(BF16)
        acc = acc + _dot(act, wd_ref[f0:f0 + FF_CHUNK, :])
    y_ref[...] = _rms(acc, nfin_ref[...])


def _ffn(x2d, w, *, tm):
    n = x2d.shape[0]
    d_ff = w["w_gate"].shape[1]

    def cspec(shape):
        return pl.BlockSpec(shape, lambda i: (0, 0), pipeline_mode=pl.Buffered(1))

    return pl.pallas_call(
        _ffn_kernel,
        grid=(n // tm,),
        in_specs=[pl.BlockSpec((tm, D_MODEL), lambda i: (i, 0)),
                  cspec((1, D_MODEL)), cspec((D_MODEL, d_ff)), cspec((D_MODEL, d_ff)),
                  cspec((d_ff, D_MODEL)), cspec((1, D_MODEL))],
        out_specs=pl.BlockSpec((tm, D_MODEL), lambda i: (i, 0)),
        out_shape=jax.ShapeDtypeStruct((n, D_MODEL), F32),
        compiler_params=pltpu.CompilerParams(
            dimension_semantics=("arbitrary",), vmem_limit_bytes=VMEM_LIMIT),
        name="ffn",
    )(x2d, w["norm_ffn"], w["w_gate"], w["w_up"], w["w_down"], w["norm_final"])


def _block_diag_halves(blocks):
    g, r, c = blocks.shape
    half = g // 2
    eye = jnp.eye(half, dtype=blocks.dtype)
    b = blocks.reshape(2, half, r, c)
    return jnp.einsum("hgrc,gk->hgrkc", b, eye).reshape(2, half * r, half * c)


def _segment_mask(tt):
    r = jnp.arange(CHUNK)
    same = (r[:, None] // tt) == (r[None, :] // tt)
    return (same & (r[:, None] >= r[None, :])).astype(F32)


def _prepare(lb_param, norm_mix, w_in, s5_a_re, s5_a_im, s5_log_dt, s5_b_re, s5_b_im, s5_c_re,
             s5_c_im, s5_d, s5_w_glu, hg_norm, w_out, norm_ffn, w_gate, w_up, w_down, norm_final):
    l = 0
    a_re = s5_a_re[l].astype(F32)
    a_im = s5_a_im[l].astype(F32)
    dt = jnp.exp(s5_log_dt[l].astype(F32))[:, None]
    mag = jnp.exp(a_re * dt)
    ab_re = mag * jnp.cos(a_im * dt)
    ab_im = mag * jnp.sin(a_im * dt)
    den = a_re * a_re + a_im * a_im
    nr = ab_re - 1.0
    ni = ab_im
    f_re = (nr * a_re + ni * a_im) / den
    f_im = (ni * a_re - nr * a_im) / den
    b_re = s5_b_re[l].astype(F32)
    b_im = s5_b_im[l].astype(F32)
    bb_re = f_re[..., None] * b_re - f_im[..., None] * b_im
    bb_im = f_re[..., None] * b_im + f_im[..., None] * b_re
    wb = jnp.concatenate([_block_diag_halves(bb_re.transpose(0, 2, 1)),
                          _block_diag_halves(bb_im.transpose(0, 2, 1))], axis=2).astype(BF16)
    wc = jnp.concatenate([_block_diag_halves(s5_c_re[l].astype(F32).transpose(0, 2, 1)),
                          _block_diag_halves(-s5_c_im[l].astype(F32).transpose(0, 2, 1))],
                         axis=1).astype(BF16)
    lb_all = jnp.cumsum(jax.nn.softmax(lb_param.astype(F32), axis=0), axis=0)
    return {
        "norm_mix": norm_mix[l].reshape(1, D_MODEL).astype(F32),
        "w_in": w_in[l].astype(BF16),
        "wb": wb,
        "a_re": ab_re.reshape(1, S5_FLAT),
        "a_im": ab_im.reshape(1, S5_FLAT),
        "wc": wc,
        "d_skip": s5_d[l].reshape(1, S5_WIDTH).astype(F32),
        "w_glu": s5_w_glu[l].astype(BF16),
        "lb": lb_all[l].reshape(1, HG_WIDTH),
        "g_norm": hg_norm[l].reshape(1, HG_DIM).astype(F32),
        "w_out": w_out[l].astype(BF16),
        "mask_64": _segment_mask(64).astype(BF16),
        "mask_8": _segment_mask(8).astype(BF16),
        "norm_ffn": norm_ffn[l].reshape(1, D_MODEL).astype(F32),
        "w_gate": w_gate[l].astype(BF16),
        "w_up": w_up[l].astype(BF16),
        "w_down": w_down[l].astype(BF16),
        "norm_final": norm_final.reshape(1, D_MODEL).astype(F32),
    }


def _trunk(x, state, w, *, nb, tt, tm):
    B, T, _ = x.shape
    x1, h_re, h_im, s_new = _mixer(x, state, w, nb=nb, tt=tt)
    y = _ffn(x1.reshape(B * T, D_MODEL), w, tm=tm).reshape(B, T, D_MODEL)
    return (y, h_re.reshape(1, B, S5_GROUPS, S5_STATE), h_im.reshape(1, B, S5_GROUPS, S5_STATE),
            s_new.reshape(1, B, HG_HEADS, HG_DIM, HG_DIM))


def kernel(x_prompt, x_sample, state_s5_re, state_s5_im, state_hgrn, lb_param, norm_mix, w_in,
           s5_a_re, s5_a_im, s5_log_dt, s5_b_re, s5_b_im, s5_c_re, s5_c_im, s5_d, s5_w_glu,
           hg_norm, w_out, norm_ffn, w_gate, w_up, w_down, norm_final):
    assert norm_mix.shape[0] == 1, "single-layer trunk"
    w = _prepare(lb_param, norm_mix, w_in, s5_a_re, s5_a_im, s5_log_dt, s5_b_re, s5_b_im,
                 s5_c_re, s5_c_im, s5_d, s5_w_glu, hg_norm, w_out, norm_ffn, w_gate, w_up,
                 w_down, norm_final)
    bs = x_sample.shape[0]
    y_p, p_re, p_im, p_hg = _trunk(x_prompt, None, w, nb=8, tt=64, tm=512)
    st = (state_s5_re[0].reshape(bs, S5_FLAT), state_s5_im[0].reshape(bs, S5_FLAT), state_hgrn[0])
    y_s, s_re, s_im, s_hg = _trunk(x_sample, st, w, nb=16, tt=8, tm=512)
    return (y_p, y_s, p_re, p_im, p_hg, s_re, s_im, s_hg)
```

```python
import functools
import math

import jax
import jax.numpy as jnp
from jax import lax
from jax.experimental import pallas as pl
from jax.experimental.pallas import tpu as pltpu

F32 = jnp.float32
BF16 = jnp.bfloat16

D_MODEL = 1024
S5_WIDTH = 512
S5_GROUP = 16
S5_GROUPS = 32
S5_STATE = 64
S5_FLAT = S5_GROUPS * S5_STATE
HG_WIDTH = 512
HG_DIM = 128
HG_HEADS = 4
IN_COLS = S5_WIDTH + 4 * HG_WIDTH
EPS = 1e-6

LANES = 128
SUBLANES = 8
CHUNK = 64
N_SLABS = S5_FLAT // LANES
HALF_U = S5_WIDTH // 2
HALF_SLABS = N_SLABS // 2
SUB = 16
PROJ_COLS = 512
VMEM_LIMIT = 56 * 1024 * 1024


def _dot(a, b):
    return jnp.dot(a, b, preferred_element_type=F32)


def _dot_nt(a, b):
    return lax.dot_general(a, b, (((1,), (1,)), ((), ())), preferred_element_type=F32)


def _dot_tn(a, b):
    return lax.dot_general(a, b, (((0,), (0,)), ((), ())), preferred_element_type=F32)


def _split3(x):
    hi = x.astype(BF16)
    r1 = x - hi.astype(F32)
    mid = r1.astype(BF16)
    lo = (r1 - mid.astype(F32)).astype(BF16)
    return hi, mid, lo


def _rms(x, g):
    return x * lax.rsqrt(jnp.mean(x * x, axis=-1, keepdims=True) + EPS) * g


def _sigmoid(x):
    return jax.nn.sigmoid(x)


def _gelu_tanh(x):
    c = math.sqrt(2.0 / math.pi)
    return 0.5 * x * (1.0 + jnp.tanh(c * (x + 0.044715 * (x * x * x))))


def _mixer_kernel(*refs, nb, tt, has_state):
    it = iter(refs)
    x_ref = next(it)
    if has_state:
        h0re_ref, h0im_ref, s0_ref = next(it), next(it), next(it)
    (norm_ref, win_ref, wb_ref, are_ref, aim_ref, wc_ref, dskip_ref, wglu_ref, lb_ref,
     gnorm_ref, wout_ref, mask_ref) = (next(it) for _ in range(12))
    x1_ref, hre_ref, him_ref, s_ref = (next(it) for _ in range(4))
    proj_scr, bre_scr, bim_scr, mix_scr = (next(it) for _ in range(4))

    bpc = CHUNK // tt
    nchunks = nb // bpc
    pitch = tt + SUBLANES if tt == CHUNK else tt
    cpitch = pitch * bpc
    ngroups = nb // SUBLANES
    state_t = tt == CHUNK
    assert not (state_t and has_state)

    tstep = pl.program_id(1)

    @pl.when(tstep == 0)
    def _init():
        if has_state:
            hre_ref[...] = h0re_ref[...]
            him_ref[...] = h0im_ref[...]
        else:
            hre_ref[...] = jnp.zeros_like(hre_ref)
            him_ref[...] = jnp.zeros_like(him_ref)
            s_ref[...] = jnp.zeros_like(s_ref)

    rows_tile = nb * tt
    scr_row0 = [c * cpitch for c in range(nchunks)]

    hn = _rms(x_ref[...].reshape(rows_tile, D_MODEL), norm_ref[...]).astype(BF16)
    for n0 in range(0, IN_COLS, PROJ_COLS):
        proj_scr[:, n0:n0 + PROJ_COLS] = _dot(hn, win_ref[:, n0:n0 + PROJ_COLS])
    ub = proj_scr[:, 0:S5_WIDTH].astype(BF16)
    for hf in range(2):
        ubh = ub[:, hf * HALF_U:(hf + 1) * HALF_U]
        for part, scr in enumerate((bre_scr, bim_scr)):
            cols = slice(part * HALF_SLABS * LANES, (part + 1) * HALF_SLABS * LANES)
            bu = _dot(ubh, wb_ref[hf, :, cols])
            for jj in range(HALF_SLABS):
                for c in range(nchunks):
                    scr[hf * HALF_SLABS + jj, scr_row0[c]:scr_row0[c] + CHUNK, :] = (
                        bu[c * CHUNK:(c + 1) * CHUNK, jj * LANES:(jj + 1) * LANES])

    slabs_per_pass = 8
    for g in range(ngroups):
        base = g * SUBLANES * pitch
        for sp in range(N_SLABS // slabs_per_pass):
            js = [sp * slabs_per_pass + k for k in range(slabs_per_pass)]
            a_re = [jnp.broadcast_to(are_ref[:, j * LANES:(j + 1) * LANES], (SUBLANES, LANES)) for j in js]
            a_im = [jnp.broadcast_to(aim_ref[:, j * LANES:(j + 1) * LANES], (SUBLANES, LANES)) for j in js]
            h_re = tuple(hre_ref[g * SUBLANES:(g + 1) * SUBLANES, j * LANES:(j + 1) * LANES] for j in js)
            h_im = tuple(him_ref[g * SUBLANES:(g + 1) * SUBLANES, j * LANES:(j + 1) * LANES] for j in js)

            def step(t, hs, js=js, a_re=a_re, a_im=a_im, base=base):
                h_re, h_im = hs
                n_re, n_im = [], []
                for k, j in enumerate(js):
                    idx = pl.ds(base + t, SUBLANES, stride=pitch)
                    b_re = bre_scr[j, idx, :]
                    b_im = bim_scr[j, idx, :]
                    r = a_re[k] * h_re[k] - a_im[k] * h_im[k] + b_re
                    i = a_re[k] * h_im[k] + a_im[k] * h_re[k] + b_im
                    bre_scr[j, idx, :] = r
                    bim_scr[j, idx, :] = i
                    n_re.append(r)
                    n_im.append(i)
                return tuple(n_re), tuple(n_im)

            h_re, h_im = lax.fori_loop(0, tt, step, (h_re, h_im), unroll=8)
            for k, j in enumerate(js):
                hre_ref[g * SUBLANES:(g + 1) * SUBLANES, j * LANES:(j + 1) * LANES] = h_re[k]
                him_ref[g * SUBLANES:(g + 1) * SUBLANES, j * LANES:(j + 1) * LANES] = h_im[k]

    ys = []
    for hf in range(2):
        parts = []
        for scr in (bre_scr, bim_scr):
            for jj in range(HALF_SLABS):
                j = hf * HALF_SLABS + jj
                parts.append(jnp.concatenate(
                    [scr[j, scr_row0[c]:scr_row0[c] + CHUNK, :] for c in range(nchunks)], axis=0))
        hcat = jnp.concatenate(parts, axis=1).astype(BF16)
        ys.append(_dot(hcat, wc_ref[hf]))
    y = jnp.concatenate(ys, axis=1) + dskip_ref[...] * proj_scr[:, 0:S5_WIDTH]
    gact = _gelu_tanh(y)
    out5 = gact * _sigmoid(_dot(gact.astype(BF16), wglu_ref[...]))
    mix_scr[:, 0:S5_WIDTH] = out5.astype(BF16)

    lb = lb_ref[...]
    seg_mask = mask_ref[...].astype(F32) > 0.5
    zeros_t = jnp.zeros((tt, HG_DIM), F32)
    dec_rows = jnp.concatenate(
        [jnp.zeros((3 * tt, HG_DIM), F32), jnp.ones((3 * tt, HG_DIM), F32)], axis=1)
    heads = range(HG_HEADS)

    def phase3(c, carry):
        r0 = pl.multiple_of(c * CHUNK, CHUNK)
        rows = pl.ds(r0, CHUNK)
        q_all = proj_scr[rows, S5_WIDTH:S5_WIDTH + HG_WIDTH]
        fz = proj_scr[rows, S5_WIDTH + HG_WIDTH:S5_WIDTH + 2 * HG_WIDTH]
        iv_all = proj_scr[rows, S5_WIDTH + 2 * HG_WIDTH:S5_WIDTH + 3 * HG_WIDTH]
        og_all = proj_scr[rows, S5_WIDTH + 3 * HG_WIDTH:S5_WIDTH + 4 * HG_WIDTH]
        f = lb + (1.0 - lb) * _sigmoid(fz)
        lf_all = jnp.log(f)
        k_all = 1.0 - f
        lf_pieces = _split3(lf_all)
        g_all = sum(_dot(mask_ref[...], p) for p in lf_pieces)
        vb_all = iv_all.astype(BF16)
        qg_all = (q_all * jnp.exp(g_all)).astype(BF16)
        hs = [slice(h * HG_DIM, (h + 1) * HG_DIM) for h in heads]

        qs, ks = [], []
        for h in heads:
            gh, qh, kh = g_all[:, hs[h]], q_all[:, hs[h]], k_all[:, hs[h]]
            if tt == CHUNK:
                nblk = CHUNK // SUB
                blocks = [slice(b * SUB, (b + 1) * SUB) for b in range(nblk)]
                refs = [gh[b * SUB + SUB // 2:b * SUB + SUB // 2 + 1, :] for b in range(nblk)]
                ref_rows = jnp.concatenate(
                    [jnp.broadcast_to(r, (SUB, HG_DIM)) for r in refs], axis=0)
                q_loc = (qh * jnp.exp(gh - ref_rows)).astype(BF16)
                k_loc = kh * jnp.exp(ref_rows - gh)
                no_keys = jnp.zeros((SUB, HG_DIM), BF16)
                for i in range(nblk):
                    qs.append(q_loc[blocks[i]])
                    parts = [(k_loc[blocks[j]] * jnp.exp(refs[i] - refs[j])).astype(BF16)
                             for j in range(i)]
                    parts.append(k_loc[blocks[i]].astype(BF16))
                    parts += [no_keys] * (nblk - 1 - i)
                    ks.append(jnp.concatenate(parts, axis=0))
            else:
                g3 = gh.reshape(bpc, tt, HG_DIM)
                ref3 = jnp.broadcast_to(g3[:, tt // 2:tt // 2 + 1, :], (bpc, tt, HG_DIM))
                ref_rows = ref3.reshape(CHUNK, HG_DIM)
                qs.append((qh * jnp.exp(gh - ref_rows)).astype(BF16))
                ks.append((kh * jnp.exp(ref_rows - gh)).astype(BF16))

        att = [_dot_nt(a, b) for a, b in zip(qs, ks)]

        o_inter = []
        for h in heads:
            gh, kh = g_all[:, hs[h]], k_all[:, hs[h]]
            oi = []
            for bb in range(bpc):
                rs = slice(bb * tt, (bb + 1) * tt)
                b_idx = c * bpc + bb
                s_old = s0_ref[b_idx, h] if has_state else s_ref[b_idx, h]
                g_last = gh[bb * tt + tt - 1:bb * tt + tt, :]
                kd = kh[rs] * jnp.exp(g_last - gh[rs])
                if state_t:
                    oi.append(_dot_nt(qg_all[rs, hs[h]], s_old.astype(BF16)))
                    s_ref[b_idx, h] = (jnp.exp(g_last) * s_old
                                       + _dot_tn(vb_all[rs, hs[h]], kd.astype(BF16)))
                else:
                    oi.append(_dot(qg_all[rs, hs[h]], s_old.astype(BF16)))
                    lhs = jnp.concatenate(
                        [kd] + [p[rs, hs[h]].astype(F32) for p in lf_pieces], axis=0)
                    rhs = jnp.concatenate(
                        [jnp.concatenate([iv_all[rs, hs[h]], zeros_t], axis=1), dec_rows], axis=0)
                    upd = _dot_tn(lhs.astype(BF16), rhs.astype(BF16))
                    s_ref[b_idx, h] = jnp.exp(upd[:, HG_DIM:]) * s_old + upd[:, :HG_DIM]
            o_inter.append(oi[0] if bpc == 1 else jnp.concatenate(oi, axis=0))

        nblk = len(att) // HG_HEADS
        for h in heads:
            sc = att[h * nblk:(h + 1) * nblk]
            sc = sc[0] if nblk == 1 else jnp.concatenate(sc, axis=0)
            sc = jnp.where(seg_mask, sc, 0.0).astype(BF16)
            o = _dot(sc, vb_all[:, hs[h]]) + o_inter[h]
            o = _rms(o, gnorm_ref[...])
            ogh = og_all[:, hs[h]]
            mix_scr[rows, S5_WIDTH + h * HG_DIM:S5_WIDTH + (h + 1) * HG_DIM] = (
                o * (ogh * _sigmoid(ogh))).astype(BF16)
        return carry

    lax.fori_loop(0, nchunks, phase3, 0, unroll=4 if state_t else 1)

    if state_t:
        @pl.when(tstep == pl.num_programs(1) - 1)
        def _untranspose_state():
            for b in range(nb):
                for h in heads:
                    s_ref[b, h] = s_ref[b, h].T

    x1 = x_ref[...].reshape(rows_tile, D_MODEL) + _dot(mix_scr[...], wout_ref[...])
    x1_ref[...] = x1.reshape(nb, tt, D_MODEL)


def _const_spec(shape):
    nd = len(shape)
    return pl.BlockSpec(shape, lambda i, j, _nd=nd: (0,) * _nd, pipeline_mode=pl.Buffered(1))


def _mixer(x, state, w, *, nb, tt):
    B, T, _ = x.shape
    has_state = state is not None
    grid = (B // nb, T // tt)
    bpc = CHUNK // tt
    nchunks = nb // bpc
    pitch = tt + SUBLANES if tt == CHUNK else tt
    scr_rows = nchunks * pitch * bpc

    consts = [w["norm_mix"], w["w_in"], w["wb"], w["a_re"], w["a_im"], w["wc"], w["d_skip"],
              w["w_glu"], w["lb"], w["g_norm"], w["w_out"], w["mask_" + str(tt)]]
    in_specs = [pl.BlockSpec((nb, tt, D_MODEL), lambda i, j: (i, j, 0))]
    args = [x]
    if has_state:
        in_specs += [pl.BlockSpec((nb, S5_FLAT), lambda i, j: (i, 0)),
                     pl.BlockSpec((nb, S5_FLAT), lambda i, j: (i, 0)),
                     pl.BlockSpec((nb, HG_HEADS, HG_DIM, HG_DIM), lambda i, j: (i, 0, 0, 0))]
        args += list(state)
    in_specs += [_const_spec(c.shape) for c in consts]
    args += consts

    out_shape = (jax.ShapeDtypeStruct((B, T, D_MODEL), F32),
                 jax.ShapeDtypeStruct((B, S5_FLAT), F32),
                 jax.ShapeDtypeStruct((B, S5_FLAT), F32),
                 jax.ShapeDtypeStruct((B, HG_HEADS, HG_DIM, HG_DIM), F32))
    out_specs = (pl.BlockSpec((nb, tt, D_MODEL), lambda i, j: (i, j, 0)),
                 pl.BlockSpec((nb, S5_FLAT), lambda i, j: (i, 0)),
                 pl.BlockSpec((nb, S5_FLAT), lambda i, j: (i, 0)),
                 pl.BlockSpec((nb, HG_HEADS, HG_DIM, HG_DIM), lambda i, j: (i, 0, 0, 0)))
    scratch = [pltpu.VMEM((nb * tt, IN_COLS), F32),
               pltpu.VMEM((N_SLABS, scr_rows, LANES), F32),
               pltpu.VMEM((N_SLABS, scr_rows, LANES), F32),
               pltpu.VMEM((nb * tt, D_MODEL), BF16)]
    return pl.pallas_call(
        functools.partial(_mixer_kernel, nb=nb, tt=tt, has_state=has_state),
        grid=grid, in_specs=in_specs, out_specs=out_specs, out_shape=out_shape,
        scratch_shapes=scratch,
        compiler_params=pltpu.CompilerParams(
            dimension_semantics=("arbitrary", "arbitrary"), vmem_limit_bytes=VMEM_LIMIT),
        name="mixer_t%d" % tt,
    )(*args)


FF_CHUNK = 256


def _ffn_kernel(x_ref, nffn_ref, wg_ref, wu_ref, wd_ref, nfin_ref, y_ref):
    x = x_ref[...]
    h2 = _rms(x, nffn_ref[...]).astype(BF16)
    d_ff = wg_ref.shape[1]
    acc = x
    for f0 in range(0, d_ff, FF_CHUNK):
        gate = _dot(h2, wg_ref[:, f0:f0 + FF_CHUNK])
        up = _dot(h2, wu_ref[:, f0:f0 + FF_CHUNK])
        act = (gate * _sigmoid(gate) * up).astype(BF16)
        acc = acc + _dot(act, wd_ref[f0:f0 + FF_CHUNK, :])
    y_ref[...] = _rms(acc, nfin_ref[...])


def _ffn(x2d, w, *, tm):
    n = x2d.shape[0]
    d_ff = w["w_gate"].shape[1]

    def cspec(shape):
        return pl.BlockSpec(shape, lambda i: (0, 0), pipeline_mode=pl.Buffered(1))

    return pl.pallas_call(
        _ffn_kernel,
        grid=(n // tm,),
        in_specs=[pl.BlockSpec((tm, D_MODEL), lambda i: (i, 0)),
                  cspec((1, D_MODEL)), cspec((D_MODEL, d_ff)), cspec((D_MODEL, d_ff)),
                  cspec((d_ff, D_MODEL)), cspec((1, D_MODEL))],
        out_specs=pl.BlockSpec((tm, D_MODEL), lambda i: (i, 0)),
        out_shape=jax.ShapeDtypeStruct((n, D_MODEL), F32),
        compiler_params=pltpu.CompilerParams(
            dimension_semantics=("arbitrary",), vmem_limit_bytes=VMEM_LIMIT),
        name="ffn",
    )(x2d, w["norm_ffn"], w["w_gate"], w["w_up"], w["w_down"], w["norm_final"])


def _block_diag_halves(blocks):
    g, r, c = blocks.shape
    half = g // 2
    eye = jnp.eye(half, dtype=blocks.dtype)
    b = blocks.reshape(2, half, r, c)
    return jnp.einsum("hgrc,gk->hgrkc", b, eye).reshape(2, half * r, half * c)


def _segment_mask(tt):
    r = jnp.arange(CHUNK)
    same = (r[:, None] // tt) == (r[None, :] // tt)
    return (same & (r[:, None] >= r[None, :])).astype(F32)


def _prepare(lb_param, norm_mix, w_in, s5_a_re, s5_a_im, s5_log_dt, s5_b_re, s5_b_im, s5_c_re,
             s5_c_im, s5_d, s5_w_glu, hg_norm, w_out, norm_ffn, w_gate, w_up, w_down, norm_final):
    l = 0
    a_re = s5_a_re[l].astype(F32)
    a_im = s5_a_im[l].astype(F32)
    dt = jnp.exp(s5_log_dt[l].astype(F32))[:, None]
    mag = jnp.exp(a_re * dt)
    ab_re = mag * jnp.cos(a_im * dt)
    ab_im = mag * jnp.sin(a_im * dt)
    den = a_re * a_re + a_im * a_im
    nr = ab_re - 1.0
    ni = ab_im
    f_re = (nr * a_re + ni * a_im) / den
    f_im = (ni * a_re - nr * a_im) / den
    b_re = s5_b_re[l].astype(F32)
    b_im = s5_b_im[l].astype(F32)
    bb_re = f_re[..., None] * b_re - f_im[..., None] * b_im
    bb_im = f_re[..., None] * b_im + f_im[..., None] * b_re
    wb = jnp.concatenate([_block_diag_halves(bb_re.transpose(0, 2, 1)),
                          _block_diag_halves(bb_im.transpose(0, 2, 1))], axis=2).astype(BF16)
    wc = jnp.concatenate([_block_diag_halves(s5_c_re[l].astype(F32).transpose(0, 2, 1)),
                          _block_diag_halves(-s5_c_im[l].astype(F32).transpose(0, 2, 1))],
                         axis=1).astype(BF16)
    lb_all = jnp.cumsum(jax.nn.softmax(lb_param.astype(F32), axis=0), axis=0)
    return {
        "norm_mix": norm_mix[l].reshape(1, D_MODEL).astype(F32),
        "w_in": w_in[l].astype(BF16),
        "wb": wb,
        "a_re": ab_re.reshape(1, S5_FLAT),
        "a_im": ab_im.reshape(1, S5_FLAT),
        "wc": wc,
        "d_skip": s5_d[l].reshape(1, S5_WIDTH).astype(F32),
        "w_glu": s5_w_glu[l].astype(BF16),
        "lb": lb_all[l].reshape(1, HG_WIDTH),
        "g_norm": hg_norm[l].reshape(1, HG_DIM).astype(F32),
        "w_out": w_out[l].astype(BF16),
        "mask_64": _segment_mask(64).astype(BF16),
        "mask_8": _segment_mask(8).astype(BF16),
        "norm_ffn": norm_ffn[l].reshape(1, D_MODEL).astype(F32),
        "w_gate": w_gate[l].astype(BF16),
        "w_up": w_up[l].astype(BF16),
        "w_down": w_down[l].astype(BF16),
        "norm_final": norm_final.reshape(1, D_MODEL).astype(F32),
    }


def _trunk(x, state, w, *, nb, tt, tm):
    B, T, _ = x.shape
    x1, h_re, h_im, s_new = _mixer(x, state, w, nb=nb, tt=tt)
    y = _ffn(x1.reshape(B * T, D_MODEL), w, tm=tm).reshape(B, T, D_MODEL)
    return (y, h_re.reshape(1, B, S5_GROUPS, S5_STATE), h_im.reshape(1, B, S5_GROUPS, S5_STATE),
            s_new.reshape(1, B, HG_HEADS, HG_DIM, HG_DIM))


def kernel(x_prompt, x_sample, state_s5_re, state_s5_im, state_hgrn, lb_param, norm_mix, w_in,
           s5_a_re, s5_a_im, s5_log_dt, s5_b_re, s5_b_im, s5_c_re, s5_c_im, s5_d, s5_w_glu,
           hg_norm, w_out, norm_ffn, w_gate, w_up, w_down, norm_final):
    assert norm_mix.shape[0] == 1, "single-layer trunk"
    w = _prepare(lb_param, norm_mix, w_in, s5_a_re, s5_a_im, s5_log_dt, s5_b_re, s5_b_im,
                 s5_c_re, s5_c_im, s5_d, s5_w_glu, hg_norm, w_out, norm_ffn, w_gate, w_up,
                 w_down, norm_final)
    bs = x_sample.shape[0]
    y_p, p_re, p_im, p_hg = _trunk(x_prompt, None, w, nb=8, tt=64, tm=512)
    st = (state_s5_re[0].reshape(bs, S5_FLAT), state_s5_im[0].reshape(bs, S5_FLAT), state_hgrn[0])
    y_s, s_re, s_im, s_hg = _trunk(x_sample, st, w, nb=16, tt=8, tm=512)
    return (y_p, y_s, p_re, p_im, p_hg, s_re, s_im, s_hg)
```

```python
import functools
import math

import jax
import jax.numpy as jnp
from jax import lax
from jax.experimental import pallas as pl
from jax.experimental.pallas import tpu as pltpu

F32 = jnp.float32
BF16 = jnp.bfloat16

D_MODEL = 1024
S5_WIDTH = 512
S5_GROUP = 16
S5_GROUPS = 32
S5_STATE = 64
S5_FLAT = S5_GROUPS * S5_STATE
HG_WIDTH = 512
HG_DIM = 128
HG_HEADS = 4
IN_COLS = S5_WIDTH + 4 * HG_WIDTH
EPS = 1e-6

LANES = 128
SUBLANES = 8
CHUNK = 64
N_SLABS = S5_FLAT // LANES
HALF_U = S5_WIDTH // 2
HALF_SLABS = N_SLABS // 2
SUB = 16
PROJ_COLS = 512
SCAN_PAD = 4
VMEM_LIMIT = 56 * 1024 * 1024


def _seq_pitch(tt):
    return tt + SCAN_PAD if tt == CHUNK else tt


def _dot(a, b):
    return jnp.dot(a, b, preferred_element_type=F32)


def _dot_nt(a, b):
    return lax.dot_general(a, b, (((1,), (1,)), ((), ())), preferred_element_type=F32)


def _dot_tn(a, b):
    return lax.dot_general(a, b, (((0,), (0,)), ((), ())), preferred_element_type=F32)


def _split3(x):
    hi = x.astype(BF16)
    r1 = x - hi.astype(F32)
    mid = r1.astype(BF16)
    lo = (r1 - mid.astype(F32)).astype(BF16)
    return hi, mid, lo


def _rms(x, g):
    return x * lax.rsqrt(jnp.mean(x * x, axis=-1, keepdims=True) + EPS) * g


def _sigmoid(x):
    return jax.nn.sigmoid(x)


def _gelu_tanh(x):
    c = math.sqrt(2.0 / math.pi)
    return 0.5 * x * (1.0 + jnp.tanh(c * (x + 0.044715 * (x * x * x))))


def _mixer_kernel(*refs, nb, tt, has_state):
    it = iter(refs)
    x_ref = next(it)
    if has_state:
        h0re_ref, h0im_ref, s0_ref = next(it), next(it), next(it)
    (norm_ref, win_ref, wb_ref, are_ref, aim_ref, wc_ref, dskip_ref, wglu_ref, lb_ref,
     gnorm_ref, wout_ref, mask_ref) = (next(it) for _ in range(12))
    x1_ref, hre_ref, him_ref, s_ref = (next(it) for _ in range(4))
    proj_scr, bre_scr, bim_scr, mix_scr = (next(it) for _ in range(4))

    bpc = CHUNK // tt
    nchunks = nb // bpc
    pitch = _seq_pitch(tt)
    cpitch = pitch * bpc
    ngroups = nb // SUBLANES
    state_t = tt == CHUNK
    assert not (state_t and has_state)

    tstep = pl.program_id(1)

    @pl.when(tstep == 0)
    def _init():
        if has_state:
            hre_ref[...] = h0re_ref[...]
            him_ref[...] = h0im_ref[...]
        else:
            hre_ref[...] = jnp.zeros_like(hre_ref)
            him_ref[...] = jnp.zeros_like(him_ref)
            s_ref[...] = jnp.zeros_like(s_ref)

    rows_tile = nb * tt
    scr_row0 = [c * cpitch for c in range(nchunks)]

    hn = _rms(x_ref[...].reshape(rows_tile, D_MODEL), norm_ref[...]).astype(BF16)
    for n0 in range(0, IN_COLS, PROJ_COLS):
        proj_scr[:, n0:n0 + PROJ_COLS] = _dot(hn, win_ref[:, n0:n0 + PROJ_COLS])
    ub = proj_scr[:, 0:S5_WIDTH].astype(BF16)
    for hf in range(2):
        ubh = ub[:, hf * HALF_U:(hf + 1) * HALF_U]
        for part, scr in enumerate((bre_scr, bim_scr)):
            cols = slice(part * HALF_SLABS * LANES, (part + 1) * HALF_SLABS * LANES)
            bu = _dot(ubh, wb_ref[hf, :, cols])
            for jj in range(HALF_SLABS):
                for c in range(nchunks):
                    scr[hf * HALF_SLABS + jj, scr_row0[c]:scr_row0[c] + CHUNK, :] = (
                        bu[c * CHUNK:(c + 1) * CHUNK, jj * LANES:(jj + 1) * LANES])

    slabs_per_pass = 8
    for g in range(ngroups):
        base = g * SUBLANES * pitch
        for sp in range(N_SLABS // slabs_per_pass):
            js = [sp * slabs_per_pass + k for k in range(slabs_per_pass)]
            a_re = [jnp.broadcast_to(are_ref[:, j * LANES:(j + 1) * LANES], (SUBLANES, LANES)) for j in js]
            a_im = [jnp.broadcast_to(aim_ref[:, j * LANES:(j + 1) * LANES], (SUBLANES, LANES)) for j in js]
            h_re = tuple(hre_ref[g * SUBLANES:(g + 1) * SUBLANES, j * LANES:(j + 1) * LANES] for j in js)
            h_im = tuple(him_ref[g * SUBLANES:(g + 1) * SUBLANES, j * LANES:(j + 1) * LANES] for j in js)

            def step(t, hs, js=js, a_re=a_re, a_im=a_im, base=base):
                h_re, h_im = hs
                n_re, n_im = [], []
                for k, j in enumerate(js):
                    idx = pl.ds(base + t, SUBLANES, stride=pitch)
                    b_re = bre_scr[j, idx, :]
                    b_im = bim_scr[j, idx, :]
                    r = a_re[k] * h_re[k] - a_im[k] * h_im[k] + b_re
                    i = a_re[k] * h_im[k] + a_im[k] * h_re[k] + b_im
                    bre_scr[j, idx, :] = r
                    bim_scr[j, idx, :] = i
                    n_re.append(r)
                    n_im.append(i)
                return tuple(n_re), tuple(n_im)

            h_re, h_im = lax.fori_loop(0, tt, step, (h_re, h_im), unroll=8)
            for k, j in enumerate(js):
                hre_ref[g * SUBLANES:(g + 1) * SUBLANES, j * LANES:(j + 1) * LANES] = h_re[k]
                him_ref[g * SUBLANES:(g + 1) * SUBLANES, j * LANES:(j + 1) * LANES] = h_im[k]

    ys = []
    for hf in range(2):
        parts = []
        for scr in (bre_scr, bim_scr):
            for jj in range(HALF_SLABS):
                j = hf * HALF_SLABS + jj
                parts.append(jnp.concatenate(
                    [scr[j, scr_row0[c]:scr_row0[c] + CHUNK, :] for c in range(nchunks)], axis=0))
        hcat = jnp.concatenate(parts, axis=1).astype(BF16)
        ys.append(_dot(hcat, wc_ref[hf]))
    y = jnp.concatenate(ys, axis=1) + dskip_ref[...] * proj_scr[:, 0:S5_WIDTH]
    gact = _gelu_tanh(y)
    out5 = gact * _sigmoid(_dot(gact.astype(BF16), wglu_ref[...]))
    mix_scr[:, 0:S5_WIDTH] = out5.astype(BF16)

    lb = lb_ref[...]
    seg_mask = mask_ref[...].astype(F32) > 0.5
    zeros_t = jnp.zeros((tt, HG_DIM), F32)
    dec_rows = jnp.concatenate(
        [jnp.zeros((3 * tt, HG_DIM), F32), jnp.ones((3 * tt, HG_DIM), F32)], axis=1)
    heads = range(HG_HEADS)

    def phase3(c, carry):
        r0 = pl.multiple_of(c * CHUNK, CHUNK)
        rows = pl.ds(r0, CHUNK)
        q_all = proj_scr[rows, S5_WIDTH:S5_WIDTH + HG_WIDTH]
        fz = proj_scr[rows, S5_WIDTH + HG_WIDTH:S5_WIDTH + 2 * HG_WIDTH]
        iv_all = proj_scr[rows, S5_WIDTH + 2 * HG_WIDTH:S5_WIDTH + 3 * HG_WIDTH]
        og_all = proj_scr[rows, S5_WIDTH + 3 * HG_WIDTH:S5_WIDTH + 4 * HG_WIDTH]
        f = lb + (1.0 - lb) * _sigmoid(fz)
        lf_all = jnp.log(f)
        k_all = 1.0 - f
        lf_pieces = _split3(lf_all)
        g_all = sum(_dot(mask_ref[...], p) for p in lf_pieces)
        vb_all = iv_all.astype(BF16)
        qg_all = (q_all * jnp.exp(g_all)).astype(BF16)
        hs = [slice(h * HG_DIM, (h + 1) * HG_DIM) for h in heads]

        qs, ks = [], []
        for h in heads:
            gh, qh, kh = g_all[:, hs[h]], q_all[:, hs[h]], k_all[:, hs[h]]
            if tt == CHUNK:
                nblk = CHUNK // SUB
                blocks = [slice(b * SUB, (b + 1) * SUB) for b in range(nblk)]
                refs = [gh[b * SUB + SUB // 2:b * SUB + SUB // 2 + 1, :] for b in range(nblk)]
                ref_rows = jnp.concatenate(
                    [jnp.broadcast_to(r, (SUB, HG_DIM)) for r in refs], axis=0)
                q_loc = (qh * jnp.exp(gh - ref_rows)).astype(BF16)
                k_loc = kh * jnp.exp(ref_rows - gh)
                no_keys = jnp.zeros((SUB, HG_DIM), BF16)
                for i in range(nblk):
                    qs.append(q_loc[blocks[i]])
                    parts = [(k_loc[blocks[j]] * jnp.exp(refs[i] - refs[j])).astype(BF16)
                             for j in range(i)]
                    parts.append(k_loc[blocks[i]].astype(BF16))
                    parts += [no_keys] * (nblk - 1 - i)
                    ks.append(jnp.concatenate(parts, axis=0))
            else:
                g3 = gh.reshape(bpc, tt, HG_DIM)
                ref3 = jnp.broadcast_to(g3[:, tt // 2:tt // 2 + 1, :], (bpc, tt, HG_DIM))
                ref_rows = ref3.reshape(CHUNK, HG_DIM)
                qs.append((qh * jnp.exp(gh - ref_rows)).astype(BF16))
                ks.append((kh * jnp.exp(ref_rows - gh)).astype(BF16))

        att = [_dot_nt(a, b) for a, b in zip(qs, ks)]

        o_inter = []
        for h in heads:
            gh, kh = g_all[:, hs[h]], k_all[:, hs[h]]
            oi = []
            for bb in range(bpc):
                rs = slice(bb * tt, (bb + 1) * tt)
                b_idx = c * bpc + bb
                s_old = s0_ref[b_idx, h] if has_state else s_ref[b_idx, h]
                g_last = gh[bb * tt + tt - 1:bb * tt + tt, :]
                kd = kh[rs] * jnp.exp(g_last - gh[rs])
                if state_t:
                    oi.append(_dot_nt(qg_all[rs, hs[h]], s_old.astype(BF16)))
                    s_ref[b_idx, h] = (jnp.exp(g_last) * s_old
                                       + _dot_tn(vb_all[rs, hs[h]], kd.astype(BF16)))
                else:
                    oi.append(_dot(qg_all[rs, hs[h]], s_old.astype(BF16)))
                    lhs = jnp.concatenate(
                        [kd] + [p[rs, hs[h]].astype(F32) for p in lf_pieces], axis=0)
                    rhs = jnp.concatenate(
                        [jnp.concatenate([iv_all[rs, hs[h]], zeros_t], axis=1), dec_rows], axis=0)
                    upd = _dot_tn(lhs.astype(BF16), rhs.astype(BF16))
                    s_ref[b_idx, h] = jnp.exp(upd[:, HG_DIM:]) * s_old + upd[:, :HG_DIM]
            o_inter.append(oi[0] if bpc == 1 else jnp.concatenate(oi, axis=0))

        nblk = len(att) // HG_HEADS
        for h in heads:
            sc = att[h * nblk:(h + 1) * nblk]
            sc = sc[0] if nblk == 1 else jnp.concatenate(sc, axis=0)
            sc = jnp.where(seg_mask, sc, 0.0).astype(BF16)
            o = _dot(sc, vb_all[:, hs[h]]) + o_inter[h]
            o = _rms(o, gnorm_ref[...])
            ogh = og_all[:, hs[h]]
            mix_scr[rows, S5_WIDTH + h * HG_DIM:S5_WIDTH + (h + 1) * HG_DIM] = (
                o * (ogh * _sigmoid(ogh))).astype(BF16)
        return carry

    lax.fori_loop(0, nchunks, phase3, 0, unroll=4 if state_t else 1)

    if state_t:
        @pl.when(tstep == pl.num_programs(1) - 1)
        def _untranspose_state():
            for b in range(nb):
                for h in heads:
                    s_ref[b, h] = s_ref[b, h].T

    x1 = x_ref[...].reshape(rows_tile, D_MODEL) + _dot(mix_scr[...], wout_ref[...])
    x1_ref[...] = x1.reshape(nb, tt, D_MODEL)


def _const_spec(shape):
    nd = len(shape)
    return pl.BlockSpec(shape, lambda i, j, _nd=nd: (0,) * _nd, pipeline_mode=pl.Buffered(1))


def _mixer(x, state, w, *, nb, tt):
    B, T, _ = x.shape
    has_state = state is not None
    grid = (B // nb, T // tt)
    bpc = CHUNK // tt
    nchunks = nb // bpc
    pitch = _seq_pitch(tt)
    scr_rows = nchunks * pitch * bpc

    consts = [w["norm_mix"], w["w_in"], w["wb"], w["a_re"], w["a_im"], w["wc"], w["d_skip"],
              w["w_glu"], w["lb"], w["g_norm"], w["w_out"], w["mask_" + str(tt)]]
    in_specs = [pl.BlockSpec((nb, tt, D_MODEL), lambda i, j: (i, j, 0))]
    args = [x]
    if has_state:
        in_specs += [pl.BlockSpec((nb, S5_FLAT), lambda i, j: (i, 0)),
                     pl.BlockSpec((nb, S5_FLAT), lambda i, j: (i, 0)),
                     pl.BlockSpec((nb, HG_HEADS, HG_DIM, HG_DIM), lambda i, j: (i, 0, 0, 0))]
        args += list(state)
    in_specs += [_const_spec(c.shape) for c in consts]
    args += consts

    out_shape = (jax.ShapeDtypeStruct((B, T, D_MODEL), F32),
                 jax.ShapeDtypeStruct((B, S5_FLAT), F32),
                 jax.ShapeDtypeStruct((B, S5_FLAT), F32),
                 jax.ShapeDtypeStruct((B, HG_HEADS, HG_DIM, HG_DIM), F32))
    out_specs = (pl.BlockSpec((nb, tt, D_MODEL), lambda i, j: (i, j, 0)),
                 pl.BlockSpec((nb, S5_FLAT), lambda i, j: (i, 0)),
                 pl.BlockSpec((nb, S5_FLAT), lambda i, j: (i, 0)),
                 pl.BlockSpec((nb, HG_HEADS, HG_DIM, HG_DIM), lambda i, j: (i, 0, 0, 0)))
    scratch = [pltpu.VMEM((nb * tt, IN_COLS), F32),
               pltpu.VMEM((N_SLABS, scr_rows, LANES), F32),
               pltpu.VMEM((N_SLABS, scr_rows, LANES), F32),
               pltpu.VMEM((nb * tt, D_MODEL), BF16)]
    return pl.pallas_call(
        functools.partial(_mixer_kernel, nb=nb, tt=tt, has_state=has_state),
        grid=grid, in_specs=in_specs, out_specs=out_specs, out_shape=out_shape,
        scratch_shapes=scratch,
        compiler_params=pltpu.CompilerParams(
            dimension_semantics=("arbitrary", "arbitrary"), vmem_limit_bytes=VMEM_LIMIT),
        name="mixer_t%d" % tt,
    )(*args)


FF_CHUNK = 256


def _ffn_kernel(x_ref, nffn_ref, wg_ref, wu_ref, wd_ref, nfin_ref, y_ref):
    x = x_ref[...]
    h2 = _rms(x, nffn_ref[...]).astype(BF16)
    d_ff = wg_ref.shape[1]
    acc = x
    for f0 in range(0, d_ff, FF_CHUNK):
        gate = _dot(h2, wg_ref[:, f0:f0 + FF_CHUNK])
        up = _dot(h2, wu_ref[:, f0:f0 + FF_CHUNK])
        act = (gate * _sigmoid(gate) * up).astype(BF16)
        acc = acc + _dot(act, wd_ref[f0:f0 + FF_CHUNK, :])
    y_ref[...] = _rms(acc, nfin_ref[...])


def _ffn(x2d, w, *, tm):
    n = x2d.shape[0]
    d_ff = w["w_gate"].shape[1]

    def cspec(shape):
        return pl.BlockSpec(shape, lambda i: (0, 0), pipeline_mode=pl.Buffered(1))

    return pl.pallas_call(
        _ffn_kernel,
        grid=(n // tm,),
        in_specs=[pl.BlockSpec((tm, D_MODEL), lambda i: (i, 0)),
                  cspec((1, D_MODEL)), cspec((D_MODEL, d_ff)), cspec((D_MODEL, d_ff)),
                  cspec((d_ff, D_MODEL)), cspec((1, D_MODEL))],
        out_specs=pl.BlockSpec((tm, D_MODEL), lambda i: (i, 0)),
        out_shape=jax.ShapeDtypeStruct((n, D_MODEL), F32),
        compiler_params=pltpu.CompilerParams(
            dimension_semantics=("arbitrary",), vmem_limit_bytes=VMEM_LIMIT),
        name="ffn",
    )(x2d, w["norm_ffn"], w["w_gate"], w["w_up"], w["w_down"], w["norm_final"])


def _block_diag_halves(blocks):
    g, r, c = blocks.shape
    half = g // 2
    eye = jnp.eye(half, dtype=blocks.dtype)
    b = blocks.reshape(2, half, r, c)
    return jnp.einsum("hgrc,gk->hgrkc", b, eye).reshape(2, half * r, half * c)


def _segment_mask(tt):
    r = jnp.arange(CHUNK)
    same = (r[:, None] // tt) == (r[None, :] // tt)
    return (same & (r[:, None] >= r[None, :])).astype(F32)


def _prepare(lb_param, norm_mix, w_in, s5_a_re, s5_a_im, s5_log_dt, s5_b_re, s5_b_im, s5_c_re,
             s5_c_im, s5_d, s5_w_glu, hg_norm, w_out, norm_ffn, w_gate, w_up, w_down, norm_final):
    l = 0
    a_re = s5_a_re[l].astype(F32)
    a_im = s5_a_im[l].astype(F32)
    dt = jnp.exp(s5_log_dt[l].astype(F32))[:, None]
    mag = jnp.exp(a_re * dt)
    ab_re = mag * jnp.cos(a_im * dt)
    ab_im = mag * jnp.sin(a_im * dt)
    den = a_re * a_re + a_im * a_im
    nr = ab_re - 1.0
    ni = ab_im
    f_re = (nr * a_re + ni * a_im) / den
    f_im = (ni * a_re - nr * a_im) / den
    b_re = s5_b_re[l].astype(F32)
    b_im = s5_b_im[l].astype(F32)
    bb_re = f_re[..., None] * b_re - f_im[..., None] * b_im
    bb_im = f_re[..., None] * b_im + f_im[..., None] * b_re
    wb = jnp.concatenate([_block_diag_halves(bb_re.transpose(0, 2, 1)),
                          _block_diag_halves(bb_im.transpose(0, 2, 1))], axis=2).astype(BF16)
    wc = jnp.concatenate([_block_diag_halves(s5_c_re[l].astype(F32).transpose(0, 2, 1)),
                          _block_diag_halves(-s5_c_im[l].astype(F32).transpose(0, 2, 1))],
                         axis=1).astype(BF16)
    lb_all = jnp.cumsum(jax.nn.softmax(lb_param.astype(F32), axis=0), axis=0)
    return {
        "norm_mix": norm_mix[l].reshape(1, D_MODEL).astype(F32),
        "w_in": w_in[l].astype(BF16),
        "wb": wb,
        "a_re": ab_re.reshape(1, S5_FLAT),
        "a_im": ab_im.reshape(1, S5_FLAT),
        "wc": wc,
        "d_skip": s5_d[l].reshape(1, S5_WIDTH).astype(F32),
        "w_glu": s5_w_glu[l].astype(BF16),
        "lb": lb_all[l].reshape(1, HG_WIDTH),
        "g_norm": hg_norm[l].reshape(1, HG_DIM).astype(F32),
        "w_out": w_out[l].astype(BF16),
        "mask_64": _segment_mask(64).astype(BF16),
        "mask_8": _segment_mask(8).astype(BF16),
        "norm_ffn": norm_ffn[l].reshape(1, D_MODEL).astype(F32),
        "w_gate": w_gate[l].astype(BF16),
        "w_up": w_up[l].astype(BF16),
        "w_down": w_down[l].astype(BF16),
        "norm_final": norm_final.reshape(1, D_MODEL).astype(F32),
    }


def _trunk(x, state, w, *, nb, tt, tm):
    B, T, _ = x.shape
    x1, h_re, h_im, s_new = _mixer(x, state, w, nb=nb, tt=tt)
    y = _ffn(x1.reshape(B * T, D_MODEL), w, tm=tm).reshape(B, T, D_MODEL)
    return (y, h_re.reshape(1, B, S5_GROUPS, S5_STATE), h_im.reshape(1, B, S5_GROUPS, S5_STATE),
            s_new.reshape(1, B, HG_HEADS, HG_DIM, HG_DIM))


def kernel(x_prompt, x_sample, state_s5_re, state_s5_im, state_hgrn, lb_param, norm_mix, w_in,
           s5_a_re, s5_a_im, s5_log_dt, s5_b_re, s5_b_im, s5_c_re, s5_c_im, s5_d, s5_w_glu,
           hg_norm, w_out, norm_ffn, w_gate, w_up, w_down, norm_final):
    assert norm_mix.shape[0] == 1, "single-layer trunk"
    w = _prepare(lb_param, norm_mix, w_in, s5_a_re, s5_a_im, s5_log_dt, s5_b_re, s5_b_im,
                 s5_c_re, s5_c_im, s5_d, s5_w_glu, hg_norm, w_out, norm_ffn, w_gate, w_up,
                 w_down, norm_final)
    bs = x_sample.shape[0]
    y_p, p_re, p_im, p_hg = _trunk(x_prompt, None, w, nb=8, tt=64, tm=512)
    st = (state_s5_re[0].reshape(bs, S5_FLAT), state_s5_im[0].reshape(bs, S5_FLAT), state_hgrn[0])
    y_s, s_re, s_im, s_hg = _trunk(x_sample, st, w, nb=16, tt=8, tm=512)
    return (y_p, y_s, p_re, p_im, p_hg, s_re, s_im, s_hg)
```

```python
import functools
import math

import jax
import jax.numpy as jnp
from jax import lax
from jax.experimental import pallas as pl
from jax.experimental.pallas import tpu as pltpu

F32 = jnp.float32
BF16 = jnp.bfloat16

D_MODEL = 1024
S5_WIDTH = 512
S5_GROUP = 16
S5_GROUPS = 32
S5_STATE = 64
S5_FLAT = S5_GROUPS * S5_STATE
HG_WIDTH = 512
HG_DIM = 128
HG_HEADS = 4
IN_COLS = S5_WIDTH + 4 * HG_WIDTH
EPS = 1e-6

LANES = 128
SUBLANES = 8
CHUNK = 64
N_SLABS = S5_FLAT // LANES
HALF_U = S5_WIDTH // 2
HALF_SLABS = N_SLABS // 2
SUB = 16
READ_K = 256
SCAN_COLS = 256
SCAN_PAD = 4
VMEM_LIMIT = 56 * 1024 * 1024


def _seq_pitch(tt):
    return tt + SCAN_PAD if tt == CHUNK else tt


def _dot(a, b):
    return jnp.dot(a, b, preferred_element_type=F32)


def _dot_nt(a, b):
    return lax.dot_general(a, b, (((1,), (1,)), ((), ())), preferred_element_type=F32)


def _dot_tn(a, b):
    return lax.dot_general(a, b, (((0,), (0,)), ((), ())), preferred_element_type=F32)


def _split3(x):
    hi = x.astype(BF16)
    r1 = x - hi.astype(F32)
    mid = r1.astype(BF16)
    lo = (r1 - mid.astype(F32)).astype(BF16)
    return hi, mid, lo


def _rms(x, g):
    return x * lax.rsqrt(jnp.mean(x * x, axis=-1, keepdims=True) + EPS) * g


def _sigmoid(x):
    return jax.nn.sigmoid(x)


def _gelu_tanh(x):
    c = math.sqrt(2.0 / math.pi)
    return 0.5 * x * (1.0 + jnp.tanh(c * (x + 0.044715 * (x * x * x))))


def _mixer_kernel(*refs, nb, tt, has_state):
    it = iter(refs)
    x_ref = next(it)
    if has_state:
        h0re_ref, h0im_ref, s0_ref = next(it), next(it), next(it)
    (norm_ref, win_ref, wb_ref, are_ref, aim_ref, wc_ref, dskip_ref, wglu_ref, lb_ref,
     gnorm_ref, wout_ref, mask_ref) = (next(it) for _ in range(12))
    x1_ref, hre_ref, him_ref, s_ref = (next(it) for _ in range(4))
    proj_scr, bre_scr, bim_scr, mix_scr = (next(it) for _ in range(4))

    bpc = CHUNK // tt
    nchunks = nb // bpc
    pitch = _seq_pitch(tt)
    cpitch = pitch * bpc
    ngroups = nb // SUBLANES
    state_t = tt == CHUNK
    assert not (state_t and has_state)

    tstep = pl.program_id(1)

    @pl.when(tstep == 0)
    def _init():
        if has_state:
            hre_ref[...] = h0re_ref[...]
            him_ref[...] = h0im_ref[...]
        else:
            hre_ref[...] = jnp.zeros_like(hre_ref)
            him_ref[...] = jnp.zeros_like(him_ref)
            s_ref[...] = jnp.zeros_like(s_ref)

    rows_tile = nb * tt
    scr_row0 = [c * cpitch for c in range(nchunks)]

    hn = _rms(x_ref[...].reshape(rows_tile, D_MODEL), norm_ref[...]).astype(BF16)
    proj_scr[:, 0:S5_WIDTH] = _dot(hn, win_ref[:, 0:S5_WIDTH])
    ub = proj_scr[:, 0:S5_WIDTH].astype(BF16)
    for hf in range(2):
        ubh = ub[:, hf * HALF_U:(hf + 1) * HALF_U]
        for part, scr in enumerate((bre_scr, bim_scr)):
            cols = slice(part * HALF_SLABS * LANES, (part + 1) * HALF_SLABS * LANES)
            bu = _dot(ubh, wb_ref[hf, :, cols])
            for jj in range(HALF_SLABS):
                for c in range(nchunks):
                    scr[hf * HALF_SLABS + jj, scr_row0[c]:scr_row0[c] + CHUNK, :] = (
                        bu[c * CHUNK:(c + 1) * CHUNK, jj * LANES:(jj + 1) * LANES])

    n_rest = (IN_COLS - S5_WIDTH) // SCAN_COLS
    steps_per_iter = tt // n_rest
    for i in range(n_rest):
        n0 = S5_WIDTH + i * SCAN_COLS
        proj_scr[:, n0:n0 + SCAN_COLS] = _dot(hn, win_ref[:, n0:n0 + SCAN_COLS])
        t0 = i * steps_per_iter
        for g in range(ngroups):
            base = g * SUBLANES * pitch
            grows = slice(g * SUBLANES, (g + 1) * SUBLANES)
            for j in range(N_SLABS):
                lanes = slice(j * LANES, (j + 1) * LANES)
                a_re = jnp.broadcast_to(are_ref[:, lanes], (SUBLANES, LANES))
                a_im = jnp.broadcast_to(aim_ref[:, lanes], (SUBLANES, LANES))
                h_re = hre_ref[grows, lanes]
                h_im = him_ref[grows, lanes]
                for dt in range(steps_per_iter):
                    idx = pl.ds(base + t0 + dt, SUBLANES, stride=pitch)
                    n_re = a_re * h_re - a_im * h_im + bre_scr[j, idx, :]
                    n_im = a_re * h_im + a_im * h_re + bim_scr[j, idx, :]
                    bre_scr[j, idx, :] = n_re
                    bim_scr[j, idx, :] = n_im
                    h_re, h_im = n_re, n_im
                hre_ref[grows, lanes] = h_re
                him_ref[grows, lanes] = h_im

    ys = [None, None]
    kb_per_half = 2 * HALF_SLABS * LANES // READ_K

    def readout_piece(p):
        hf, kb = divmod(p, kb_per_half)
        scr = bre_scr if kb < kb_per_half // 2 else bim_scr
        j0 = hf * HALF_SLABS + (READ_K // LANES) * (kb % (kb_per_half // 2))
        piece = jnp.concatenate(
            [jnp.concatenate([scr[j0 + d, scr_row0[c]:scr_row0[c] + CHUNK, :]
                              for c in range(nchunks)], axis=0)
             for d in range(READ_K // LANES)], axis=1).astype(BF16)
        part = _dot(piece, wc_ref[hf, kb * READ_K:(kb + 1) * READ_K, :])
        ys[hf] = part if ys[hf] is None else ys[hf] + part

    def readout_finish():
        y = jnp.concatenate(ys, axis=1) + dskip_ref[...] * proj_scr[:, 0:S5_WIDTH]
        gact = _gelu_tanh(y)
        out5 = gact * _sigmoid(_dot(gact.astype(BF16), wglu_ref[...]))
        mix_scr[:, 0:S5_WIDTH] = out5.astype(BF16)

    n_pieces = 2 * kb_per_half
    if not state_t:
        for p in range(n_pieces):
            readout_piece(p)
        readout_finish()

    lb = lb_ref[...]
    seg_mask = mask_ref[...].astype(F32) > 0.5
    zeros_t = jnp.zeros((tt, HG_DIM), F32)
    dec_rows = jnp.concatenate(
        [jnp.zeros((3 * tt, HG_DIM), F32), jnp.ones((3 * tt, HG_DIM), F32)], axis=1)
    heads = range(HG_HEADS)

    def phase3(c, carry, hooks=(None, None)):
        r0 = pl.multiple_of(c * CHUNK, CHUNK)
        rows = pl.ds(r0, CHUNK)
        q_all = proj_scr[rows, S5_WIDTH:S5_WIDTH + HG_WIDTH]
        fz = proj_scr[rows, S5_WIDTH + HG_WIDTH:S5_WIDTH + 2 * HG_WIDTH]
        iv_all = proj_scr[rows, S5_WIDTH + 2 * HG_WIDTH:S5_WIDTH + 3 * HG_WIDTH]
        og_all = proj_scr[rows, S5_WIDTH + 3 * HG_WIDTH:S5_WIDTH + 4 * HG_WIDTH]
        f = lb + (1.0 - lb) * _sigmoid(fz)
        lf_all = jnp.log(f)
        k_all = 1.0 - f
        lf_pieces = _split3(lf_all)
        g_all = sum(_dot(mask_ref[...], p) for p in lf_pieces)
        vb_all = iv_all.astype(BF16)
        qg_all = (q_all * jnp.exp(g_all)).astype(BF16)
        hs = [slice(h * HG_DIM, (h + 1) * HG_DIM) for h in heads]

        qs, ks = [], []
        for h in heads:
            gh, qh, kh = g_all[:, hs[h]], q_all[:, hs[h]], k_all[:, hs[h]]
            if tt == CHUNK:
                nblk = CHUNK // SUB
                blocks = [slice(b * SUB, (b + 1) * SUB) for b in range(nblk)]
                refs = [gh[b * SUB + SUB // 2:b * SUB + SUB // 2 + 1, :] for b in range(nblk)]
                ref_rows = jnp.concatenate(
                    [jnp.broadcast_to(r, (SUB, HG_DIM)) for r in refs], axis=0)
                q_loc = (qh * jnp.exp(gh - ref_rows)).astype(BF16)
                k_loc = kh * jnp.exp(ref_rows - gh)
                no_keys = jnp.zeros((SUB, HG_DIM), BF16)
                for i in range(nblk):
                    qs.append(q_loc[blocks[i]])
                    parts = [(k_loc[blocks[j]] * jnp.exp(refs[i] - refs[j])).astype(BF16)
                             for j in range(i)]
                    parts.append(k_loc[blocks[i]].astype(BF16))
                    parts += [no_keys] * (nblk - 1 - i)
                    ks.append(jnp.concatenate(parts, axis=0))
            else:
                g3 = gh.reshape(bpc, tt, HG_DIM)
                ref3 = jnp.broadcast_to(g3[:, tt // 2:tt // 2 + 1, :], (bpc, tt, HG_DIM))
                ref_rows = ref3.reshape(CHUNK, HG_DIM)
                qs.append((qh * jnp.exp(gh - ref_rows)).astype(BF16))
                ks.append((kh * jnp.exp(ref_rows - gh)).astype(BF16))

        if hooks[0] is not None:
            hooks[0]()
        att = [_dot_nt(a, b) for a, b in zip(qs, ks)]

        o_inter = []
        for h in heads:
            gh, kh = g_all[:, hs[h]], k_all[:, hs[h]]
            oi = []
            for bb in range(bpc):
                rs = slice(bb * tt, (bb + 1) * tt)
                b_idx = c * bpc + bb
                s_old = s0_ref[b_idx, h] if has_state else s_ref[b_idx, h]
                g_last = gh[bb * tt + tt - 1:bb * tt + tt, :]
                kd = kh[rs] * jnp.exp(g_last - gh[rs])
                if state_t:
                    oi.append(_dot_nt(qg_all[rs, hs[h]], s_old.astype(BF16)))
                    s_ref[b_idx, h] = (jnp.exp(g_last) * s_old
                                       + _dot_tn(vb_all[rs, hs[h]], kd.astype(BF16)))
                else:
                    oi.append(_dot(qg_all[rs, hs[h]], s_old.astype(BF16)))
                    lhs = jnp.concatenate(
                        [kd] + [p[rs, hs[h]].astype(F32) for p in lf_pieces], axis=0)
                    rhs = jnp.concatenate(
                        [jnp.concatenate([iv_all[rs, hs[h]], zeros_t], axis=1), dec_rows], axis=0)
                    upd = _dot_tn(lhs.astype(BF16), rhs.astype(BF16))
                    s_ref[b_idx, h] = jnp.exp(upd[:, HG_DIM:]) * s_old + upd[:, :HG_DIM]
            o_inter.append(oi[0] if bpc == 1 else jnp.concatenate(oi, axis=0))

        if hooks[1] is not None:
            hooks[1]()
        nblk = len(att) // HG_HEADS
        for h in heads:
            sc = att[h * nblk:(h + 1) * nblk]
            sc = sc[0] if nblk == 1 else jnp.concatenate(sc, axis=0)
            sc = jnp.where(seg_mask, sc, 0.0).astype(BF16)
            o = _dot(sc, vb_all[:, hs[h]]) + o_inter[h]
            o = _rms(o, gnorm_ref[...])
            ogh = og_all[:, hs[h]]
            mix_scr[rows, S5_WIDTH + h * HG_DIM:S5_WIDTH + (h + 1) * HG_DIM] = (
                o * (ogh * _sigmoid(ogh))).astype(BF16)
        return carry

    if state_t:
        per_chunk = n_pieces // nchunks
        assert per_chunk == 2
        for c_static in range(nchunks):
            phase3(c_static, 0, hooks=tuple(
                functools.partial(readout_piece, c_static * per_chunk + k) for k in range(per_chunk)))
        readout_finish()
    else:
        lax.fori_loop(0, nchunks, phase3, 0)

    if state_t:
        @pl.when(tstep == pl.num_programs(1) - 1)
        def _untranspose_state():
            for b in range(nb):
                for h in heads:
                    s_ref[b, h] = s_ref[b, h].T

    x1 = x_ref[...].reshape(rows_tile, D_MODEL) + _dot(mix_scr[...], wout_ref[...])
    x1_ref[...] = x1.reshape(nb, tt, D_MODEL)


def _const_spec(shape):
    nd = len(shape)
    return pl.BlockSpec(shape, lambda i, j, _nd=nd: (0,) * _nd, pipeline_mode=pl.Buffered(1))


def _mixer(x, state, w, *, nb, tt):
    B, T, _ = x.shape
    has_state = state is not None
    grid = (B // nb, T // tt)
    bpc = CHUNK // tt
    nchunks = nb // bpc
    pitch = _seq_pitch(tt)
    scr_rows = nchunks * pitch * bpc

    consts = [w["norm_mix"], w["w_in"], w["wb"], w["a_re"], w["a_im"], w["wc"], w["d_skip"],
              w["w_glu"], w["lb"], w["g_norm"], w["w_out"], w["mask_" + str(tt)]]
    in_specs = [pl.BlockSpec((nb, tt, D_MODEL), lambda i, j: (i, j, 0))]
    args = [x]
    if has_state:
        in_specs += [pl.BlockSpec((nb, S5_FLAT), lambda i, j: (i, 0)),
                     pl.BlockSpec((nb, S5_FLAT), lambda i, j: (i, 0)),
                     pl.BlockSpec((nb, HG_HEADS, HG_DIM, HG_DIM), lambda i, j: (i, 0, 0, 0))]
        args += list(state)
    in_specs += [_const_spec(c.shape) for c in consts]
    args += consts

    out_shape = (jax.ShapeDtypeStruct((B, T, D_MODEL), F32),
                 jax.ShapeDtypeStruct((B, S5_FLAT), F32),
                 jax.ShapeDtypeStruct((B, S5_FLAT), F32),
                 jax.ShapeDtypeStruct((B, HG_HEADS, HG_DIM, HG_DIM), F32))
    out_specs = (pl.BlockSpec((nb, tt, D_MODEL), lambda i, j: (i, j, 0)),
                 pl.BlockSpec((nb, S5_FLAT), lambda i, j: (i, 0)),
                 pl.BlockSpec((nb, S5_FLAT), lambda i, j: (i, 0)),
                 pl.BlockSpec((nb, HG_HEADS, HG_DIM, HG_DIM), lambda i, j: (i, 0, 0, 0)))
    scratch = [pltpu.VMEM((nb * tt, IN_COLS), F32),
               pltpu.VMEM((N_SLABS, scr_rows, LANES), F32),
               pltpu.VMEM((N_SLABS, scr_rows, LANES), F32),
               pltpu.VMEM((nb * tt, D_MODEL), BF16)]
    return pl.pallas_call(
        functools.partial(_mixer_kernel, nb=nb, tt=tt, has_state=has_state),
        grid=grid, in_specs=in_specs, out_specs=out_specs, out_shape=out_shape,
        scratch_shapes=scratch,
        compiler_params=pltpu.CompilerParams(
            dimension_semantics=("arbitrary", "arbitrary"), vmem_limit_bytes=VMEM_LIMIT),
        name="mixer_t%d" % tt,
    )(*args)


FF_CHUNK = 256


def _ffn_kernel(x_ref, nffn_ref, wg_ref, wu_ref, wd_ref, nfin_ref, y_ref):
    x = x_ref[...]
    h2 = _rms(x, nffn_ref[...]).astype(BF16)
    d_ff = wg_ref.shape[1]
    acc = x
    for f0 in range(0, d_ff, FF_CHUNK):
        gate = _dot(h2, wg_ref[:, f0:f0 + FF_CHUNK])
        up = _dot(h2, wu_ref[:, f0:f0 + FF_CHUNK])
        act = (gate * _sigmoid(gate) * up).astype(BF16)
        acc = acc + _dot(act, wd_ref[f0:f0 + FF_CHUNK, :])
    y_ref[...] = _rms(acc, nfin_ref[...])


def _ffn(x2d, w, *, tm):
    n = x2d.shape[0]
    d_ff = w["w_gate"].shape[1]

    def cspec(shape):
        return pl.BlockSpec(shape, lambda i: (0, 0), pipeline_mode=pl.Buffered(1))

    return pl.pallas_call(
        _ffn_kernel,
        grid=(n // tm,),
        in_specs=[pl.BlockSpec((tm, D_MODEL), lambda i: (i, 0)),
                  cspec((1, D_MODEL)), cspec((D_MODEL, d_ff)), cspec((D_MODEL, d_ff)),
                  cspec((d_ff, D_MODEL)), cspec((1, D_MODEL))],
        out_specs=pl.BlockSpec((tm, D_MODEL), lambda i: (i, 0)),
        out_shape=jax.ShapeDtypeStruct((n, D_MODEL), F32),
        compiler_params=pltpu.CompilerParams(
            dimension_semantics=("arbitrary",), vmem_limit_bytes=VMEM_LIMIT),
        name="ffn",
    )(x2d, w["norm_ffn"], w["w_gate"], w["w_up"], w["w_down"], w["norm_final"])


def _block_diag_halves(blocks):
    g, r, c = blocks.shape
    half = g // 2
    eye = jnp.eye(half, dtype=blocks.dtype)
    b = blocks.reshape(2, half, r, c)
    return jnp.einsum("hgrc,gk->hgrkc", b, eye).reshape(2, half * r, half * c)


def _segment_mask(tt):
    r = jnp.arange(CHUNK)
    same = (r[:, None] // tt) == (r[None, :] // tt)
    return (same & (r[:, None] >= r[None, :])).astype(F32)


def _prepare(lb_param, norm_mix, w_in, s5_a_re, s5_a_im, s5_log_dt, s5_b_re, s5_b_im, s5_c_re,
             s5_c_im, s5_d, s5_w_glu, hg_norm, w_out, norm_ffn, w_gate, w_up, w_down, norm_final):
    l = 0
    a_re = s5_a_re[l].astype(F32)
    a_im = s5_a_im[l].astype(F32)
    dt = jnp.exp(s5_log_dt[l].astype(F32))[:, None]
    mag = jnp.exp(a_re * dt)
    ab_re = mag * jnp.cos(a_im * dt)
    ab_im = mag * jnp.sin(a_im * dt)
    den = a_re * a_re + a_im * a_im
    nr = ab_re - 1.0
    ni = ab_im
    f_re = (nr * a_re + ni * a_im) / den
    f_im = (ni * a_re - nr * a_im) / den
    b_re = s5_b_re[l].astype(F32)
    b_im = s5_b_im[l].astype(F32)
    bb_re = f_re[..., None] * b_re - f_im[..., None] * b_im
    bb_im = f_re[..., None] * b_im + f_im[..., None] * b_re
    wb = jnp.concatenate([_block_diag_halves(bb_re.transpose(0, 2, 1)),
                          _block_diag_halves(bb_im.transpose(0, 2, 1))], axis=2).astype(BF16)
    wc = jnp.concatenate([_block_diag_halves(s5_c_re[l].astype(F32).transpose(0, 2, 1)),
                          _block_diag_halves(-s5_c_im[l].astype(F32).transpose(0, 2, 1))],
                         axis=1).astype(BF16)
    lb_all = jnp.cumsum(jax.nn.softmax(lb_param.astype(F32), axis=0), axis=0)
    return {
        "norm_mix": norm_mix[l].reshape(1, D_MODEL).astype(F32),
        "w_in": w_in[l].astype(BF16),
        "wb": wb,
        "a_re": ab_re.reshape(1, S5_FLAT),
        "a_im": ab_im.reshape(1, S5_FLAT),
        "wc": wc,
        "d_skip": s5_d[l].reshape(1, S5_WIDTH).astype(F32),
        "w_glu": s5_w_glu[l].astype(BF16),
        "lb": lb_all[l].reshape(1, HG_WIDTH),
        "g_norm": hg_norm[l].reshape(1, HG_DIM).astype(F32),
        "w_out": w_out[l].astype(BF16),
        "mask_64": _segment_mask(64).astype(BF16),
        "mask_8": _segment_mask(8).astype(BF16),
        "norm_ffn": norm_ffn[l].reshape(1, D_MODEL).astype(F32),
        "w_gate": w_gate[l].astype(BF16),
        "w_up": w_up[l].astype(BF16),
        "w_down": w_down[l].astype(BF16),
        "norm_final": norm_final.reshape(1, D_MODEL).astype(F32),
    }


def _trunk(x, state, w, *, nb, tt, tm):
    B, T, _ = x.shape
    x1, h_re, h_im, s_new = _mixer(x, state, w, nb=nb, tt=tt)
    y = _ffn(x1.reshape(B * T, D_MODEL), w, tm=tm).reshape(B, T, D_MODEL)
    return (y, h_re.reshape(1, B, S5_GROUPS, S5_STATE), h_im.reshape(1, B, S5_GROUPS, S5_STATE),
            s_new.reshape(1, B, HG_HEADS, HG_DIM, HG_DIM))


def kernel(x_prompt, x_sample, state_s5_re, state_s5_im, state_hgrn, lb_param, norm_mix, w_in,
           s5_a_re, s5_a_im, s5_log_dt, s5_b_re, s5_b_im, s5_c_re, s5_c_im, s5_d, s5_w_glu,
           hg_norm, w_out, norm_ffn, w_gate, w_up, w_down, norm_final):
    assert norm_mix.shape[0] == 1, "single-layer trunk"
    w = _prepare(lb_param, norm_mix, w_in, s5_a_re, s5_a_im, s5_log_dt, s5_b_re, s5_b_im,
                 s5_c_re, s5_c_im, s5_d, s5_w_glu, hg_norm, w_out, norm_ffn, w_gate, w_up,
                 w_down, norm_final)
    bs = x_sample.shape[0]
    y_p, p_re, p_im, p_hg = _trunk(x_prompt, None, w, nb=8, tt=64, tm=512)
    st = (state_s5_re[0].reshape(bs, S5_FLAT), state_s5_im[0].reshape(bs, S5_FLAT), state_hgrn[0])
    y_s, s_re, s_im, s_hg = _trunk(x_sample, st, w, nb=16, tt=8, tm=512)
    return (y_p, y_s, p_re, p_im, p_hg, s_re, s_im, s_hg)
```

```python
import functools
import math

import jax
import jax.numpy as jnp
from jax import lax
from jax.experimental import pallas as pl
from jax.experimental.pallas import tpu as pltpu

F32 = jnp.float32
BF16 = jnp.bfloat16

D_MODEL = 1024
S5_WIDTH = 512
S5_GROUP = 16
S5_GROUPS = 32
S5_STATE = 64
S5_FLAT = S5_GROUPS * S5_STATE
HG_WIDTH = 512
HG_DIM = 128
HG_HEADS = 4
IN_COLS = S5_WIDTH + 4 * HG_WIDTH
EPS = 1e-6

LANES = 128
SUBLANES = 8
CHUNK = 64
N_SLABS = S5_FLAT // LANES
HALF_U = S5_WIDTH // 2
HALF_SLABS = N_SLABS // 2
SUB = 16
READ_K = 256
SCAN_COLS = 256
SCAN_PAD = 4
VMEM_LIMIT = 56 * 1024 * 1024


def _seq_pitch(tt):
    return tt + SCAN_PAD if tt == CHUNK else tt


def _dot(a, b):
    return jnp.dot(a, b, preferred_element_type=F32)


def _dot_nt(a, b):
    return lax.dot_general(a, b, (((1,), (1,)), ((), ())), preferred_element_type=F32)


def _dot_tn(a, b):
    return lax.dot_general(a, b, (((0,), (0,)), ((), ())), preferred_element_type=F32)


def _split3(x):
    hi = x.astype(BF16)
    r1 = x - hi.astype(F32)
    mid = r1.astype(BF16)
    lo = (r1 - mid.astype(F32)).astype(BF16)
    return hi, mid, lo


def _rms(x, g):
    return x * lax.rsqrt(jnp.mean(x * x, axis=-1, keepdims=True) + EPS) * g


def _sigmoid(x):
    return jax.nn.sigmoid(x)


def _gelu_tanh(x):
    c = math.sqrt(2.0 / math.pi)
    return 0.5 * x * (1.0 + jnp.tanh(c * (x + 0.044715 * (x * x * x))))


def _mixer_kernel(*refs, nb, tt, has_state):
    it = iter(refs)
    x_ref = next(it)
    if has_state:
        h0re_ref, h0im_ref, s0_ref = next(it), next(it), next(it)
    (norm_ref, win_ref, wb_ref, are_ref, aim_ref, wc_ref, dskip_ref, wglu_ref, lb_ref,
     gnorm_ref, wout_ref, mask_ref) = (next(it) for _ in range(12))
    x1_ref, hre_ref, him_ref, s_ref = (next(it) for _ in range(4))
    proj_scr, bre_scr, bim_scr, mix_scr = (next(it) for _ in range(4))

    bpc = CHUNK // tt
    nchunks = nb // bpc
    pitch = _seq_pitch(tt)
    cpitch = pitch * bpc
    ngroups = nb // SUBLANES
    state_t = tt == CHUNK
    assert not (state_t and has_state)

    tstep = pl.program_id(1)

    @pl.when(tstep == 0)
    def _init():
        if has_state:
            hre_ref[...] = h0re_ref[...]
            him_ref[...] = h0im_ref[...]
        else:
            hre_ref[...] = jnp.zeros_like(hre_ref)
            him_ref[...] = jnp.zeros_like(him_ref)
            s_ref[...] = jnp.zeros_like(s_ref)

    rows_tile = nb * tt
    scr_row0 = [c * cpitch for c in range(nchunks)]

    hn = _rms(x_ref[...].reshape(rows_tile, D_MODEL), norm_ref[...]).astype(BF16)
    proj_scr[:, 0:S5_WIDTH] = _dot(hn, win_ref[:, 0:S5_WIDTH])
    ub = proj_scr[:, 0:S5_WIDTH].astype(BF16)
    for hf in range(2):
        ubh = ub[:, hf * HALF_U:(hf + 1) * HALF_U]
        for part, scr in enumerate((bre_scr, bim_scr)):
            cols = slice(part * HALF_SLABS * LANES, (part + 1) * HALF_SLABS * LANES)
            bu = _dot(ubh, wb_ref[hf, :, cols])
            for jj in range(HALF_SLABS):
                for c in range(nchunks):
                    scr[hf * HALF_SLABS + jj, scr_row0[c]:scr_row0[c] + CHUNK, :] = (
                        bu[c * CHUNK:(c + 1) * CHUNK, jj * LANES:(jj + 1) * LANES])

    n_rest = (IN_COLS - S5_WIDTH) // SCAN_COLS
    steps_per_iter = tt // n_rest
    for i in range(n_rest):
        n0 = S5_WIDTH + i * SCAN_COLS
        proj_scr[:, n0:n0 + SCAN_COLS] = _dot(hn, win_ref[:, n0:n0 + SCAN_COLS])
        t0 = i * steps_per_iter
        for g in range(ngroups):
            base = g * SUBLANES * pitch
            grows = slice(g * SUBLANES, (g + 1) * SUBLANES)
            for j in range(N_SLABS):
                lanes = slice(j * LANES, (j + 1) * LANES)
                a_re = jnp.broadcast_to(are_ref[:, lanes], (SUBLANES, LANES))
                a_im = jnp.broadcast_to(aim_ref[:, lanes], (SUBLANES, LANES))
                h_re = hre_ref[grows, lanes]
                h_im = him_ref[grows, lanes]
                for dt in range(steps_per_iter):
                    idx = pl.ds(base + t0 + dt, SUBLANES, stride=pitch)
                    n_re = a_re * h_re - a_im * h_im + bre_scr[j, idx, :]
                    n_im = a_re * h_im + a_im * h_re + bim_scr[j, idx, :]
                    bre_scr[j, idx, :] = n_re
                    bim_scr[j, idx, :] = n_im
                    h_re, h_im = n_re, n_im
                hre_ref[grows, lanes] = h_re
                him_ref[grows, lanes] = h_im

    ys = [None, None]
    kb_per_half = 2 * HALF_SLABS * LANES // READ_K

    def readout_piece(p):
        hf, kb = divmod(p, kb_per_half)
        scr = bre_scr if kb < kb_per_half // 2 else bim_scr
        j0 = hf * HALF_SLABS + (READ_K // LANES) * (kb % (kb_per_half // 2))
        piece = jnp.concatenate(
            [jnp.concatenate([scr[j0 + d, scr_row0[c]:scr_row0[c] + CHUNK, :]
                              for c in range(nchunks)], axis=0)
             for d in range(READ_K // LANES)], axis=1).astype(BF16)
        part = _dot(piece, wc_ref[hf, kb * READ_K:(kb + 1) * READ_K, :])
        ys[hf] = part if ys[hf] is None else ys[hf] + part

    def readout_finish():
        y = jnp.concatenate(ys, axis=1) + dskip_ref[...] * proj_scr[:, 0:S5_WIDTH]
        gact = _gelu_tanh(y)
        out5 = gact * _sigmoid(_dot(gact.astype(BF16), wglu_ref[...]))
        mix_scr[:, 0:S5_WIDTH] = out5.astype(BF16)

    n_pieces = 2 * kb_per_half
    if not state_t:
        for p in range(n_pieces):
            readout_piece(p)
        readout_finish()

    lb = lb_ref[...]
    seg_mask = mask_ref[...].astype(F32) > 0.5
    zeros_t = jnp.zeros((tt, HG_DIM), F32)
    dec_rows = jnp.concatenate(
        [jnp.zeros((3 * tt, HG_DIM), F32), jnp.ones((3 * tt, HG_DIM), F32)], axis=1)
    heads = range(HG_HEADS)

    def chunk_stages(c):
        r0 = pl.multiple_of(c * CHUNK, CHUNK)
        rows = pl.ds(r0, CHUNK)
        q_all = proj_scr[rows, S5_WIDTH:S5_WIDTH + HG_WIDTH]
        fz = proj_scr[rows, S5_WIDTH + HG_WIDTH:S5_WIDTH + 2 * HG_WIDTH]
        iv_all = proj_scr[rows, S5_WIDTH + 2 * HG_WIDTH:S5_WIDTH + 3 * HG_WIDTH]
        og_all = proj_scr[rows, S5_WIDTH + 3 * HG_WIDTH:S5_WIDTH + 4 * HG_WIDTH]
        f = lb + (1.0 - lb) * _sigmoid(fz)
        lf_all = jnp.log(f)
        k_all = 1.0 - f
        lf_pieces = _split3(lf_all)
        g_all = sum(_dot(mask_ref[...], p) for p in lf_pieces)
        vb_all = iv_all.astype(BF16)
        qg_all = (q_all * jnp.exp(g_all)).astype(BF16)
        hs = [slice(h * HG_DIM, (h + 1) * HG_DIM) for h in heads]

        qs, ks = [], []
        for h in heads:
            gh, qh, kh = g_all[:, hs[h]], q_all[:, hs[h]], k_all[:, hs[h]]
            if tt == CHUNK:
                nblk = CHUNK // SUB
                blocks = [slice(b * SUB, (b + 1) * SUB) for b in range(nblk)]
                refs = [gh[b * SUB + SUB // 2:b * SUB + SUB // 2 + 1, :] for b in range(nblk)]
                ref_rows = jnp.concatenate(
                    [jnp.broadcast_to(r, (SUB, HG_DIM)) for r in refs], axis=0)
                q_loc = (qh * jnp.exp(gh - ref_rows)).astype(BF16)
                k_loc = kh * jnp.exp(ref_rows - gh)
                no_keys = jnp.zeros((SUB, HG_DIM), BF16)
                for i in range(nblk):
                    qs.append(q_loc[blocks[i]])
                    parts = [(k_loc[blocks[j]] * jnp.exp(refs[i] - refs[j])).astype(BF16)
                             for j in range(i)]
                    parts.append(k_loc[blocks[i]].astype(BF16))
                    parts += [no_keys] * (nblk - 1 - i)
                    ks.append(jnp.concatenate(parts, axis=0))
            else:
                g3 = gh.reshape(bpc, tt, HG_DIM)
                ref3 = jnp.broadcast_to(g3[:, tt // 2:tt // 2 + 1, :], (bpc, tt, HG_DIM))
                ref_rows = ref3.reshape(CHUNK, HG_DIM)
                qs.append((qh * jnp.exp(gh - ref_rows)).astype(BF16))
                ks.append((kh * jnp.exp(ref_rows - gh)).astype(BF16))

        yield
        att = [_dot_nt(a, b) for a, b in zip(qs, ks)]

        yield
        o_inter = []
        for h in heads:
            gh, kh = g_all[:, hs[h]], k_all[:, hs[h]]
            oi = []
            for bb in range(bpc):
                rs = slice(bb * tt, (bb + 1) * tt)
                b_idx = c * bpc + bb
                s_old = s0_ref[b_idx, h] if has_state else s_ref[b_idx, h]
                g_last = gh[bb * tt + tt - 1:bb * tt + tt, :]
                kd = kh[rs] * jnp.exp(g_last - gh[rs])
                if state_t:
                    oi.append(_dot_nt(qg_all[rs, hs[h]], s_old.astype(BF16)))
                    s_ref[b_idx, h] = (jnp.exp(g_last) * s_old
                                       + _dot_tn(vb_all[rs, hs[h]], kd.astype(BF16)))
                else:
                    oi.append(_dot(qg_all[rs, hs[h]], s_old.astype(BF16)))
                    lhs = jnp.concatenate(
                        [kd] + [p[rs, hs[h]].astype(F32) for p in lf_pieces], axis=0)
                    rhs = jnp.concatenate(
                        [jnp.concatenate([iv_all[rs, hs[h]], zeros_t], axis=1), dec_rows], axis=0)
                    upd = _dot_tn(lhs.astype(BF16), rhs.astype(BF16))
                    s_ref[b_idx, h] = jnp.exp(upd[:, HG_DIM:]) * s_old + upd[:, :HG_DIM]
            o_inter.append(oi[0] if bpc == 1 else jnp.concatenate(oi, axis=0))

        yield
        nblk = len(att) // HG_HEADS
        for h in heads:
            sc = att[h * nblk:(h + 1) * nblk]
            sc = sc[0] if nblk == 1 else jnp.concatenate(sc, axis=0)
            sc = jnp.where(seg_mask, sc, 0.0).astype(BF16)
            o = _dot(sc, vb_all[:, hs[h]]) + o_inter[h]
            o = _rms(o, gnorm_ref[...])
            ogh = og_all[:, hs[h]]
            mix_scr[rows, S5_WIDTH + h * HG_DIM:S5_WIDTH + (h + 1) * HG_DIM] = (
                o * (ogh * _sigmoid(ogh))).astype(BF16)

    def phase3(c, carry):
        for _ in chunk_stages(c):
            pass
        return carry

    if state_t:
        per_chunk = n_pieces // nchunks
        assert per_chunk == 2
        gens = [chunk_stages(c) for c in range(nchunks)]
        next(gens[0])
        for c in range(nchunks):
            next(gens[c])
            readout_piece(c * per_chunk)
            if c + 1 < nchunks:
                next(gens[c + 1])
            next(gens[c])
            readout_piece(c * per_chunk + 1)
            for _ in gens[c]:
                pass
        readout_finish()
    else:
        lax.fori_loop(0, nchunks, phase3, 0)

    if state_t:
        @pl.when(tstep == pl.num_programs(1) - 1)
        def _untranspose_state():
            for b in range(nb):
                for h in heads:
                    s_ref[b, h] = s_ref[b, h].T

    x1 = x_ref[...].reshape(rows_tile, D_MODEL) + _dot(mix_scr[...], wout_ref[...])
    x1_ref[...] = x1.reshape(nb, tt, D_MODEL)


def _const_spec(shape):
    nd = len(shape)
    return pl.BlockSpec(shape, lambda i, j, _nd=nd: (0,) * _nd, pipeline_mode=pl.Buffered(1))


def _mixer(x, state, w, *, nb, tt):
    B, T, _ = x.shape
    has_state = state is not None
    grid = (B // nb, T // tt)
    bpc = CHUNK // tt
    nchunks = nb // bpc
    pitch = _seq_pitch(tt)
    scr_rows = nchunks * pitch * bpc

    consts = [w["norm_mix"], w["w_in"], w["wb"], w["a_re"], w["a_im"], w["wc"], w["d_skip"],
              w["w_glu"], w["lb"], w["g_norm"], w["w_out"], w["mask_" + str(tt)]]
    in_specs = [pl.BlockSpec((nb, tt, D_MODEL), lambda i, j: (i, j, 0))]
    args = [x]
    if has_state:
        in_specs += [pl.BlockSpec((nb, S5_FLAT), lambda i, j: (i, 0)),
                     pl.BlockSpec((nb, S5_FLAT), lambda i, j: (i, 0)),
                     pl.BlockSpec((nb, HG_HEADS, HG_DIM, HG_DIM), lambda i, j: (i, 0, 0, 0))]
        args += list(state)
    in_specs += [_const_spec(c.shape) for c in consts]
    args += consts

    out_shape = (jax.ShapeDtypeStruct((B, T, D_MODEL), F32),
                 jax.ShapeDtypeStruct((B, S5_FLAT), F32),
                 jax.ShapeDtypeStruct((B, S5_FLAT), F32),
                 jax.ShapeDtypeStruct((B, HG_HEADS, HG_DIM, HG_DIM), F32))
    out_specs = (pl.BlockSpec((nb, tt, D_MODEL), lambda i, j: (i, j, 0)),
                 pl.BlockSpec((nb, S5_FLAT), lambda i, j: (i, 0)),
                 pl.BlockSpec((nb, S5_FLAT), lambda i, j: (i, 0)),
                 pl.BlockSpec((nb, HG_HEADS, HG_DIM, HG_DIM), lambda i, j: (i, 0, 0, 0)))
    scratch = [pltpu.VMEM((nb * tt, IN_COLS), F32),
               pltpu.VMEM((N_SLABS, scr_rows, LANES), F32),
               pltpu.VMEM((N_SLABS, scr_rows, LANES), F32),
               pltpu.VMEM((nb * tt, D_MODEL), BF16)]
    return pl.pallas_call(
        functools.partial(_mixer_kernel, nb=nb, tt=tt, has_state=has_state),
        grid=grid, in_specs=in_specs, out_specs=out_specs, out_shape=out_shape,
        scratch_shapes=scratch,
        compiler_params=pltpu.CompilerParams(
            dimension_semantics=("arbitrary", "arbitrary"), vmem_limit_bytes=VMEM_LIMIT),
        name="mixer_t%d" % tt,
    )(*args)


FF_CHUNK = 256


def _ffn_kernel(x_ref, nffn_ref, wg_ref, wu_ref, wd_ref, nfin_ref, y_ref):
    x = x_ref[...]
    h2 = _rms(x, nffn_ref[...]).astype(BF16)
    d_ff = wg_ref.shape[1]
    acc = x
    for f0 in range(0, d_ff, FF_CHUNK):
        gate = _dot(h2, wg_ref[:, f0:f0 + FF_CHUNK])
        up = _dot(h2, wu_ref[:, f0:f0 + FF_CHUNK])
        act = (gate * _sigmoid(gate) * up).astype(BF16)
        acc = acc + _dot(act, wd_ref[f0:f0 + FF_CHUNK, :])
    y_ref[...] = _rms(acc, nfin_ref[...])


def _ffn(x2d, w, *, tm):
    n = x2d.shape[0]
    d_ff = w["w_gate"].shape[1]

    def cspec(shape):
        return pl.BlockSpec(shape, lambda i: (0, 0), pipeline_mode=pl.Buffered(1))

    return pl.pallas_call(
        _ffn_kernel,
        grid=(n // tm,),
        in_specs=[pl.BlockSpec((tm, D_MODEL), lambda i: (i, 0)),
                  cspec((1, D_MODEL)), cspec((D_MODEL, d_ff)), cspec((D_MODEL, d_ff)),
                  cspec((d_ff, D_MODEL)), cspec((1, D_MODEL))],
        out_specs=pl.BlockSpec((tm, D_MODEL), lambda i: (i, 0)),
        out_shape=jax.ShapeDtypeStruct((n, D_MODEL), F32),
        compiler_params=pltpu.CompilerParams(
            dimension_semantics=("arbitrary",), vmem_limit_bytes=VMEM_LIMIT),
        name="ffn",
    )(x2d, w["norm_ffn"], w["w_gate"], w["w_up"], w["w_down"], w["norm_final"])


def _block_diag_halves(blocks):
    g, r, c = blocks.shape
    half = g // 2
    eye = jnp.eye(half, dtype=blocks.dtype)
    b = blocks.reshape(2, half, r, c)
    return jnp.einsum("hgrc,gk->hgrkc", b, eye).reshape(2, half * r, half * c)


def _segment_mask(tt):
    r = jnp.arange(CHUNK)
    same = (r[:, None] // tt) == (r[None, :] // tt)
    return (same & (r[:, None] >= r[None, :])).astype(F32)


def _prepare(lb_param, norm_mix, w_in, s5_a_re, s5_a_im, s5_log_dt, s5_b_re, s5_b_im, s5_c_re,
             s5_c_im, s5_d, s5_w_glu, hg_norm, w_out, norm_ffn, w_gate, w_up, w_down, norm_final):
    l = 0
    a_re = s5_a_re[l].astype(F32)
    a_im = s5_a_im[l].astype(F32)
    dt = jnp.exp(s5_log_dt[l].astype(F32))[:, None]
    mag = jnp.exp(a_re * dt)
    ab_re = mag * jnp.cos(a_im * dt)
    ab_im = mag * jnp.sin(a_im * dt)
    den = a_re * a_re + a_im * a_im
    nr = ab_re - 1.0
    ni = ab_im
    f_re = (nr * a_re + ni * a_im) / den
    f_im = (ni * a_re - nr * a_im) / den
    b_re = s5_b_re[l].astype(F32)
    b_im = s5_b_im[l].astype(F32)
    bb_re = f_re[..., None] * b_re - f_im[..., None] * b_im
    bb_im = f_re[..., None] * b_im + f_im[..., None] * b_re
    wb = jnp.concatenate([_block_diag_halves(bb_re.transpose(0, 2, 1)),
                          _block_diag_halves(bb_im.transpose(0, 2, 1))], axis=2).astype(BF16)
    wc = jnp.concatenate([_block_diag_halves(s5_c_re[l].astype(F32).transpose(0, 2, 1)),
                          _block_diag_halves(-s5_c_im[l].astype(F32).transpose(0, 2, 1))],
                         axis=1).astype(BF16)
    lb_all = jnp.cumsum(jax.nn.softmax(lb_param.astype(F32), axis=0), axis=0)
    return {
        "norm_mix": norm_mix[l].reshape(1, D_MODEL).astype(F32),
        "w_in": w_in[l].astype(BF16),
        "wb": wb,
        "a_re": ab_re.reshape(1, S5_FLAT),
        "a_im": ab_im.reshape(1, S5_FLAT),
        "wc": wc,
        "d_skip": s5_d[l].reshape(1, S5_WIDTH).astype(F32),
        "w_glu": s5_w_glu[l].astype(BF16),
        "lb": lb_all[l].reshape(1, HG_WIDTH),
        "g_norm": hg_norm[l].reshape(1, HG_DIM).astype(F32),
        "w_out": w_out[l].astype(BF16),
        "mask_64": _segment_mask(64).astype(BF16),
        "mask_8": _segment_mask(8).astype(BF16),
        "norm_ffn": norm_ffn[l].reshape(1, D_MODEL).astype(F32),
        "w_gate": w_gate[l].astype(BF16),
        "w_up": w_up[l].astype(BF16),
        "w_down": w_down[l].astype(BF16),
        "norm_final": norm_final.reshape(1, D_MODEL).astype(F32),
    }


def _trunk(x, state, w, *, nb, tt, tm):
    B, T, _ = x.shape
    x1, h_re, h_im, s_new = _mixer(x, state, w, nb=nb, tt=tt)
    y = _ffn(x1.reshape(B * T, D_MODEL), w, tm=tm).reshape(B, T, D_MODEL)
    return (y, h_re.reshape(1, B, S5_GROUPS, S5_STATE), h_im.reshape(1, B, S5_GROUPS, S5_STATE),
            s_new.reshape(1, B, HG_HEADS, HG_DIM, HG_DIM))


def kernel(x_prompt, x_sample, state_s5_re, state_s5_im, state_hgrn, lb_param, norm_mix, w_in,
           s5_a_re, s5_a_im, s5_log_dt, s5_b_re, s5_b_im, s5_c_re, s5_c_im, s5_d, s5_w_glu,
           hg_norm, w_out, norm_ffn, w_gate, w_up, w_down, norm_final):
    assert norm_mix.shape[0] == 1, "single-layer trunk"
    w = _prepare(lb_param, norm_mix, w_in, s5_a_re, s5_a_im, s5_log_dt, s5_b_re, s5_b_im,
                 s5_c_re, s5_c_im, s5_d, s5_w_glu, hg_norm, w_out, norm_ffn, w_gate, w_up,
                 w_down, norm_final)
    bs = x_sample.shape[0]
    y_p, p_re, p_im, p_hg = _trunk(x_prompt, None, w, nb=8, tt=64, tm=512)
    st = (state_s5_re[0].reshape(bs, S5_FLAT), state_s5_im[0].reshape(bs, S5_FLAT), state_hgrn[0])
    y_s, s_re, s_im, s_hg = _trunk(x_sample, st, w, nb=16, tt=8, tm=512)
    return (y_p, y_s, p_re, p_im, p_hg, s_re, s_im, s_hg)
```

```python
import functools
import math

import jax
import jax.numpy as jnp
from jax import lax
from jax.experimental import pallas as pl
from jax.experimental.pallas import tpu as pltpu

F32 = jnp.float32
BF16 = jnp.bfloat16

D_MODEL = 1024
S5_WIDTH = 512
S5_GROUP = 16
S5_GROUPS = 32
S5_STATE = 64
S5_FLAT = S5_GROUPS * S5_STATE
HG_WIDTH = 512
HG_DIM = 128
HG_HEADS = 4
IN_COLS = S5_WIDTH + 4 * HG_WIDTH
EPS = 1e-6

LANES = 128
SUBLANES = 8
CHUNK = 64
N_SLABS = S5_FLAT // LANES
HALF_U = S5_WIDTH // 2
HALF_SLABS = N_SLABS // 2
SUB = 16
READ_K = 256
SCAN_COLS = 256
SCAN_PAD = 4
VMEM_LIMIT = 56 * 1024 * 1024


def _seq_pitch(tt):
    return tt + SCAN_PAD if tt == CHUNK else tt


def _dot(a, b):
    return jnp.dot(a, b, preferred_element_type=F32)


def _dot_nt(a, b):
    return lax.dot_general(a, b, (((1,), (1,)), ((), ())), preferred_element_type=F32)


def _dot_tn(a, b):
    return lax.dot_general(a, b, (((0,), (0,)), ((), ())), preferred_element_type=F32)


def _split3(x):
    hi = x.astype(BF16)
    r1 = x - hi.astype(F32)
    mid = r1.astype(BF16)
    lo = (r1 - mid.astype(F32)).astype(BF16)
    return hi, mid, lo


def _rms(x, g):
    return x * lax.rsqrt(jnp.mean(x * x, axis=-1, keepdims=True) + EPS) * g


def _sigmoid(x):
    return jax.nn.sigmoid(x)


def _gelu_tanh(x):
    c = math.sqrt(2.0 / math.pi)
    return 0.5 * x * (1.0 + jnp.tanh(c * (x + 0.044715 * (x * x * x))))


def _mixer_kernel(*refs, nb, tt, has_state):
    it = iter(refs)
    x_ref = next(it)
    if has_state:
        h0re_ref, h0im_ref, s0_ref = next(it), next(it), next(it)
    (norm_ref, win_ref, wb_ref, are_ref, aim_ref, wc_ref, dskip_ref, wglu_ref, lb_ref,
     gnorm_ref, wout_ref, mask_ref) = (next(it) for _ in range(12))
    x1_ref, hre_ref, him_ref, s_ref = (next(it) for _ in range(4))
    proj_scr, bre_scr, bim_scr, mix_scr = (next(it) for _ in range(4))

    bpc = CHUNK // tt
    nchunks = nb // bpc
    pitch = _seq_pitch(tt)
    cpitch = pitch * bpc
    ngroups = nb // SUBLANES
    state_t = tt == CHUNK
    assert not (state_t and has_state)

    tstep = pl.program_id(1)

    @pl.when(tstep == 0)
    def _init():
        if has_state:
            hre_ref[...] = h0re_ref[...]
            him_ref[...] = h0im_ref[...]
        else:
            hre_ref[...] = jnp.zeros_like(hre_ref)
            him_ref[...] = jnp.zeros_like(him_ref)
            s_ref[...] = jnp.zeros_like(s_ref)

    rows_tile = nb * tt
    scr_row0 = [c * cpitch for c in range(nchunks)]

    hn = _rms(x_ref[...].reshape(rows_tile, D_MODEL), norm_ref[...]).astype(BF16)
    proj_scr[:, 0:S5_WIDTH] = _dot(hn, win_ref[:, 0:S5_WIDTH])
    ub = proj_scr[:, 0:S5_WIDTH].astype(BF16)
    for hf in range(2):
        ubh = ub[:, hf * HALF_U:(hf + 1) * HALF_U]
        for part, scr in enumerate((bre_scr, bim_scr)):
            cols = slice(part * HALF_SLABS * LANES, (part + 1) * HALF_SLABS * LANES)
            bu = _dot(ubh, wb_ref[hf, :, cols])
            for jj in range(HALF_SLABS):
                for c in range(nchunks):
                    scr[hf * HALF_SLABS + jj, scr_row0[c]:scr_row0[c] + CHUNK, :] = (
                        bu[c * CHUNK:(c + 1) * CHUNK, jj * LANES:(jj + 1) * LANES])

    n_rest = (IN_COLS - S5_WIDTH) // SCAN_COLS
    steps_per_iter = tt // n_rest
    for i in range(n_rest):
        n0 = S5_WIDTH + i * SCAN_COLS
        proj_scr[:, n0:n0 + SCAN_COLS] = _dot(hn, win_ref[:, n0:n0 + SCAN_COLS])
        t0 = i * steps_per_iter
        for g in range(ngroups):
            base = g * SUBLANES * pitch
            grows = slice(g * SUBLANES, (g + 1) * SUBLANES)
            for j in range(N_SLABS):
                lanes = slice(j * LANES, (j + 1) * LANES)
                a_re = jnp.broadcast_to(are_ref[:, lanes], (SUBLANES, LANES))
                a_im = jnp.broadcast_to(aim_ref[:, lanes], (SUBLANES, LANES))
                h_re = hre_ref[grows, lanes]
                h_im = him_ref[grows, lanes]
                for dt in range(steps_per_iter):
                    idx = pl.ds(base + t0 + dt, SUBLANES, stride=pitch)
                    n_re = a_re * h_re - a_im * h_im + bre_scr[j, idx, :]
                    n_im = a_re * h_im + a_im * h_re + bim_scr[j, idx, :]
                    bre_scr[j, idx, :] = n_re
                    bim_scr[j, idx, :] = n_im
                    h_re, h_im = n_re, n_im
                hre_ref[grows, lanes] = h_re
                him_ref[grows, lanes] = h_im

    ys = [None, None]
    kb_per_half = 2 * HALF_SLABS * LANES // READ_K

    def readout_piece(p):
        hf, kb = divmod(p, kb_per_half)
        scr = bre_scr if kb < kb_per_half // 2 else bim_scr
        j0 = hf * HALF_SLABS + (READ_K // LANES) * (kb % (kb_per_half // 2))
        piece = jnp.concatenate(
            [jnp.concatenate([scr[j0 + d, scr_row0[c]:scr_row0[c] + CHUNK, :]
                              for c in range(nchunks)], axis=0)
             for d in range(READ_K // LANES)], axis=1).astype(BF16)
        part = _dot(piece, wc_ref[hf, kb * READ_K:(kb + 1) * READ_K, :])
        ys[hf] = part if ys[hf] is None else ys[hf] + part

    def readout_finish():
        y = jnp.concatenate(ys, axis=1) + dskip_ref[...] * proj_scr[:, 0:S5_WIDTH]
        gact = _gelu_tanh(y)
        out5 = gact * _sigmoid(_dot(gact.astype(BF16), wglu_ref[...]))
        mix_scr[:, 0:S5_WIDTH] = out5.astype(BF16)

    n_pieces = 2 * kb_per_half
    if not state_t:
        for p in range(n_pieces):
            readout_piece(p)
        readout_finish()

    lb = lb_ref[...]
    seg_mask = mask_ref[...].astype(F32) > 0.5
    zeros_t = jnp.zeros((tt, HG_DIM), F32)
    dec_rows = jnp.concatenate(
        [jnp.zeros((3 * tt, HG_DIM), F32), jnp.ones((3 * tt, HG_DIM), F32)], axis=1)
    heads = range(HG_HEADS)

    def chunk_stages(c):
        r0 = pl.multiple_of(c * CHUNK, CHUNK)
        rows = pl.ds(r0, CHUNK)
        q_all = proj_scr[rows, S5_WIDTH:S5_WIDTH + HG_WIDTH]
        fz = proj_scr[rows, S5_WIDTH + HG_WIDTH:S5_WIDTH + 2 * HG_WIDTH]
        iv_all = proj_scr[rows, S5_WIDTH + 2 * HG_WIDTH:S5_WIDTH + 3 * HG_WIDTH]
        og_all = proj_scr[rows, S5_WIDTH + 3 * HG_WIDTH:S5_WIDTH + 4 * HG_WIDTH]
        f = lb + (1.0 - lb) * _sigmoid(fz)
        lf_all = jnp.log(f)
        k_all = 1.0 - f
        lf_pieces = _split3(lf_all)
        g_all = sum(_dot(mask_ref[...], p) for p in lf_pieces)
        vb_all = iv_all.astype(BF16)
        qg_all = (q_all * jnp.exp(g_all)).astype(BF16)
        hs = [slice(h * HG_DIM, (h + 1) * HG_DIM) for h in heads]

        qs, ks = [], []
        for h in heads:
            gh, qh, kh = g_all[:, hs[h]], q_all[:, hs[h]], k_all[:, hs[h]]
            if tt == CHUNK:
                nblk = CHUNK // SUB
                blocks = [slice(b * SUB, (b + 1) * SUB) for b in range(nblk)]
                refs = [gh[b * SUB + SUB // 2:b * SUB + SUB // 2 + 1, :] for b in range(nblk)]
                ref_rows = jnp.concatenate(
                    [jnp.broadcast_to(r, (SUB, HG_DIM)) for r in refs], axis=0)
                q_loc = (qh * jnp.exp(gh - ref_rows)).astype(BF16)
                k_loc = kh * jnp.exp(ref_rows - gh)
                no_keys = jnp.zeros((SUB, HG_DIM), BF16)
                for i in range(nblk):
                    qs.append(q_loc[blocks[i]])
                    parts = [(k_loc[blocks[j]] * jnp.exp(refs[i] - refs[j])).astype(BF16)
                             for j in range(i)]
                    parts.append(k_loc[blocks[i]].astype(BF16))
                    parts += [no_keys] * (nblk - 1 - i)
                    ks.append(jnp.concatenate(parts, axis=0))
            else:
                g3 = gh.reshape(bpc, tt, HG_DIM)
                ref3 = jnp.broadcast_to(g3[:, tt // 2:tt // 2 + 1, :], (bpc, tt, HG_DIM))
                ref_rows = ref3.reshape(CHUNK, HG_DIM)
                qs.append((qh * jnp.exp(gh - ref_rows)).astype(BF16))
                ks.append((kh * jnp.exp(ref_rows - gh)).astype(BF16))

        yield
        att = [_dot_nt(a, b) for a, b in zip(qs, ks)]

        yield
        o_inter = []
        for h in heads:
            gh, kh = g_all[:, hs[h]], k_all[:, hs[h]]
            oi = []
            for bb in range(bpc):
                rs = slice(bb * tt, (bb + 1) * tt)
                b_idx = c * bpc + bb
                s_old = s0_ref[b_idx, h] if has_state else s_ref[b_idx, h]
                g_last = gh[bb * tt + tt - 1:bb * tt + tt, :]
                kd = kh[rs] * jnp.exp(g_last - gh[rs])
                if state_t:
                    oi.append(_dot_nt(qg_all[rs, hs[h]], s_old.astype(BF16)))
                    s_ref[b_idx, h] = (jnp.exp(g_last) * s_old
                                       + _dot_tn(vb_all[rs, hs[h]], kd.astype(BF16)))
                else:
                    oi.append(_dot(qg_all[rs, hs[h]], s_old.astype(BF16)))
                    lhs = jnp.concatenate(
                        [kd] + [p[rs, hs[h]].astype(F32) for p in lf_pieces], axis=0)
                    rhs = jnp.concatenate(
                        [jnp.concatenate([iv_all[rs, hs[h]], zeros_t], axis=1), dec_rows], axis=0)
                    upd = _dot_tn(lhs.astype(BF16), rhs.astype(BF16))
                    s_ref[b_idx, h] = jnp.exp(upd[:, HG_DIM:]) * s_old + upd[:, :HG_DIM]
            o_inter.append(oi[0] if bpc == 1 else jnp.concatenate(oi, axis=0))

        yield
        nblk = len(att) // HG_HEADS
        for h in heads:
            sc = att[h * nblk:(h + 1) * nblk]
            sc = sc[0] if nblk == 1 else jnp.concatenate(sc, axis=0)
            sc = jnp.where(seg_mask, sc, 0.0).astype(BF16)
            o = _dot(sc, vb_all[:, hs[h]]) + o_inter[h]
            o = _rms(o, gnorm_ref[...])
            ogh = og_all[:, hs[h]]
            mix_scr[rows, S5_WIDTH + h * HG_DIM:S5_WIDTH + (h + 1) * HG_DIM] = (
                o * (ogh * _sigmoid(ogh))).astype(BF16)

    def phase3(c, carry):
        for _ in chunk_stages(c):
            pass
        return carry

    if state_t:
        per_chunk = n_pieces // nchunks
        assert per_chunk == 2
        gens = [chunk_stages(c) for c in range(nchunks)]
        next(gens[0])
        for c in range(nchunks):
            next(gens[c])
            readout_piece(c * per_chunk)
            if c + 1 < nchunks:
                next(gens[c + 1])
            next(gens[c])
            readout_piece(c * per_chunk + 1)
            for _ in gens[c]:
                pass
        readout_finish()
    else:
        lax.fori_loop(0, nchunks, phase3, 0)

    if state_t:
        @pl.when(tstep == pl.num_programs(1) - 1)
        def _untranspose_state():
            for b in range(nb):
                for h in heads:
                    s_ref[b, h] = s_ref[b, h].T

    x1 = x_ref[...].reshape(rows_tile, D_MODEL) + _dot(mix_scr[...], wout_ref[...])
    x1_ref[...] = x1.reshape(nb, tt, D_MODEL)


def _const_spec(shape):
    nd = len(shape)
    return pl.BlockSpec(shape, lambda i, j, _nd=nd: (0,) * _nd, pipeline_mode=pl.Buffered(1))


def _mixer(x, state, w, *, nb, tt):
    B, T, _ = x.shape
    has_state = state is not None
    grid = (B // nb, T // tt)
    bpc = CHUNK // tt
    nchunks = nb // bpc
    pitch = _seq_pitch(tt)
    scr_rows = nchunks * pitch * bpc

    consts = [w["norm_mix"], w["w_in"], w["wb"], w["a_re"], w["a_im"], w["wc"], w["d_skip"],
              w["w_glu"], w["lb"], w["g_norm"], w["w_out"], w["mask_" + str(tt)]]
    in_specs = [pl.BlockSpec((nb, tt, D_MODEL), lambda i, j: (i, j, 0))]
    args = [x]
    if has_state:
        in_specs += [pl.BlockSpec((nb, S5_FLAT), lambda i, j: (i, 0)),
                     pl.BlockSpec((nb, S5_FLAT), lambda i, j: (i, 0)),
                     pl.BlockSpec((nb, HG_HEADS, HG_DIM, HG_DIM), lambda i, j: (i, 0, 0, 0))]
        args += list(state)
    in_specs += [_const_spec(c.shape) for c in consts]
    args += consts

    out_shape = (jax.ShapeDtypeStruct((B, T, D_MODEL), F32),
                 jax.ShapeDtypeStruct((B, S5_FLAT), F32),
                 jax.ShapeDtypeStruct((B, S5_FLAT), F32),
                 jax.ShapeDtypeStruct((B, HG_HEADS, HG_DIM, HG_DIM), F32))
    out_specs = (pl.BlockSpec((nb, tt, D_MODEL), lambda i, j: (i, j, 0)),
                 pl.BlockSpec((nb, S5_FLAT), lambda i, j: (i, 0)),
                 pl.BlockSpec((nb, S5_FLAT), lambda i, j: (i, 0)),
                 pl.BlockSpec((nb, HG_HEADS, HG_DIM, HG_DIM), lambda i, j: (i, 0, 0, 0)))
    scratch = [pltpu.VMEM((nb * tt, IN_COLS), F32),
               pltpu.VMEM((N_SLABS, scr_rows, LANES), F32),
               pltpu.VMEM((N_SLABS, scr_rows, LANES), F32),
               pltpu.VMEM((nb * tt, D_MODEL), BF16)]
    return pl.pallas_call(
        functools.partial(_mixer_kernel, nb=nb, tt=tt, has_state=has_state),
        grid=grid, in_specs=in_specs, out_specs=out_specs, out_shape=out_shape,
        scratch_shapes=scratch,
        compiler_params=pltpu.CompilerParams(
            dimension_semantics=("arbitrary", "arbitrary"), vmem_limit_bytes=VMEM_LIMIT),
        name="mixer_t%d" % tt,
    )(*args)


FF_CHUNK = 256


def _ffn_kernel(xa_ref, xb_ref, nffn_ref, wg_ref, wu_ref, wd_ref, nfin_ref, ya_ref, yb_ref, *,
                steps_a):
    def tile(x_ref, y_ref):
        x = x_ref[...]
        h2 = _rms(x, nffn_ref[...]).astype(BF16)
        d_ff = wg_ref.shape[1]
        acc = x
        for f0 in range(0, d_ff, FF_CHUNK):
            gate = _dot(h2, wg_ref[:, f0:f0 + FF_CHUNK].astype(BF16))
            up = _dot(h2, wu_ref[:, f0:f0 + FF_CHUNK].astype(BF16))
            act = (gate * _sigmoid(gate) * up).astype(BF16)
            acc = acc + _dot(act, wd_ref[f0:f0 + FF_CHUNK, :].astype(BF16))
        y_ref[...] = _rms(acc, nfin_ref[...])

    step = pl.program_id(0)

    @pl.when(step < steps_a)
    def _first():
        tile(xa_ref, ya_ref)

    @pl.when(step >= steps_a)
    def _second():
        tile(xb_ref, yb_ref)


def _ffn(xa, xb, w, *, tm):
    steps_a, steps_b = xa.shape[0] // tm, xb.shape[0] // tm
    d_ff = w["w_gate"].shape[1]

    def cspec(shape):
        return pl.BlockSpec(shape, lambda i: (0, 0), pipeline_mode=pl.Buffered(1))

    spec_a = pl.BlockSpec((tm, D_MODEL), lambda i: (jnp.minimum(i, steps_a - 1), 0))
    spec_b = pl.BlockSpec((tm, D_MODEL), lambda i: (jnp.maximum(i - steps_a, 0), 0))
    return pl.pallas_call(
        functools.partial(_ffn_kernel, steps_a=steps_a),
        grid=(steps_a + steps_b,),
        in_specs=[spec_a, spec_b,
                  cspec((1, D_MODEL)), cspec((D_MODEL, d_ff)), cspec((D_MODEL, d_ff)),
                  cspec((d_ff, D_MODEL)), cspec((1, D_MODEL))],
        out_specs=(spec_a, spec_b),
        out_shape=(jax.ShapeDtypeStruct(xa.shape, F32), jax.ShapeDtypeStruct(xb.shape, F32)),
        compiler_params=pltpu.CompilerParams(
            dimension_semantics=("arbitrary",), vmem_limit_bytes=VMEM_LIMIT),
        name="ffn",
    )(xa, xb, w["norm_ffn"], w["w_gate"], w["w_up"], w["w_down"], w["norm_final"])


def _block_diag_halves(blocks):
    g, r, c = blocks.shape
    half = g // 2
    eye = jnp.eye(half, dtype=blocks.dtype)
    b = blocks.reshape(2, half, r, c)
    return jnp.einsum("hgrc,gk->hgrkc", b, eye).reshape(2, half * r, half * c)


def _segment_mask(tt):
    r = jnp.arange(CHUNK)
    same = (r[:, None] // tt) == (r[None, :] // tt)
    return (same & (r[:, None] >= r[None, :])).astype(F32)


def _prepare(lb_param, norm_mix, w_in, s5_a_re, s5_a_im, s5_log_dt, s5_b_re, s5_b_im, s5_c_re,
             s5_c_im, s5_d, s5_w_glu, hg_norm, w_out, norm_ffn, w_gate, w_up, w_down, norm_final):
    l = 0
    a_re = s5_a_re[l].astype(F32)
    a_im = s5_a_im[l].astype(F32)
    dt = jnp.exp(s5_log_dt[l].astype(F32))[:, None]
    mag = jnp.exp(a_re * dt)
    ab_re = mag * jnp.cos(a_im * dt)
    ab_im = mag * jnp.sin(a_im * dt)
    den = a_re * a_re + a_im * a_im
    nr = ab_re - 1.0
    ni = ab_im
    f_re = (nr * a_re + ni * a_im) / den
    f_im = (ni * a_re - nr * a_im) / den
    b_re = s5_b_re[l].astype(F32)
    b_im = s5_b_im[l].astype(F32)
    bb_re = f_re[..., None] * b_re - f_im[..., None] * b_im
    bb_im = f_re[..., None] * b_im + f_im[..., None] * b_re
    wb = jnp.concatenate([_block_diag_halves(bb_re.transpose(0, 2, 1)),
                          _block_diag_halves(bb_im.transpose(0, 2, 1))], axis=2).astype(BF16)
    wc = jnp.concatenate([_block_diag_halves(s5_c_re[l].astype(F32).transpose(0, 2, 1)),
                          _block_diag_halves(-s5_c_im[l].astype(F32).transpose(0, 2, 1))],
                         axis=1).astype(BF16)
    lb_all = jnp.cumsum(jax.nn.softmax(lb_param.astype(F32), axis=0), axis=0)
    return {
        "norm_mix": norm_mix[l].reshape(1, D_MODEL).astype(F32),
        "w_in": w_in[l].astype(BF16),
        "wb": wb,
        "a_re": ab_re.reshape(1, S5_FLAT),
        "a_im": ab_im.reshape(1, S5_FLAT),
        "wc": wc,
        "d_skip": s5_d[l].reshape(1, S5_WIDTH).astype(F32),
        "w_glu": s5_w_glu[l].astype(BF16),
        "lb": lb_all[l].reshape(1, HG_WIDTH),
        "g_norm": hg_norm[l].reshape(1, HG_DIM).astype(F32),
        "w_out": w_out[l].astype(BF16),
        "mask_64": _segment_mask(64).astype(BF16),
        "mask_8": _segment_mask(8).astype(BF16),
        "norm_ffn": norm_ffn[l].reshape(1, D_MODEL).astype(F32),
        "w_gate": w_gate[l].astype(F32),
        "w_up": w_up[l].astype(F32),
        "w_down": w_down[l].astype(F32),
        "norm_final": norm_final.reshape(1, D_MODEL).astype(F32),
    }


def _states(h_re, h_im, s_new):
    b = h_re.shape[0]
    return (h_re.reshape(1, b, S5_GROUPS, S5_STATE), h_im.reshape(1, b, S5_GROUPS, S5_STATE),
            s_new.reshape(1, b, HG_HEADS, HG_DIM, HG_DIM))


def kernel(x_prompt, x_sample, state_s5_re, state_s5_im, state_hgrn, lb_param, norm_mix, w_in,
           s5_a_re, s5_a_im, s5_log_dt, s5_b_re, s5_b_im, s5_c_re, s5_c_im, s5_d, s5_w_glu,
           hg_norm, w_out, norm_ffn, w_gate, w_up, w_down, norm_final):
    assert norm_mix.shape[0] == 1, "single-layer trunk"
    w = _prepare(lb_param, norm_mix, w_in, s5_a_re, s5_a_im, s5_log_dt, s5_b_re, s5_b_im,
                 s5_c_re, s5_c_im, s5_d, s5_w_glu, hg_norm, w_out, norm_ffn, w_gate, w_up,
                 w_down, norm_final)
    bp, tp, _ = x_prompt.shape
    bs, ts, _ = x_sample.shape
    x1_p, *st_p = _mixer(x_prompt, None, w, nb=8, tt=64)
    st = (state_s5_re[0].reshape(bs, S5_FLAT), state_s5_im[0].reshape(bs, S5_FLAT), state_hgrn[0])
    x1_s, *st_s = _mixer(x_sample, st, w, nb=16, tt=8)
    y_p, y_s = _ffn(x1_p.reshape(bp * tp, D_MODEL), x1_s.reshape(bs * ts, D_MODEL), w, tm=512)
    return (y_p.reshape(bp, tp, D_MODEL), y_s.reshape(bs, ts, D_MODEL),
            *_states(*st_p), *_states(*st_s))
```

```python
import functools
import math

import jax
import jax.numpy as jnp
import numpy as np
from jax import lax
from jax.experimental import pallas as pl
from jax.experimental.pallas import tpu as pltpu

F32 = jnp.float32
BF16 = jnp.bfloat16

D_MODEL = 1024
S5_WIDTH = 512
S5_GROUP = 16
S5_GROUPS = 32
S5_STATE = 64
S5_FLAT = S5_GROUPS * S5_STATE
HG_WIDTH = 512
HG_DIM = 128
HG_HEADS = 4
IN_COLS = S5_WIDTH + 4 * HG_WIDTH
EPS = 1e-6

LANES = 128
SUBLANES = 8
CHUNK = 64
N_SLABS = S5_FLAT // LANES
HALF_U = S5_WIDTH // 2
HALF_SLABS = N_SLABS // 2
SUB = 16
READ_K = 256
SCAN_COLS = 256
SCAN_PAD = 4
VMEM_LIMIT = 56 * 1024 * 1024


def _seq_pitch(tt):
    return tt + SCAN_PAD if tt == CHUNK else tt


def _dot(a, b):
    return jnp.dot(a, b, preferred_element_type=F32)


def _dot_nt(a, b):
    return lax.dot_general(a, b, (((1,), (1,)), ((), ())), preferred_element_type=F32)


def _dot_tn(a, b):
    return lax.dot_general(a, b, (((0,), (0,)), ((), ())), preferred_element_type=F32)


def _split3(x):
    hi = x.astype(BF16)
    r1 = x - hi.astype(F32)
    mid = r1.astype(BF16)
    lo = (r1 - mid.astype(F32)).astype(BF16)
    return hi, mid, lo


def _rms(x, g):
    return x * lax.rsqrt(jnp.mean(x * x, axis=-1, keepdims=True) + EPS) * g


def _sigmoid(x):
    return jax.nn.sigmoid(x)


def _gelu_tanh(x):
    c = math.sqrt(2.0 / math.pi)
    return 0.5 * x * (1.0 + jnp.tanh(c * (x + 0.044715 * (x * x * x))))


def _mixer_kernel(*refs, nb, tt, has_state):
    it = iter(refs)
    x_ref = next(it)
    if has_state:
        h0re_ref, h0im_ref, s0_ref = next(it), next(it), next(it)
    (norm_ref, win_ref, wb_ref, are_ref, aim_ref, wc_ref, dskip_ref, wglu_ref, lb_ref,
     gnorm_ref, wout_ref, mask_ref) = (next(it) for _ in range(12))
    x1_ref, hre_ref, him_ref, s_ref = (next(it) for _ in range(4))
    proj_scr, bre_scr, bim_scr, mix_scr = (next(it) for _ in range(4))

    bpc = CHUNK // tt
    nchunks = nb // bpc
    pitch = _seq_pitch(tt)
    cpitch = pitch * bpc
    ngroups = nb // SUBLANES
    state_t = tt == CHUNK
    assert not (state_t and has_state)

    tstep = pl.program_id(1)

    @pl.when(tstep == 0)
    def _init():
        if has_state:
            hre_ref[...] = h0re_ref[...]
            him_ref[...] = h0im_ref[...]
        else:
            hre_ref[...] = jnp.zeros_like(hre_ref)
            him_ref[...] = jnp.zeros_like(him_ref)
            s_ref[...] = jnp.zeros_like(s_ref)

    rows_tile = nb * tt
    scr_row0 = [c * cpitch for c in range(nchunks)]

    hn = _rms(x_ref[...].reshape(rows_tile, D_MODEL), norm_ref[...]).astype(BF16)
    proj_scr[:, 0:S5_WIDTH] = _dot(hn, win_ref[:, 0:S5_WIDTH])
    ub = proj_scr[:, 0:S5_WIDTH].astype(BF16)
    for hf in range(2):
        ubh = ub[:, hf * HALF_U:(hf + 1) * HALF_U]
        for part, scr in enumerate((bre_scr, bim_scr)):
            cols = slice(part * HALF_SLABS * LANES, (part + 1) * HALF_SLABS * LANES)
            bu = _dot(ubh, wb_ref[hf, :, cols])
            for jj in range(HALF_SLABS):
                for c in range(nchunks):
                    scr[hf * HALF_SLABS + jj, scr_row0[c]:scr_row0[c] + CHUNK, :] = (
                        bu[c * CHUNK:(c + 1) * CHUNK, jj * LANES:(jj + 1) * LANES])

    n_rest = (IN_COLS - S5_WIDTH) // SCAN_COLS
    steps_per_iter = tt // n_rest
    for i in range(n_rest):
        n0 = S5_WIDTH + i * SCAN_COLS
        proj_scr[:, n0:n0 + SCAN_COLS] = _dot(hn, win_ref[:, n0:n0 + SCAN_COLS])
        t0 = i * steps_per_iter
        for g in range(ngroups):
            base = g * SUBLANES * pitch
            grows = slice(g * SUBLANES, (g + 1) * SUBLANES)
            for j in range(N_SLABS):
                lanes = slice(j * LANES, (j + 1) * LANES)
                a_re = jnp.broadcast_to(are_ref[:, lanes], (SUBLANES, LANES))
                a_im = jnp.broadcast_to(aim_ref[:, lanes], (SUBLANES, LANES))
                h_re = hre_ref[grows, lanes]
                h_im = him_ref[grows, lanes]
                for dt in range(steps_per_iter):
                    idx = pl.ds(base + t0 + dt, SUBLANES, stride=pitch)
                    n_re = a_re * h_re - a_im * h_im + bre_scr[j, idx, :]
                    n_im = a_re * h_im + a_im * h_re + bim_scr[j, idx, :]
                    bre_scr[j, idx, :] = n_re
                    bim_scr[j, idx, :] = n_im
                    h_re, h_im = n_re, n_im
                hre_ref[grows, lanes] = h_re
                him_ref[grows, lanes] = h_im

    ys = [None, None]
    kb_per_half = 2 * HALF_SLABS * LANES // READ_K

    def readout_piece(p):
        hf, kb = divmod(p, kb_per_half)
        scr = bre_scr if kb < kb_per_half // 2 else bim_scr
        j0 = hf * HALF_SLABS + (READ_K // LANES) * (kb % (kb_per_half // 2))
        piece = jnp.concatenate(
            [jnp.concatenate([scr[j0 + d, scr_row0[c]:scr_row0[c] + CHUNK, :]
                              for c in range(nchunks)], axis=0)
             for d in range(READ_K // LANES)], axis=1).astype(BF16)
        part = _dot(piece, wc_ref[hf, kb * READ_K:(kb + 1) * READ_K, :])
        ys[hf] = part if ys[hf] is None else ys[hf] + part

    def readout_finish():
        y = jnp.concatenate(ys, axis=1) + dskip_ref[...] * proj_scr[:, 0:S5_WIDTH]
        gact = _gelu_tanh(y)
        out5 = gact * _sigmoid(_dot(gact.astype(BF16), wglu_ref[...]))
        mix_scr[:, 0:S5_WIDTH] = out5.astype(BF16)

    n_pieces = 2 * kb_per_half

    lb = lb_ref[...]
    seg_mask = mask_ref[...].astype(F32) > 0.5
    zeros_t = jnp.zeros((tt, HG_DIM), F32)
    dec_rows = jnp.concatenate(
        [jnp.zeros((3 * tt, HG_DIM), F32), jnp.ones((3 * tt, HG_DIM), F32)], axis=1)
    heads = range(HG_HEADS)

    def chunk_stages(c):
        rows = slice(c * CHUNK, (c + 1) * CHUNK)
        q_all = proj_scr[rows, S5_WIDTH:S5_WIDTH + HG_WIDTH]
        fz = proj_scr[rows, S5_WIDTH + HG_WIDTH:S5_WIDTH + 2 * HG_WIDTH]
        iv_all = proj_scr[rows, S5_WIDTH + 2 * HG_WIDTH:S5_WIDTH + 3 * HG_WIDTH]
        og_all = proj_scr[rows, S5_WIDTH + 3 * HG_WIDTH:S5_WIDTH + 4 * HG_WIDTH]
        f = lb + (1.0 - lb) * _sigmoid(fz)
        lf_all = jnp.log(f)
        k_all = 1.0 - f
        lf_pieces = _split3(lf_all)
        g_all = sum(_dot(mask_ref[...], p) for p in lf_pieces)
        vb_all = iv_all.astype(BF16)
        qg_all = (q_all * jnp.exp(g_all)).astype(BF16)
        hs = [slice(h * HG_DIM, (h + 1) * HG_DIM) for h in heads]

        qs, ks = [], []
        for h in heads:
            gh, qh, kh = g_all[:, hs[h]], q_all[:, hs[h]], k_all[:, hs[h]]
            if tt == CHUNK:
                nblk = CHUNK // SUB
                blocks = [slice(b * SUB, (b + 1) * SUB) for b in range(nblk)]
                refs = [gh[b * SUB + SUB // 2:b * SUB + SUB // 2 + 1, :] for b in range(nblk)]
                ref_rows = jnp.concatenate(
                    [jnp.broadcast_to(r, (SUB, HG_DIM)) for r in refs], axis=0)
                q_loc = (qh * jnp.exp(gh - ref_rows)).astype(BF16)
                k_loc = kh * jnp.exp(ref_rows - gh)
                no_keys = jnp.zeros((SUB, HG_DIM), BF16)
                for i in range(nblk):
                    qs.append(q_loc[blocks[i]])
                    parts = [(k_loc[blocks[j]] * jnp.exp(refs[i] - refs[j])).astype(BF16)
                             for j in range(i)]
                    parts.append(k_loc[blocks[i]].astype(BF16))
                    parts += [no_keys] * (nblk - 1 - i)
                    ks.append(jnp.concatenate(parts, axis=0))
            else:
                g3 = gh.reshape(bpc, tt, HG_DIM)
                ref3 = jnp.broadcast_to(g3[:, tt // 2:tt // 2 + 1, :], (bpc, tt, HG_DIM))
                ref_rows = ref3.reshape(CHUNK, HG_DIM)
                qs.append((qh * jnp.exp(gh - ref_rows)).astype(BF16))
                ks.append((kh * jnp.exp(ref_rows - gh)).astype(BF16))

        yield
        att = [_dot_nt(a, b) for a, b in zip(qs, ks)]

        yield
        o_inter = []
        for h in heads:
            gh, kh = g_all[:, hs[h]], k_all[:, hs[h]]
            oi = []
            for bb in range(bpc):
                rs = slice(bb * tt, (bb + 1) * tt)
                b_idx = c * bpc + bb
                s_old = s0_ref[b_idx, h] if has_state else s_ref[b_idx, h]
                g_last = gh[bb * tt + tt - 1:bb * tt + tt, :]
                kd = kh[rs] * jnp.exp(g_last - gh[rs])
                if state_t:
                    oi.append(_dot_nt(qg_all[rs, hs[h]], s_old.astype(BF16)))
                    s_ref[b_idx, h] = (jnp.exp(g_last) * s_old
                                       + _dot_tn(vb_all[rs, hs[h]], kd.astype(BF16)))
                else:
                    oi.append(_dot(qg_all[rs, hs[h]], s_old.astype(BF16)))
                    lhs = jnp.concatenate(
                        [kd] + [p[rs, hs[h]].astype(F32) for p in lf_pieces], axis=0)
                    rhs = jnp.concatenate(
                        [jnp.concatenate([iv_all[rs, hs[h]], zeros_t], axis=1), dec_rows], axis=0)
                    upd = _dot_tn(lhs.astype(BF16), rhs.astype(BF16))
                    s_ref[b_idx, h] = jnp.exp(upd[:, HG_DIM:]) * s_old + upd[:, :HG_DIM]
            o_inter.append(oi[0] if bpc == 1 else jnp.concatenate(oi, axis=0))

        yield
        nblk = len(att) // HG_HEADS
        for h in heads:
            sc = att[h * nblk:(h + 1) * nblk]
            sc = sc[0] if nblk == 1 else jnp.concatenate(sc, axis=0)
            sc = jnp.where(seg_mask, sc, 0.0).astype(BF16)
            o = _dot(sc, vb_all[:, hs[h]]) + o_inter[h]
            o = _rms(o, gnorm_ref[...])
            ogh = og_all[:, hs[h]]
            mix_scr[rows, S5_WIDTH + h * HG_DIM:S5_WIDTH + (h + 1) * HG_DIM] = (
                o * (ogh * _sigmoid(ogh))).astype(BF16)

    per_chunk = n_pieces // nchunks
    gens = [chunk_stages(c) for c in range(nchunks)]
    next(gens[0])
    for c in range(nchunks):
        pieces = range(c * per_chunk, (c + 1) * per_chunk)
        next(gens[c])
        for p in pieces[:per_chunk // 2]:
            readout_piece(p)
        if c + 1 < nchunks:
            next(gens[c + 1])
        next(gens[c])
        for p in pieces[per_chunk // 2:]:
            readout_piece(p)
        for _ in gens[c]:
            pass
    readout_finish()

    if state_t:
        @pl.when(tstep == pl.num_programs(1) - 1)
        def _untranspose_state():
            for b in range(nb):
                for h in heads:
                    s_ref[b, h] = s_ref[b, h].T

    x1 = x_ref[...].reshape(rows_tile, D_MODEL) + _dot(mix_scr[...], wout_ref[...])
    x1_ref[...] = x1.reshape(nb, tt, D_MODEL)


def _const_spec(shape):
    nd = len(shape)
    return pl.BlockSpec(shape, lambda i, j, _nd=nd: (0,) * _nd, pipeline_mode=pl.Buffered(1))


def _mixer(x, state, w, *, nb, tt):
    B, T, _ = x.shape
    has_state = state is not None
    grid = (B // nb, T // tt)
    bpc = CHUNK // tt
    nchunks = nb // bpc
    pitch = _seq_pitch(tt)
    scr_rows = nchunks * pitch * bpc

    consts = [w["norm_mix"], w["w_in"], w["wb"], w["a_re"], w["a_im"], w["wc"], w["d_skip"],
              w["w_glu"], w["lb"], w["g_norm"], w["w_out"], w["mask_" + str(tt)]]
    in_specs = [pl.BlockSpec((nb, tt, D_MODEL), lambda i, j: (i, j, 0))]
    args = [x]
    if has_state:
        in_specs += [pl.BlockSpec((nb, S5_FLAT), lambda i, j: (i, 0)),
                     pl.BlockSpec((nb, S5_FLAT), lambda i, j: (i, 0)),
                     pl.BlockSpec((None, nb, HG_HEADS, HG_DIM, HG_DIM),
                                  lambda i, j: (0, i, 0, 0, 0))]
        args += list(state)
    in_specs += [_const_spec(c.shape) for c in consts]
    args += consts

    out_shape = (jax.ShapeDtypeStruct((B, T, D_MODEL), F32),
                 jax.ShapeDtypeStruct((B, S5_FLAT), F32),
                 jax.ShapeDtypeStruct((B, S5_FLAT), F32),
                 jax.ShapeDtypeStruct((B, HG_HEADS, HG_DIM, HG_DIM), F32))
    out_specs = (pl.BlockSpec((nb, tt, D_MODEL), lambda i, j: (i, j, 0)),
                 pl.BlockSpec((nb, S5_FLAT), lambda i, j: (i, 0)),
                 pl.BlockSpec((nb, S5_FLAT), lambda i, j: (i, 0)),
                 pl.BlockSpec((nb, HG_HEADS, HG_DIM, HG_DIM), lambda i, j: (i, 0, 0, 0)))
    scratch = [pltpu.VMEM((nb * tt, IN_COLS), F32),
               pltpu.VMEM((N_SLABS, scr_rows, LANES), F32),
               pltpu.VMEM((N_SLABS, scr_rows, LANES), F32),
               pltpu.VMEM((nb * tt, D_MODEL), BF16)]
    return pl.pallas_call(
        functools.partial(_mixer_kernel, nb=nb, tt=tt, has_state=has_state),
        grid=grid, in_specs=in_specs, out_specs=out_specs, out_shape=out_shape,
        scratch_shapes=scratch,
        compiler_params=pltpu.CompilerParams(
            dimension_semantics=("arbitrary", "arbitrary"), vmem_limit_bytes=VMEM_LIMIT),
        name="mixer_t%d" % tt,
    )(*args)


FF_CHUNK = 256


def _ffn_kernel(xa_ref, xb_ref, nffn_ref, wg_ref, wu_ref, wd_ref, nfin_ref, ya_ref, yb_ref, *,
                steps_a):
    def tile(x_ref, y_ref):
        x = x_ref[...]
        h2 = _rms(x, nffn_ref[...]).astype(BF16)
        d_ff = wg_ref.shape[1]
        acc = x
        for f0 in range(0, d_ff, FF_CHUNK):
            gate = _dot(h2, wg_ref[:, f0:f0 + FF_CHUNK].astype(BF16))
            up = _dot(h2, wu_ref[:, f0:f0 + FF_CHUNK].astype(BF16))
            act = (gate * _sigmoid(gate) * up).astype(BF16)
            acc = acc + _dot(act, wd_ref[f0:f0 + FF_CHUNK, :].astype(BF16))
        y_ref[...] = _rms(acc, nfin_ref[...])

    step = pl.program_id(0)

    @pl.when(step < steps_a)
    def _first():
        tile(xa_ref, ya_ref)

    @pl.when(step >= steps_a)
    def _second():
        tile(xb_ref, yb_ref)


def _ffn(xa, xb, w, *, tm):
    steps_a, steps_b = xa.shape[0] // tm, xb.shape[0] // tm
    d_ff = w["w_gate"].shape[-1]

    def cspec(shape):
        return pl.BlockSpec(shape, lambda i: (0, 0), pipeline_mode=pl.Buffered(1))

    def layer_spec(shape):
        return pl.BlockSpec((None,) + shape, lambda i: (0, 0, 0), pipeline_mode=pl.Buffered(1))

    spec_a = pl.BlockSpec((tm, D_MODEL), lambda i: (jnp.minimum(i, steps_a - 1), 0))
    spec_b = pl.BlockSpec((tm, D_MODEL), lambda i: (jnp.maximum(i - steps_a, 0), 0))
    return pl.pallas_call(
        functools.partial(_ffn_kernel, steps_a=steps_a),
        grid=(steps_a + steps_b,),
        in_specs=[spec_a, spec_b,
                  cspec((1, D_MODEL)), layer_spec((D_MODEL, d_ff)), layer_spec((D_MODEL, d_ff)),
                  layer_spec((d_ff, D_MODEL)), cspec((1, D_MODEL))],
        out_specs=(spec_a, spec_b),
        out_shape=(jax.ShapeDtypeStruct(xa.shape, F32), jax.ShapeDtypeStruct(xb.shape, F32)),
        compiler_params=pltpu.CompilerParams(
            dimension_semantics=("arbitrary",), vmem_limit_bytes=VMEM_LIMIT),
        name="ffn",
    )(xa, xb, w["norm_ffn"], w["w_gate"], w["w_up"], w["w_down"], w["norm_final"])


def _block_diag_halves(blocks):
    g, r, c = blocks.shape
    half = g // 2
    eye = jnp.eye(half, dtype=blocks.dtype)
    b = blocks.reshape(2, half, r, c)
    return jnp.einsum("hgrc,gk->hgrkc", b, eye).reshape(2, half * r, half * c)


def _segment_mask(tt):
    r = np.arange(CHUNK)
    same = (r[:, None] // tt) == (r[None, :] // tt)
    return jnp.asarray(same & (r[:, None] >= r[None, :]), dtype=BF16)


def _prepare(lb_param, norm_mix, w_in, s5_a_re, s5_a_im, s5_log_dt, s5_b_re, s5_b_im, s5_c_re,
             s5_c_im, s5_d, s5_w_glu, hg_norm, w_out, norm_ffn, w_gate, w_up, w_down, norm_final):
    l = 0
    a_re = s5_a_re[l].astype(F32)
    a_im = s5_a_im[l].astype(F32)
    dt = jnp.exp(s5_log_dt[l].astype(F32))[:, None]
    mag = jnp.exp(a_re * dt)
    ab_re = mag * jnp.cos(a_im * dt)
    ab_im = mag * jnp.sin(a_im * dt)
    den = a_re * a_re + a_im * a_im
    nr = ab_re - 1.0
    ni = ab_im
    f_re = (nr * a_re + ni * a_im) / den
    f_im = (ni * a_re - nr * a_im) / den
    b_re = s5_b_re[l].astype(F32)
    b_im = s5_b_im[l].astype(F32)
    bb_re = f_re[..., None] * b_re - f_im[..., None] * b_im
    bb_im = f_re[..., None] * b_im + f_im[..., None] * b_re
    wb = jnp.concatenate([_block_diag_halves(bb_re.transpose(0, 2, 1)),
                          _block_diag_halves(bb_im.transpose(0, 2, 1))], axis=2).astype(BF16)
    wc = jnp.concatenate([_block_diag_halves(s5_c_re[l].astype(F32).transpose(0, 2, 1)),
                          _block_diag_halves(-s5_c_im[l].astype(F32).transpose(0, 2, 1))],
                         axis=1).astype(BF16)
    lb_all = jnp.cumsum(jax.nn.softmax(lb_param.astype(F32), axis=0), axis=0)
    return {
        "norm_mix": norm_mix[l].reshape(1, D_MODEL).astype(F32),
        "w_in": w_in[l].astype(BF16),
        "wb": wb,
        "a_re": ab_re.reshape(1, S5_FLAT),
        "a_im": ab_im.reshape(1, S5_FLAT),
        "wc": wc,
        "d_skip": s5_d[l].reshape(1, S5_WIDTH).astype(F32),
        "w_glu": s5_w_glu[l].astype(BF16),
        "lb": lb_all[l].reshape(1, HG_WIDTH),
        "g_norm": hg_norm[l].reshape(1, HG_DIM).astype(F32),
        "w_out": w_out[l].astype(BF16),
        "mask_64": _segment_mask(64),
        "mask_8": _segment_mask(8),
        "norm_ffn": norm_ffn[l].reshape(1, D_MODEL).astype(F32),
        "w_gate": w_gate.astype(F32),
        "w_up": w_up.astype(F32),
        "w_down": w_down.astype(F32),
        "norm_final": norm_final.reshape(1, D_MODEL).astype(F32),
    }


def _states(h_re, h_im, s_new):
    b = h_re.shape[0]
    return (h_re.reshape(1, b, S5_GROUPS, S5_STATE), h_im.reshape(1, b, S5_GROUPS, S5_STATE),
            s_new.reshape(1, b, HG_HEADS, HG_DIM, HG_DIM))


def kernel(x_prompt, x_sample, state_s5_re, state_s5_im, state_hgrn, lb_param, norm_mix, w_in,
           s5_a_re, s5_a_im, s5_log_dt, s5_b_re, s5_b_im, s5_c_re, s5_c_im, s5_d, s5_w_glu,
           hg_norm, w_out, norm_ffn, w_gate, w_up, w_down, norm_final):
    assert norm_mix.shape[0] == 1, "single-layer trunk"
    w = _prepare(lb_param, norm_mix, w_in, s5_a_re, s5_a_im, s5_log_dt, s5_b_re, s5_b_im,
                 s5_c_re, s5_c_im, s5_d, s5_w_glu, hg_norm, w_out, norm_ffn, w_gate, w_up,
                 w_down, norm_final)
    bp, tp, _ = x_prompt.shape
    bs, ts, _ = x_sample.shape
    x1_p, *st_p = _mixer(x_prompt, None, w, nb=8, tt=64)
    st = (state_s5_re[0].reshape(bs, S5_FLAT), state_s5_im[0].reshape(bs, S5_FLAT), state_hgrn)
    x1_s, *st_s = _mixer(x_sample, st, w, nb=16, tt=8)
    y_p, y_s = _ffn(x1_p.reshape(bp * tp, D_MODEL), x1_s.reshape(bs * ts, D_MODEL), w, tm=512)
    return (y_p.reshape(bp, tp, D_MODEL), y_s.reshape(bs, ts, D_MODEL),
            *_states(*st_p), *_states(*st_s))
```

```python
import functools
import math

import jax
import jax.numpy as jnp
import numpy as np
from jax import lax
from jax.experimental import pallas as pl
from jax.experimental.pallas import tpu as pltpu

F32 = jnp.float32
BF16 = jnp.bfloat16

D_MODEL = 1024
S5_WIDTH = 512
S5_GROUP = 16
S5_GROUPS = 32
S5_STATE = 64
S5_FLAT = S5_GROUPS * S5_STATE
HG_WIDTH = 512
HG_DIM = 128
HG_HEADS = 4
IN_COLS = S5_WIDTH + 4 * HG_WIDTH
EPS = 1e-6

LANES = 128
SUBLANES = 8
CHUNK = 64
N_SLABS = S5_FLAT // LANES
HALF_U = S5_WIDTH // 2
HALF_SLABS = N_SLABS // 2
SUB = 16
READ_K = 256
SCAN_COLS = 256
SCAN_PAD = 4
VMEM_LIMIT = 56 * 1024 * 1024


def _seq_pitch(tt):
    return tt + SCAN_PAD if tt == CHUNK else tt


def _dot(a, b):
    return jnp.dot(a, b, preferred_element_type=F32)


def _dot_nt(a, b):
    return lax.dot_general(a, b, (((1,), (1,)), ((), ())), preferred_element_type=F32)


def _dot_tn(a, b):
    return lax.dot_general(a, b, (((0,), (0,)), ((), ())), preferred_element_type=F32)


def _split3(x):
    hi = x.astype(BF16)
    r1 = x - hi.astype(F32)
    mid = r1.astype(BF16)
    lo = (r1 - mid.astype(F32)).astype(BF16)
    return hi, mid, lo


def _rms(x, g):
    return x * lax.rsqrt(jnp.mean(x * x, axis=-1, keepdims=True) + EPS) * g


def _sigmoid(x):
    return jax.nn.sigmoid(x)


def _gelu_tanh(x):
    c = math.sqrt(2.0 / math.pi)
    return 0.5 * x * (1.0 + jnp.tanh(c * (x + 0.044715 * (x * x * x))))


def _mixer_kernel(*refs, nb, tt, has_state):
    it = iter(refs)
    x_ref = next(it)
    if has_state:
        h0re_ref, h0im_ref, s0_ref = next(it), next(it), next(it)
    (norm_ref, win_ref, wb_ref, are_ref, aim_ref, wc_ref, dskip_ref, wglu_ref, lb_ref,
     gnorm_ref, wout_ref, mask_ref) = (next(it) for _ in range(12))
    x1_ref, hre_ref, him_ref, s_ref = (next(it) for _ in range(4))
    proj_scr, bre_scr, bim_scr, mix_scr = (next(it) for _ in range(4))

    bpc = CHUNK // tt
    nchunks = nb // bpc
    pitch = _seq_pitch(tt)
    cpitch = pitch * bpc
    ngroups = nb // SUBLANES
    state_t = tt == CHUNK
    assert not (state_t and has_state)

    tstep = pl.program_id(1)

    @pl.when(tstep == 0)
    def _init():
        if has_state:
            hre_ref[...] = h0re_ref[...]
            him_ref[...] = h0im_ref[...]
        else:
            hre_ref[...] = jnp.zeros_like(hre_ref)
            him_ref[...] = jnp.zeros_like(him_ref)
            s_ref[...] = jnp.zeros_like(s_ref)

    rows_tile = nb * tt
    scr_row0 = [c * cpitch for c in range(nchunks)]

    hn = _rms(x_ref[...].reshape(rows_tile, D_MODEL), norm_ref[...]).astype(BF16)
    proj_scr[:, 0:S5_WIDTH] = _dot(hn, win_ref[:, 0:S5_WIDTH])
    ub = proj_scr[:, 0:S5_WIDTH].astype(BF16)
    for hf in range(2):
        ubh = ub[:, hf * HALF_U:(hf + 1) * HALF_U]
        for part, scr in enumerate((bre_scr, bim_scr)):
            cols = slice(part * HALF_SLABS * LANES, (part + 1) * HALF_SLABS * LANES)
            bu = _dot(ubh, wb_ref[hf, :, cols])
            for jj in range(HALF_SLABS):
                for c in range(nchunks):
                    scr[hf * HALF_SLABS + jj, scr_row0[c]:scr_row0[c] + CHUNK, :] = (
                        bu[c * CHUNK:(c + 1) * CHUNK, jj * LANES:(jj + 1) * LANES])

    n_rest = (IN_COLS - S5_WIDTH) // SCAN_COLS
    steps_per_iter = tt // n_rest
    for i in range(n_rest):
        n0 = S5_WIDTH + i * SCAN_COLS
        proj_scr[:, n0:n0 + SCAN_COLS] = _dot(hn, win_ref[:, n0:n0 + SCAN_COLS])
        t0 = i * steps_per_iter
        for g in range(ngroups):
            base = g * SUBLANES * pitch
            grows = slice(g * SUBLANES, (g + 1) * SUBLANES)
            for j in range(N_SLABS):
                lanes = slice(j * LANES, (j + 1) * LANES)
                a_re = jnp.broadcast_to(are_ref[:, lanes], (SUBLANES, LANES))
                a_im = jnp.broadcast_to(aim_ref[:, lanes], (SUBLANES, LANES))
                h_re = hre_ref[grows, lanes]
                h_im = him_ref[grows, lanes]
                for dt in range(steps_per_iter):
                    idx = pl.ds(base + t0 + dt, SUBLANES, stride=pitch)
                    n_re = a_re * h_re - a_im * h_im + bre_scr[j, idx, :]
                    n_im = a_re * h_im + a_im * h_re + bim_scr[j, idx, :]
                    bre_scr[j, idx, :] = n_re
                    bim_scr[j, idx, :] = n_im
                    h_re, h_im = n_re, n_im
                hre_ref[grows, lanes] = h_re
                him_ref[grows, lanes] = h_im

    ys = [None, None]
    kb_per_half = 2 * HALF_SLABS * LANES // READ_K

    def readout_piece(p):
        hf, kb = divmod(p, kb_per_half)
        scr = bre_scr if kb < kb_per_half // 2 else bim_scr
        j0 = hf * HALF_SLABS + (READ_K // LANES) * (kb % (kb_per_half // 2))
        piece = jnp.concatenate(
            [jnp.concatenate([scr[j0 + d, scr_row0[c]:scr_row0[c] + CHUNK, :]
                              for c in range(nchunks)], axis=0)
             for d in range(READ_K // LANES)], axis=1).astype(BF16)
        part = _dot(piece, wc_ref[hf, kb * READ_K:(kb + 1) * READ_K, :])
        ys[hf] = part if ys[hf] is None else ys[hf] + part

    def readout_finish():
        y = jnp.concatenate(ys, axis=1) + dskip_ref[...] * proj_scr[:, 0:S5_WIDTH]
        gact = _gelu_tanh(y)
        out5 = gact * _sigmoid(_dot(gact.astype(BF16), wglu_ref[...]))
        mix_scr[:, 0:S5_WIDTH] = out5.astype(BF16)

    n_pieces = 2 * kb_per_half

    lb = lb_ref[...]
    seg_mask = mask_ref[...].astype(F32) > 0.5
    zeros_t = jnp.zeros((tt, HG_DIM), F32)
    dec_rows = jnp.concatenate(
        [jnp.zeros((3 * tt, HG_DIM), F32), jnp.ones((3 * tt, HG_DIM), F32)], axis=1)
    heads = range(HG_HEADS)

    def chunk_stages(c):
        rows = slice(c * CHUNK, (c + 1) * CHUNK)
        q_all = proj_scr[rows, S5_WIDTH:S5_WIDTH + HG_WIDTH]
        fz = proj_scr[rows, S5_WIDTH + HG_WIDTH:S5_WIDTH + 2 * HG_WIDTH]
        iv_all = proj_scr[rows, S5_WIDTH + 2 * HG_WIDTH:S5_WIDTH + 3 * HG_WIDTH]
        og_all = proj_scr[rows, S5_WIDTH + 3 * HG_WIDTH:S5_WIDTH + 4 * HG_WIDTH]
        f = lb + (1.0 - lb) * _sigmoid(fz)
        lf_all = jnp.log(f)
        k_all = 1.0 - f
        lf_pieces = _split3(lf_all)
        g_all = sum(_dot(mask_ref[...], p) for p in lf_pieces)
        vb_all = iv_all.astype(BF16)
        qg_all = (q_all * jnp.exp(g_all)).astype(BF16)
        hs = [slice(h * HG_DIM, (h + 1) * HG_DIM) for h in heads]

        qs, ks = [], []
        for h in heads:
            gh, qh, kh = g_all[:, hs[h]], q_all[:, hs[h]], k_all[:, hs[h]]
            if tt == CHUNK:
                nblk = CHUNK // SUB
                blocks = [slice(b * SUB, (b + 1) * SUB) for b in range(nblk)]
                refs = [gh[b * SUB + SUB // 2:b * SUB + SUB // 2 + 1, :] for b in range(nblk)]
                ref_rows = jnp.concatenate(
                    [jnp.broadcast_to(r, (SUB, HG_DIM)) for r in refs], axis=0)
                q_loc = (qh * jnp.exp(gh - ref_rows)).astype(BF16)
                k_loc = kh * jnp.exp(ref_rows - gh)
                no_keys = jnp.zeros((SUB, HG_DIM), BF16)
                for i in range(nblk):
                    qs.append(q_loc[blocks[i]])
                    parts = [(k_loc[blocks[j]] * jnp.exp(refs[i] - refs[j])).astype(BF16)
                             for j in range(i)]
                    parts.append(k_loc[blocks[i]].astype(BF16))
                    parts += [no_keys] * (nblk - 1 - i)
                    ks.append(jnp.concatenate(parts, axis=0))
            else:
                g3 = gh.reshape(bpc, tt, HG_DIM)
                ref3 = jnp.broadcast_to(g3[:, tt // 2:tt // 2 + 1, :], (bpc, tt, HG_DIM))
                ref_rows = ref3.reshape(CHUNK, HG_DIM)
                qs.append((qh * jnp.exp(gh - ref_rows)).astype(BF16))
                ks.append((kh * jnp.exp(ref_rows - gh)).astype(BF16))

        yield
        att = [_dot_nt(a, b) for a, b in zip(qs, ks)]

        yield
        o_inter = []
        for h in heads:
            gh, kh = g_all[:, hs[h]], k_all[:, hs[h]]
            oi = []
            for bb in range(bpc):
                rs = slice(bb * tt, (bb + 1) * tt)
                b_idx = c * bpc + bb
                s_old = s0_ref[b_idx, h] if has_state else s_ref[b_idx, h]
                g_last = gh[bb * tt + tt - 1:bb * tt + tt, :]
                kd = kh[rs] * jnp.exp(g_last - gh[rs])
                if state_t:
                    oi.append(_dot_nt(qg_all[rs, hs[h]], s_old.astype(BF16)))
                    s_ref[b_idx, h] = (jnp.exp(g_last) * s_old
                                       + _dot_tn(vb_all[rs, hs[h]], kd.astype(BF16)))
                else:
                    oi.append(_dot(qg_all[rs, hs[h]], s_old.astype(BF16)))
                    lhs = jnp.concatenate(
                        [kd] + [p[rs, hs[h]].astype(F32) for p in lf_pieces], axis=0)
                    rhs = jnp.concatenate(
                        [jnp.concatenate([iv_all[rs, hs[h]], zeros_t], axis=1), dec_rows], axis=0)
                    upd = _dot_tn(lhs.astype(BF16), rhs.astype(BF16))
                    s_ref[b_idx, h] = jnp.exp(upd[:, HG_DIM:]) * s_old + upd[:, :HG_DIM]
            o_inter.append(oi[0] if bpc == 1 else jnp.concatenate(oi, axis=0))

        yield
        nblk = len(att) // HG_HEADS
        for h in heads:
            sc = att[h * nblk:(h + 1) * nblk]
            sc = sc[0] if nblk == 1 else jnp.concatenate(sc, axis=0)
            sc = jnp.where(seg_mask, sc, 0.0).astype(BF16)
            o = _dot(sc, vb_all[:, hs[h]]) + o_inter[h]
            o = _rms(o, gnorm_ref[...])
            ogh = og_all[:, hs[h]]
            mix_scr[rows, S5_WIDTH + h * HG_DIM:S5_WIDTH + (h + 1) * HG_DIM] = (
                o * (ogh * _sigmoid(ogh))).astype(BF16)

    per_chunk = n_pieces // nchunks
    gens = [chunk_stages(c) for c in range(nchunks)]
    next(gens[0])
    for c in range(nchunks):
        pieces = range(c * per_chunk, (c + 1) * per_chunk)
        next(gens[c])
        for p in pieces[:per_chunk // 2]:
            readout_piece(p)
        if c + 1 < nchunks:
            next(gens[c + 1])
        next(gens[c])
        for p in pieces[per_chunk // 2:]:
            readout_piece(p)
        for _ in gens[c]:
            pass
    readout_finish()

    if state_t:
        @pl.when(tstep == pl.num_programs(1) - 1)
        def _untranspose_state():
            for b in range(nb):
                for h in heads:
                    s_ref[b, h] = s_ref[b, h].T

    x1 = x_ref[...].reshape(rows_tile, D_MODEL) + _dot(mix_scr[...], wout_ref[...])
    x1_ref[...] = x1.reshape(nb, tt, D_MODEL)


def _const_spec(shape):
    nd = len(shape)
    return pl.BlockSpec(shape, lambda i, j, _nd=nd: (0,) * _nd, pipeline_mode=pl.Buffered(1))


def _mixer(x, state, w, *, nb, tt):
    B, T, _ = x.shape
    has_state = state is not None
    grid = (B // nb, T // tt)
    bpc = CHUNK // tt
    nchunks = nb // bpc
    pitch = _seq_pitch(tt)
    scr_rows = nchunks * pitch * bpc

    consts = [w["norm_mix"], w["w_in"], w["wb"], w["a_re"], w["a_im"], w["wc"], w["d_skip"],
              w["w_glu"], w["lb"], w["g_norm"], w["w_out"], w["mask_" + str(tt)]]
    in_specs = [pl.BlockSpec((nb, tt, D_MODEL), lambda i, j: (i, j, 0))]
    args = [x]
    if has_state:
        in_specs += [pl.BlockSpec((nb, S5_FLAT), lambda i, j: (i, 0)),
                     pl.BlockSpec((nb, S5_FLAT), lambda i, j: (i, 0)),
                     pl.BlockSpec((None, nb, HG_HEADS, HG_DIM, HG_DIM),
                                  lambda i, j: (0, i, 0, 0, 0))]
        args += list(state)
    in_specs += [_const_spec(c.shape) for c in consts]
    args += consts

    out_shape = (jax.ShapeDtypeStruct((B, T, D_MODEL), F32),
                 jax.ShapeDtypeStruct((B, S5_FLAT), F32),
                 jax.ShapeDtypeStruct((B, S5_FLAT), F32),
                 jax.ShapeDtypeStruct((B, HG_HEADS, HG_DIM, HG_DIM), F32))
    out_specs = (pl.BlockSpec((nb, tt, D_MODEL), lambda i, j: (i, j, 0)),
                 pl.BlockSpec((nb, S5_FLAT), lambda i, j: (i, 0)),
                 pl.BlockSpec((nb, S5_FLAT), lambda i, j: (i, 0)),
                 pl.BlockSpec((nb, HG_HEADS, HG_DIM, HG_DIM), lambda i, j: (i, 0, 0, 0)))
    scratch = [pltpu.VMEM((nb * tt, IN_COLS), F32),
               pltpu.VMEM((N_SLABS, scr_rows, LANES), F32),
               pltpu.VMEM((N_SLABS, scr_rows, LANES), F32),
               pltpu.VMEM((nb * tt, D_MODEL), BF16)]
    return pl.pallas_call(
        functools.partial(_mixer_kernel, nb=nb, tt=tt, has_state=has_state),
        grid=grid, in_specs=in_specs, out_specs=out_specs, out_shape=out_shape,
        scratch_shapes=scratch,
        compiler_params=pltpu.CompilerParams(
            dimension_semantics=("arbitrary", "arbitrary"), vmem_limit_bytes=VMEM_LIMIT),
        name="mixer_t%d" % tt,
    )(*args)


FF_CHUNK = 256


def _ffn_kernel(xa_ref, xb_ref, nffn_ref, wg_ref, wu_ref, wd_ref, nfin_ref, ya_ref, yb_ref, *,
                steps_a):
    def tile(x_ref, y_ref):
        x = x_ref[...]
        h2 = _rms(x, nffn_ref[...]).astype(BF16)
        d_ff = wg_ref.shape[1]
        acc = x
        for f0 in range(0, d_ff, FF_CHUNK):
            gate = _dot(h2, wg_ref[:, f0:f0 + FF_CHUNK].astype(BF16))
            up = _dot(h2, wu_ref[:, f0:f0 + FF_CHUNK].astype(BF16))
            act = (gate * _sigmoid(gate) * up).astype(BF16)
            acc = acc + _dot(act, wd_ref[f0:f0 + FF_CHUNK, :].astype(BF16))
        y_ref[...] = _rms(acc, nfin_ref[...])

    step = pl.program_id(0)

    @pl.when(step < steps_a)
    def _first():
        tile(xa_ref, ya_ref)

    @pl.when(step >= steps_a)
    def _second():
        tile(xb_ref, yb_ref)


def _ffn(xa, xb, w, *, tm):
    steps_a, steps_b = xa.shape[0] // tm, xb.shape[0] // tm
    d_ff = w["w_gate"].shape[-1]

    def cspec(shape):
        return pl.BlockSpec(shape, lambda i: (0, 0), pipeline_mode=pl.Buffered(1))

    def layer_spec(shape):
        return pl.BlockSpec((None,) + shape, lambda i: (0, 0, 0), pipeline_mode=pl.Buffered(1))

    spec_a = pl.BlockSpec((tm, D_MODEL), lambda i: (jnp.minimum(i, steps_a - 1), 0))
    spec_b = pl.BlockSpec((tm, D_MODEL), lambda i: (jnp.maximum(i - steps_a, 0), 0))
    return pl.pallas_call(
        functools.partial(_ffn_kernel, steps_a=steps_a),
        grid=(steps_a + steps_b,),
        in_specs=[spec_a, spec_b,
                  cspec((1, D_MODEL)), layer_spec((D_MODEL, d_ff)), layer_spec((D_MODEL, d_ff)),
                  layer_spec((d_ff, D_MODEL)), cspec((1, D_MODEL))],
        out_specs=(spec_a, spec_b),
        out_shape=(jax.ShapeDtypeStruct(xa.shape, F32), jax.ShapeDtypeStruct(xb.shape, F32)),
        compiler_params=pltpu.CompilerParams(
            dimension_semantics=("arbitrary",), vmem_limit_bytes=VMEM_LIMIT),
        name="ffn",
    )(xa, xb, w["norm_ffn"], w["w_gate"], w["w_up"], w["w_down"], w["norm_final"])


HALF_GROUPS = S5_GROUPS // 2


def _block_diag_halves(stacks):
    rows = HALF_GROUPS * S5_GROUP
    cols = HALF_GROUPS * S5_STATE
    x = jnp.concatenate([m.reshape(2 * rows, S5_STATE) for m in stacks], axis=0)
    spread = np.tile(np.eye(S5_STATE, dtype=np.float32), (1, HALF_GROUPS))
    diag = (np.arange(rows)[:, None] // S5_GROUP) == (np.arange(cols)[None, :] // S5_STATE)
    y = jnp.dot(x, spread, precision=lax.Precision.HIGHEST)
    y = jnp.where(np.tile(diag, (2 * len(stacks), 1)), y, 0.0).astype(BF16)
    return y.reshape(len(stacks), 2, rows, cols)


def _segment_mask(tt):
    r = np.arange(CHUNK)
    same = (r[:, None] // tt) == (r[None, :] // tt)
    return jnp.asarray(same & (r[:, None] >= r[None, :]), dtype=BF16)


def _prepare(lb_param, norm_mix, w_in, s5_a_re, s5_a_im, s5_log_dt, s5_b_re, s5_b_im, s5_c_re,
             s5_c_im, s5_d, s5_w_glu, hg_norm, w_out, norm_ffn, w_gate, w_up, w_down, norm_final):
    l = 0
    a_re = s5_a_re[l].astype(F32)
    a_im = s5_a_im[l].astype(F32)
    dt = jnp.exp(s5_log_dt[l].astype(F32))[:, None]
    mag = jnp.exp(a_re * dt)
    ab_re = mag * jnp.cos(a_im * dt)
    ab_im = mag * jnp.sin(a_im * dt)
    den = a_re * a_re + a_im * a_im
    nr = ab_re - 1.0
    ni = ab_im
    f_re = (nr * a_re + ni * a_im) / den
    f_im = (ni * a_re - nr * a_im) / den
    b_re = s5_b_re[l].astype(F32).transpose(0, 2, 1)
    b_im = s5_b_im[l].astype(F32).transpose(0, 2, 1)
    bb_re = f_re[:, None, :] * b_re - f_im[:, None, :] * b_im
    bb_im = f_re[:, None, :] * b_im + f_im[:, None, :] * b_re
    blk = _block_diag_halves([bb_re, bb_im, s5_c_re[l].astype(F32), -s5_c_im[l].astype(F32)])
    wb = jnp.concatenate([blk[0], blk[1]], axis=2)
    wc = jnp.swapaxes(jnp.concatenate([blk[2], blk[3]], axis=2), 1, 2)
    lb_all = jnp.cumsum(jax.nn.softmax(lb_param.astype(F32), axis=0), axis=0)
    return {
        "norm_mix": norm_mix[l].reshape(1, D_MODEL).astype(F32),
        "w_in": w_in[l].astype(BF16),
        "wb": wb,
        "a_re": ab_re.reshape(1, S5_FLAT),
        "a_im": ab_im.reshape(1, S5_FLAT),
        "wc": wc,
        "d_skip": s5_d[l].reshape(1, S5_WIDTH).astype(F32),
        "w_glu": s5_w_glu[l].astype(BF16),
        "lb": lb_all[l].reshape(1, HG_WIDTH),
        "g_norm": hg_norm[l].reshape(1, HG_DIM).astype(F32),
        "w_out": w_out[l].astype(BF16),
        "mask_64": _segment_mask(64),
        "mask_8": _segment_mask(8),
        "norm_ffn": norm_ffn[l].reshape(1, D_MODEL).astype(F32),
        "w_gate": w_gate.astype(F32),
        "w_up": w_up.astype(F32),
        "w_down": w_down.astype(F32),
        "norm_final": norm_final.reshape(1, D_MODEL).astype(F32),
    }


def _states(h_re, h_im, s_new):
    b = h_re.shape[0]
    return (h_re.reshape(1, b, S5_GROUPS, S5_STATE), h_im.reshape(1, b, S5_GROUPS, S5_STATE),
            s_new.reshape(1, b, HG_HEADS, HG_DIM, HG_DIM))


def kernel(x_prompt, x_sample, state_s5_re, state_s5_im, state_hgrn, lb_param, norm_mix, w_in,
           s5_a_re, s5_a_im, s5_log_dt, s5_b_re, s5_b_im, s5_c_re, s5_c_im, s5_d, s5_w_glu,
           hg_norm, w_out, norm_ffn, w_gate, w_up, w_down, norm_final):
    assert norm_mix.shape[0] == 1, "single-layer trunk"
    w = _prepare(lb_param, norm_mix, w_in, s5_a_re, s5_a_im, s5_log_dt, s5_b_re, s5_b_im,
                 s5_c_re, s5_c_im, s5_d, s5_w_glu, hg_norm, w_out, norm_ffn, w_gate, w_up,
                 w_down, norm_final)
    bp, tp, _ = x_prompt.shape
    bs, ts, _ = x_sample.shape
    x1_p, *st_p = _mixer(x_prompt, None, w, nb=8, tt=64)
    st = (state_s5_re[0].reshape(bs, S5_FLAT), state_s5_im[0].reshape(bs, S5_FLAT), state_hgrn)
    x1_s, *st_s = _mixer(x_sample, st, w, nb=16, tt=8)
    y_p, y_s = _ffn(x1_p.reshape(bp * tp, D_MODEL), x1_s.reshape(bs * ts, D_MODEL), w, tm=512)
    return (y_p.reshape(bp, tp, D_MODEL), y_s.reshape(bs, ts, D_MODEL),
            *_states(*st_p), *_states(*st_s))
```

```python
import functools
import math

import jax
import jax.numpy as jnp
import numpy as np
from jax import lax
from jax.experimental import pallas as pl
from jax.experimental.pallas import tpu as pltpu

F32 = jnp.float32
BF16 = jnp.bfloat16

D_MODEL = 1024
S5_WIDTH = 512
S5_GROUP = 16
S5_GROUPS = 32
S5_STATE = 64
S5_FLAT = S5_GROUPS * S5_STATE
HG_WIDTH = 512
HG_DIM = 128
HG_HEADS = 4
IN_COLS = S5_WIDTH + 4 * HG_WIDTH
EPS = 1e-6

LANES = 128
SUBLANES = 8
CHUNK = 64
N_SLABS = S5_FLAT // LANES
HALF_U = S5_WIDTH // 2
HALF_SLABS = N_SLABS // 2
SUB = 16
READ_K = 256
SCAN_COLS = 256
SCAN_PAD = 4
VMEM_LIMIT = 56 * 1024 * 1024


def _seq_pitch(tt):
    return tt + SCAN_PAD if tt == CHUNK else tt


def _dot(a, b):
    return jnp.dot(a, b, preferred_element_type=F32)


def _dot_nt(a, b):
    return lax.dot_general(a, b, (((1,), (1,)), ((), ())), preferred_element_type=F32)


def _dot_tn(a, b):
    return lax.dot_general(a, b, (((0,), (0,)), ((), ())), preferred_element_type=F32)


def _split3(x):
    hi = x.astype(BF16)
    r1 = x - hi.astype(F32)
    mid = r1.astype(BF16)
    lo = (r1 - mid.astype(F32)).astype(BF16)
    return hi, mid, lo


def _rms(x, g):
    return x * lax.rsqrt(jnp.mean(x * x, axis=-1, keepdims=True) + EPS) * g


def _sigmoid(x):
    return jax.nn.sigmoid(x)


def _gelu_tanh(x):
    c = math.sqrt(2.0 / math.pi)
    return 0.5 * x * (1.0 + jnp.tanh(c * (x + 0.044715 * (x * x * x))))


def _mixer_kernel(*refs, nb, tt, has_state):
    it = iter(refs)
    x_ref = next(it)
    if has_state:
        h0re_ref, h0im_ref, s0_ref = next(it), next(it), next(it)
    (norm_ref, win_ref, wb_ref, are_ref, aim_ref, wc_ref, dskip_ref, wglu_ref, lb_ref,
     gnorm_ref, wout_ref, mask_ref) = (next(it) for _ in range(12))
    x1_ref, hre_ref, him_ref, s_ref = (next(it) for _ in range(4))
    proj_scr, bre_scr, bim_scr, mix_scr = (next(it) for _ in range(4))

    bpc = CHUNK // tt
    nchunks = nb // bpc
    pitch = _seq_pitch(tt)
    cpitch = pitch * bpc
    ngroups = nb // SUBLANES
    state_t = tt == CHUNK
    assert not (state_t and has_state)

    tstep = pl.program_id(1)

    @pl.when(tstep == 0)
    def _init():
        if has_state:
            hre_ref[...] = h0re_ref[...]
            him_ref[...] = h0im_ref[...]
        else:
            hre_ref[...] = jnp.zeros_like(hre_ref)
            him_ref[...] = jnp.zeros_like(him_ref)
            s_ref[...] = jnp.zeros_like(s_ref)

    rows_tile = nb * tt
    scr_row0 = [c * cpitch for c in range(nchunks)]

    hn = _rms(x_ref[...].reshape(rows_tile, D_MODEL), norm_ref[...]).astype(BF16)
    proj_scr[:, 0:S5_WIDTH] = _dot(hn, win_ref[:, 0:S5_WIDTH])
    ub = proj_scr[:, 0:S5_WIDTH].astype(BF16)
    for hf in range(2):
        ubh = ub[:, hf * HALF_U:(hf + 1) * HALF_U]
        for part, scr in enumerate((bre_scr, bim_scr)):
            bu = _dot(ubh, wb_ref[part, hf])
            for jj in range(HALF_SLABS):
                for c in range(nchunks):
                    scr[hf * HALF_SLABS + jj, scr_row0[c]:scr_row0[c] + CHUNK, :] = (
                        bu[c * CHUNK:(c + 1) * CHUNK, jj * LANES:(jj + 1) * LANES])

    n_rest = (IN_COLS - S5_WIDTH) // SCAN_COLS
    steps_per_iter = tt // n_rest
    for i in range(n_rest):
        n0 = S5_WIDTH + i * SCAN_COLS
        proj_scr[:, n0:n0 + SCAN_COLS] = _dot(hn, win_ref[:, n0:n0 + SCAN_COLS])
        t0 = i * steps_per_iter
        for g in range(ngroups):
            base = g * SUBLANES * pitch
            grows = slice(g * SUBLANES, (g + 1) * SUBLANES)
            for j in range(N_SLABS):
                lanes = slice(j * LANES, (j + 1) * LANES)
                a_re = jnp.broadcast_to(are_ref[:, lanes], (SUBLANES, LANES))
                a_im = jnp.broadcast_to(aim_ref[:, lanes], (SUBLANES, LANES))
                h_re = hre_ref[grows, lanes]
                h_im = him_ref[grows, lanes]
                for dt in range(steps_per_iter):
                    idx = pl.ds(base + t0 + dt, SUBLANES, stride=pitch)
                    n_re = a_re * h_re - a_im * h_im + bre_scr[j, idx, :]
                    n_im = a_re * h_im + a_im * h_re + bim_scr[j, idx, :]
                    bre_scr[j, idx, :] = n_re
                    bim_scr[j, idx, :] = n_im
                    h_re, h_im = n_re, n_im
                hre_ref[grows, lanes] = h_re
                him_ref[grows, lanes] = h_im

    ys = [None, None]
    kb_per_half = 2 * HALF_SLABS * LANES // READ_K

    def readout_piece(p):
        hf, kb = divmod(p, kb_per_half)
        part, kb = divmod(kb, kb_per_half // 2)
        scr = (bre_scr, bim_scr)[part]
        j0 = hf * HALF_SLABS + (READ_K // LANES) * kb
        piece = jnp.concatenate(
            [jnp.concatenate([scr[j0 + d, scr_row0[c]:scr_row0[c] + CHUNK, :]
                              for c in range(nchunks)], axis=0)
             for d in range(READ_K // LANES)], axis=1).astype(BF16)
        term = _dot(piece, wc_ref[part, hf, kb * READ_K:(kb + 1) * READ_K, :])
        ys[hf] = term if ys[hf] is None else ys[hf] + term

    def readout_finish():
        y = jnp.concatenate(ys, axis=1) + dskip_ref[...] * proj_scr[:, 0:S5_WIDTH]
        gact = _gelu_tanh(y)
        out5 = gact * _sigmoid(_dot(gact.astype(BF16), wglu_ref[...]))
        mix_scr[:, 0:S5_WIDTH] = out5.astype(BF16)

    n_pieces = 2 * kb_per_half

    lb = lb_ref[...]
    seg_mask = mask_ref[...].astype(F32) > 0.5
    zeros_t = jnp.zeros((tt, HG_DIM), F32)
    dec_rows = jnp.concatenate(
        [jnp.zeros((3 * tt, HG_DIM), F32), jnp.ones((3 * tt, HG_DIM), F32)], axis=1)
    heads = range(HG_HEADS)

    def chunk_stages(c):
        rows = slice(c * CHUNK, (c + 1) * CHUNK)
        q_all = proj_scr[rows, S5_WIDTH:S5_WIDTH + HG_WIDTH]
        fz = proj_scr[rows, S5_WIDTH + HG_WIDTH:S5_WIDTH + 2 * HG_WIDTH]
        iv_all = proj_scr[rows, S5_WIDTH + 2 * HG_WIDTH:S5_WIDTH + 3 * HG_WIDTH]
        og_all = proj_scr[rows, S5_WIDTH + 3 * HG_WIDTH:S5_WIDTH + 4 * HG_WIDTH]
        f = lb + (1.0 - lb) * _sigmoid(fz)
        lf_all = jnp.log(f)
        k_all = 1.0 - f
        lf_pieces = _split3(lf_all)
        g_all = sum(_dot(mask_ref[...], p) for p in lf_pieces)
        vb_all = iv_all.astype(BF16)
        qg_all = (q_all * jnp.exp(g_all)).astype(BF16)
        hs = [slice(h * HG_DIM, (h + 1) * HG_DIM) for h in heads]

        qs, ks = [], []
        for h in heads:
            gh, qh, kh = g_all[:, hs[h]], q_all[:, hs[h]], k_all[:, hs[h]]
            if tt == CHUNK:
                nblk = CHUNK // SUB
                blocks = [slice(b * SUB, (b + 1) * SUB) for b in range(nblk)]
                refs = [gh[b * SUB + SUB // 2:b * SUB + SUB // 2 + 1, :] for b in range(nblk)]
                ref_rows = jnp.concatenate(
                    [jnp.broadcast_to(r, (SUB, HG_DIM)) for r in refs], axis=0)
                q_loc = (qh * jnp.exp(gh - ref_rows)).astype(BF16)
                k_loc = kh * jnp.exp(ref_rows - gh)
                no_keys = jnp.zeros((SUB, HG_DIM), BF16)
                for i in range(nblk):
                    qs.append(q_loc[blocks[i]])
                    parts = [(k_loc[blocks[j]] * jnp.exp(refs[i] - refs[j])).astype(BF16)
                             for j in range(i)]
                    parts.append(k_loc[blocks[i]].astype(BF16))
                    parts += [no_keys] * (nblk - 1 - i)
                    ks.append(jnp.concatenate(parts, axis=0))
            else:
                g3 = gh.reshape(bpc, tt, HG_DIM)
                ref3 = jnp.broadcast_to(g3[:, tt // 2:tt // 2 + 1, :], (bpc, tt, HG_DIM))
                ref_rows = ref3.reshape(CHUNK, HG_DIM)
                qs.append((qh * jnp.exp(gh - ref_rows)).astype(BF16))
                ks.append((kh * jnp.exp(ref_rows - gh)).astype(BF16))

        yield
        att = [_dot_nt(a, b) for a, b in zip(qs, ks)]

        yield
        o_inter = []
        for h in heads:
            gh, kh = g_all[:, hs[h]], k_all[:, hs[h]]
            oi = []
            for bb in range(bpc):
                rs = slice(bb * tt, (bb + 1) * tt)
                b_idx = c * bpc + bb
                s_old = s0_ref[b_idx, h] if has_state else s_ref[b_idx, h]
                g_last = gh[bb * tt + tt - 1:bb * tt + tt, :]
                kd = kh[rs] * jnp.exp(g_last - gh[rs])
                if state_t:
                    oi.append(_dot_nt(qg_all[rs, hs[h]], s_old.astype(BF16)))
                    s_ref[b_idx, h] = (jnp.exp(g_last) * s_old
                                       + _dot_tn(vb_all[rs, hs[h]], kd.astype(BF16)))
                else:
                    oi.append(_dot(qg_all[rs, hs[h]], s_old.astype(BF16)))
                    lhs = jnp.concatenate(
                        [kd] + [p[rs, hs[h]].astype(F32) for p in lf_pieces], axis=0)
                    rhs = jnp.concatenate(
                        [jnp.concatenate([iv_all[rs, hs[h]], zeros_t], axis=1), dec_rows], axis=0)
                    upd = _dot_tn(lhs.astype(BF16), rhs.astype(BF16))
                    s_ref[b_idx, h] = jnp.exp(upd[:, HG_DIM:]) * s_old + upd[:, :HG_DIM]
            o_inter.append(oi[0] if bpc == 1 else jnp.concatenate(oi, axis=0))

        yield
        nblk = len(att) // HG_HEADS
        for h in heads:
            sc = att[h * nblk:(h + 1) * nblk]
            sc = sc[0] if nblk == 1 else jnp.concatenate(sc, axis=0)
            sc = jnp.where(seg_mask, sc, 0.0).astype(BF16)
            o = _dot(sc, vb_all[:, hs[h]]) + o_inter[h]
            o = _rms(o, gnorm_ref[...])
            ogh = og_all[:, hs[h]]
            mix_scr[rows, S5_WIDTH + h * HG_DIM:S5_WIDTH + (h + 1) * HG_DIM] = (
                o * (ogh * _sigmoid(ogh))).astype(BF16)

    per_chunk = n_pieces // nchunks
    gens = [chunk_stages(c) for c in range(nchunks)]
    next(gens[0])
    for c in range(nchunks):
        pieces = range(c * per_chunk, (c + 1) * per_chunk)
        next(gens[c])
        for p in pieces[:per_chunk // 2]:
            readout_piece(p)
        if c + 1 < nchunks:
            next(gens[c + 1])
        next(gens[c])
        for p in pieces[per_chunk // 2:]:
            readout_piece(p)
        for _ in gens[c]:
            pass
    readout_finish()

    if state_t:
        @pl.when(tstep == pl.num_programs(1) - 1)
        def _untranspose_state():
            for b in range(nb):
                for h in heads:
                    s_ref[b, h] = s_ref[b, h].T

    x1 = x_ref[...].reshape(rows_tile, D_MODEL) + _dot(mix_scr[...], wout_ref[...])
    x1_ref[...] = x1.reshape(nb, tt, D_MODEL)


def _const_spec(shape):
    nd = len(shape)
    return pl.BlockSpec(shape, lambda i, j, _nd=nd: (0,) * _nd, pipeline_mode=pl.Buffered(1))


def _mixer(x, state, w, *, nb, tt):
    B, T, _ = x.shape
    has_state = state is not None
    grid = (B // nb, T // tt)
    bpc = CHUNK // tt
    nchunks = nb // bpc
    pitch = _seq_pitch(tt)
    scr_rows = nchunks * pitch * bpc

    consts = [w["norm_mix"], w["w_in"], w["wb"], w["a_re"], w["a_im"], w["wc"], w["d_skip"],
              w["w_glu"], w["lb"], w["g_norm"], w["w_out"], w["mask_" + str(tt)]]
    in_specs = [pl.BlockSpec((nb, tt, D_MODEL), lambda i, j: (i, j, 0))]
    args = [x]
    if has_state:
        in_specs += [pl.BlockSpec((nb, S5_FLAT), lambda i, j: (i, 0)),
                     pl.BlockSpec((nb, S5_FLAT), lambda i, j: (i, 0)),
                     pl.BlockSpec((None, nb, HG_HEADS, HG_DIM, HG_DIM),
                                  lambda i, j: (0, i, 0, 0, 0))]
        args += list(state)
    in_specs += [_const_spec(c.shape) for c in consts]
    args += consts

    out_shape = (jax.ShapeDtypeStruct((B, T, D_MODEL), F32),
                 jax.ShapeDtypeStruct((B, S5_FLAT), F32),
                 jax.ShapeDtypeStruct((B, S5_FLAT), F32),
                 jax.ShapeDtypeStruct((B, HG_HEADS, HG_DIM, HG_DIM), F32))
    out_specs = (pl.BlockSpec((nb, tt, D_MODEL), lambda i, j: (i, j, 0)),
                 pl.BlockSpec((nb, S5_FLAT), lambda i, j: (i, 0)),
                 pl.BlockSpec((nb, S5_FLAT), lambda i, j: (i, 0)),
                 pl.BlockSpec((nb, HG_HEADS, HG_DIM, HG_DIM), lambda i, j: (i, 0, 0, 0)))
    scratch = [pltpu.VMEM((nb * tt, IN_COLS), F32),
               pltpu.VMEM((N_SLABS, scr_rows, LANES), F32),
               pltpu.VMEM((N_SLABS, scr_rows, LANES), F32),
               pltpu.VMEM((nb * tt, D_MODEL), BF16)]
    return pl.pallas_call(
        functools.partial(_mixer_kernel, nb=nb, tt=tt, has_state=has_state),
        grid=grid, in_specs=in_specs, out_specs=out_specs, out_shape=out_shape,
        scratch_shapes=scratch,
        compiler_params=pltpu.CompilerParams(
            dimension_semantics=("arbitrary", "arbitrary"), vmem_limit_bytes=VMEM_LIMIT),
        name="mixer_t%d" % tt,
    )(*args)


FF_CHUNK = 256


def _ffn_kernel(xa_ref, xb_ref, nffn_ref, wg_ref, wu_ref, wd_ref, nfin_ref, ya_ref, yb_ref, *,
                steps_a):
    def tile(x_ref, y_ref):
        x = x_ref[...]
        h2 = _rms(x, nffn_ref[...]).astype(BF16)
        d_ff = wg_ref.shape[1]
        acc = x
        for f0 in range(0, d_ff, FF_CHUNK):
            gate = _dot(h2, wg_ref[:, f0:f0 + FF_CHUNK].astype(BF16))
            up = _dot(h2, wu_ref[:, f0:f0 + FF_CHUNK].astype(BF16))
            act = (gate * _sigmoid(gate) * up).astype(BF16)
            acc = acc + _dot(act, wd_ref[f0:f0 + FF_CHUNK, :].astype(BF16))
        y_ref[...] = _rms(acc, nfin_ref[...])

    step = pl.program_id(0)

    @pl.when(step < steps_a)
    def _first():
        tile(xa_ref, ya_ref)

    @pl.when(step >= steps_a)
    def _second():
        tile(xb_ref, yb_ref)


def _ffn(xa, xb, w, *, tm):
    steps_a, steps_b = xa.shape[0] // tm, xb.shape[0] // tm
    d_ff = w["w_gate"].shape[-1]

    def cspec(shape):
        return pl.BlockSpec(shape, lambda i: (0, 0), pipeline_mode=pl.Buffered(1))

    def layer_spec(shape):
        return pl.BlockSpec((None,) + shape, lambda i: (0, 0, 0), pipeline_mode=pl.Buffered(1))

    spec_a = pl.BlockSpec((tm, D_MODEL), lambda i: (jnp.minimum(i, steps_a - 1), 0))
    spec_b = pl.BlockSpec((tm, D_MODEL), lambda i: (jnp.maximum(i - steps_a, 0), 0))
    return pl.pallas_call(
        functools.partial(_ffn_kernel, steps_a=steps_a),
        grid=(steps_a + steps_b,),
        in_specs=[spec_a, spec_b,
                  cspec((1, D_MODEL)), layer_spec((D_MODEL, d_ff)), layer_spec((D_MODEL, d_ff)),
                  layer_spec((d_ff, D_MODEL)), cspec((1, D_MODEL))],
        out_specs=(spec_a, spec_b),
        out_shape=(jax.ShapeDtypeStruct(xa.shape, F32), jax.ShapeDtypeStruct(xb.shape, F32)),
        compiler_params=pltpu.CompilerParams(
            dimension_semantics=("arbitrary",), vmem_limit_bytes=VMEM_LIMIT),
        name="ffn",
    )(xa, xb, w["norm_ffn"], w["w_gate"], w["w_up"], w["w_down"], w["norm_final"])


HALF_GROUPS = S5_GROUPS // 2


def _block_diag_halves(stacks):
    rows = HALF_GROUPS * S5_GROUP
    cols = HALF_GROUPS * S5_STATE
    x = jnp.concatenate([m.reshape(2 * rows, S5_STATE) for m in stacks], axis=0).astype(BF16)
    spread = jnp.asarray(np.tile(np.eye(S5_STATE), (1, HALF_GROUPS)), dtype=BF16)
    diag = (np.arange(rows)[:, None] // S5_GROUP) == (np.arange(cols)[None, :] // S5_STATE)
    y = jnp.dot(x, spread, preferred_element_type=BF16)
    y = jnp.where(np.tile(diag, (2 * len(stacks), 1)), y, jnp.zeros((), BF16))
    return y.reshape(len(stacks), 2, rows, cols)


def _segment_mask(tt):
    r = np.arange(CHUNK)
    same = (r[:, None] // tt) == (r[None, :] // tt)
    return jnp.asarray(same & (r[:, None] >= r[None, :]), dtype=BF16)


def _prepare(lb_param, norm_mix, w_in, s5_a_re, s5_a_im, s5_log_dt, s5_b_re, s5_b_im, s5_c_re,
             s5_c_im, s5_d, s5_w_glu, hg_norm, w_out, norm_ffn, w_gate, w_up, w_down, norm_final):
    l = 0
    a_re = s5_a_re[l].astype(F32)
    a_im = s5_a_im[l].astype(F32)
    dt = jnp.exp(s5_log_dt[l].astype(F32))[:, None]
    mag = jnp.exp(a_re * dt)
    ab_re = mag * jnp.cos(a_im * dt)
    ab_im = mag * jnp.sin(a_im * dt)
    den = a_re * a_re + a_im * a_im
    nr = ab_re - 1.0
    ni = ab_im
    f_re = (nr * a_re + ni * a_im) / den
    f_im = (ni * a_re - nr * a_im) / den
    b_re = s5_b_re[l].astype(F32).transpose(0, 2, 1)
    b_im = s5_b_im[l].astype(F32).transpose(0, 2, 1)
    bb_re = f_re[:, None, :] * b_re - f_im[:, None, :] * b_im
    bb_im = f_re[:, None, :] * b_im + f_im[:, None, :] * b_re
    blk = _block_diag_halves([bb_re, bb_im, s5_c_re[l].astype(F32), -s5_c_im[l].astype(F32)])
    wb = blk[0:2]
    wc = jnp.swapaxes(blk[2:4], 2, 3)
    lb_all = jnp.cumsum(jax.nn.softmax(lb_param.astype(F32), axis=0), axis=0)
    return {
        "norm_mix": norm_mix[l].reshape(1, D_MODEL).astype(F32),
        "w_in": w_in[l].astype(BF16),
        "wb": wb,
        "a_re": ab_re.reshape(1, S5_FLAT),
        "a_im": ab_im.reshape(1, S5_FLAT),
        "wc": wc,
        "d_skip": s5_d[l].reshape(1, S5_WIDTH).astype(F32),
        "w_glu": s5_w_glu[l].astype(BF16),
        "lb": lb_all[l].reshape(1, HG_WIDTH),
        "g_norm": hg_norm[l].reshape(1, HG_DIM).astype(F32),
        "w_out": w_out[l].astype(BF16),
        "mask_64": _segment_mask(64),
        "mask_8": _segment_mask(8),
        "norm_ffn": norm_ffn[l].reshape(1, D_MODEL).astype(F32),
        "w_gate": w_gate.astype(F32),
        "w_up": w_up.astype(F32),
        "w_down": w_down.astype(F32),
        "norm_final": norm_final.reshape(1, D_MODEL).astype(F32),
    }


def _states(h_re, h_im, s_new):
    b = h_re.shape[0]
    return (h_re.reshape(1, b, S5_GROUPS, S5_STATE), h_im.reshape(1, b, S5_GROUPS, S5_STATE),
            s_new.reshape(1, b, HG_HEADS, HG_DIM, HG_DIM))


def kernel(x_prompt, x_sample, state_s5_re, state_s5_im, state_hgrn, lb_param, norm_mix, w_in,
           s5_a_re, s5_a_im, s5_log_dt, s5_b_re, s5_b_im, s5_c_re, s5_c_im, s5_d, s5_w_glu,
           hg_norm, w_out, norm_ffn, w_gate, w_up, w_down, norm_final):
    assert norm_mix.shape[0] == 1, "single-layer trunk"
    w = _prepare(lb_param, norm_mix, w_in, s5_a_re, s5_a_im, s5_log_dt, s5_b_re, s5_b_im,
                 s5_c_re, s5_c_im, s5_d, s5_w_glu, hg_norm, w_out, norm_ffn, w_gate, w_up,
                 w_down, norm_final)
    bp, tp, _ = x_prompt.shape
    bs, ts, _ = x_sample.shape
    x1_p, *st_p = _mixer(x_prompt, None, w, nb=8, tt=64)
    st = (state_s5_re[0].reshape(bs, S5_FLAT), state_s5_im[0].reshape(bs, S5_FLAT), state_hgrn)
    x1_s, *st_s = _mixer(x_sample, st, w, nb=16, tt=8)
    y_p, y_s = _ffn(x1_p.reshape(bp * tp, D_MODEL), x1_s.reshape(bs * ts, D_MODEL), w, tm=512)
    return (y_p.reshape(bp, tp, D_MODEL), y_s.reshape(bs, ts, D_MODEL),
            *_states(*st_p), *_states(*st_s))
```

```python
import functools
import math

import jax
import jax.numpy as jnp
import numpy as np
from jax import lax
from jax.experimental import pallas as pl
from jax.experimental.pallas import tpu as pltpu

F32 = jnp.float32
BF16 = jnp.bfloat16

D_MODEL = 1024
S5_WIDTH = 512
S5_GROUP = 16
S5_GROUPS = 32
S5_STATE = 64
S5_FLAT = S5_GROUPS * S5_STATE
HG_WIDTH = 512
HG_DIM = 128
HG_HEADS = 4
IN_COLS = S5_WIDTH + 4 * HG_WIDTH
EPS = 1e-6

LANES = 128
SUBLANES = 8
CHUNK = 64
N_SLABS = S5_FLAT // LANES
HALF_U = S5_WIDTH // 2
HALF_SLABS = N_SLABS // 2
SUB = 16
READ_K = 256
SCAN_COLS = 256
SCAN_PAD = 4
VMEM_LIMIT = 56 * 1024 * 1024


def _seq_pitch(tt):
    return tt + SCAN_PAD if tt == CHUNK else tt


def _dot(a, b):
    return jnp.dot(a, b, preferred_element_type=F32)


def _dot_nt(a, b):
    return lax.dot_general(a, b, (((1,), (1,)), ((), ())), preferred_element_type=F32)


def _dot_tn(a, b):
    return lax.dot_general(a, b, (((0,), (0,)), ((), ())), preferred_element_type=F32)


def _split3(x):
    hi = x.astype(BF16)
    r1 = x - hi.astype(F32)
    mid = r1.astype(BF16)
    lo = (r1 - mid.astype(F32)).astype(BF16)
    return hi, mid, lo


def _rms(x, g):
    return x * lax.rsqrt(jnp.mean(x * x, axis=-1, keepdims=True) + EPS) * g


def _sigmoid(x):
    return jax.nn.sigmoid(x)


def _gelu_tanh(x):
    c = math.sqrt(2.0 / math.pi)
    return 0.5 * x * (1.0 + jnp.tanh(c * (x + 0.044715 * (x * x * x))))


def _mixer_kernel(*refs, nb, tt, has_state):
    it = iter(refs)
    x_ref = next(it)
    if has_state:
        h0re_ref, h0im_ref, s0_ref = next(it), next(it), next(it)
    (norm_ref, win_ref, wb_ref, are_ref, aim_ref, wc_ref, dskip_ref, wglu_ref, lb_ref,
     gnorm_ref, wout_ref, mask_ref) = (next(it) for _ in range(12))
    x1_ref, hre_ref, him_ref, s_ref = (next(it) for _ in range(4))
    proj_scr, bre_scr, bim_scr, mix_scr = (next(it) for _ in range(4))

    bpc = CHUNK // tt
    nchunks = nb // bpc
    pitch = _seq_pitch(tt)
    cpitch = pitch * bpc
    ngroups = nb // SUBLANES
    state_t = tt == CHUNK
    assert not (state_t and has_state)

    tstep = pl.program_id(1)

    @pl.when(tstep == 0)
    def _init():
        if has_state:
            hre_ref[...] = h0re_ref[...]
            him_ref[...] = h0im_ref[...]
        else:
            hre_ref[...] = jnp.zeros_like(hre_ref)
            him_ref[...] = jnp.zeros_like(him_ref)
            s_ref[...] = jnp.zeros_like(s_ref)

    rows_tile = nb * tt
    scr_row0 = [c * cpitch for c in range(nchunks)]

    hn = _rms(x_ref[...].reshape(rows_tile, D_MODEL), norm_ref[...]).astype(BF16)
    proj_scr[:, 0:S5_WIDTH] = _dot(hn, win_ref[:, 0:S5_WIDTH])
    ub = proj_scr[:, 0:S5_WIDTH].astype(BF16)
    for hf in range(2):
        ubh = ub[:, hf * HALF_U:(hf + 1) * HALF_U]
        for part, scr in enumerate((bre_scr, bim_scr)):
            bu = _dot(ubh, wb_ref[part, hf])
            for jj in range(HALF_SLABS):
                for c in range(nchunks):
                    scr[hf * HALF_SLABS + jj, scr_row0[c]:scr_row0[c] + CHUNK, :] = (
                        bu[c * CHUNK:(c + 1) * CHUNK, jj * LANES:(jj + 1) * LANES])

    n_rest = (IN_COLS - S5_WIDTH) // SCAN_COLS
    steps_per_iter = tt // n_rest
    for i in range(n_rest):
        n0 = S5_WIDTH + i * SCAN_COLS
        proj_scr[:, n0:n0 + SCAN_COLS] = _dot(hn, win_ref[:, n0:n0 + SCAN_COLS])
        t0 = i * steps_per_iter
        for g in range(ngroups):
            base = g * SUBLANES * pitch
            grows = slice(g * SUBLANES, (g + 1) * SUBLANES)
            for j in range(N_SLABS):
                lanes = slice(j * LANES, (j + 1) * LANES)
                a_re = jnp.broadcast_to(are_ref[:, lanes], (SUBLANES, LANES))
                a_im = jnp.broadcast_to(aim_ref[:, lanes], (SUBLANES, LANES))
                h_re = hre_ref[grows, lanes]
                h_im = him_ref[grows, lanes]
                for dt in range(steps_per_iter):
                    idx = pl.ds(base + t0 + dt, SUBLANES, stride=pitch)
                    n_re = a_re * h_re - a_im * h_im + bre_scr[j, idx, :]
                    n_im = a_re * h_im + a_im * h_re + bim_scr[j, idx, :]
                    bre_scr[j, idx, :] = n_re
                    bim_scr[j, idx, :] = n_im
                    h_re, h_im = n_re, n_im
                hre_ref[grows, lanes] = h_re
                him_ref[grows, lanes] = h_im

    ys = [None, None]
    kb_per_half = 2 * HALF_SLABS * LANES // READ_K

    def readout_piece(p):
        hf, kb = divmod(p, kb_per_half)
        part, kb = divmod(kb, kb_per_half // 2)
        scr = (bre_scr, bim_scr)[part]
        j0 = hf * HALF_SLABS + (READ_K // LANES) * kb
        piece = jnp.concatenate(
            [jnp.concatenate([scr[j0 + d, scr_row0[c]:scr_row0[c] + CHUNK, :]
                              for c in range(nchunks)], axis=0)
             for d in range(READ_K // LANES)], axis=1).astype(BF16)
        term = _dot(piece, wc_ref[part, hf, kb * READ_K:(kb + 1) * READ_K, :])
        ys[hf] = term if ys[hf] is None else ys[hf] + term

    def readout_finish():
        y = jnp.concatenate(ys, axis=1) + dskip_ref[...] * proj_scr[:, 0:S5_WIDTH]
        gact = _gelu_tanh(y)
        out5 = gact * _sigmoid(_dot(gact.astype(BF16), wglu_ref[...]))
        mix_scr[:, 0:S5_WIDTH] = out5.astype(BF16)

    n_pieces = 2 * kb_per_half

    lb = lb_ref[...]
    seg_mask = mask_ref[...].astype(F32) > 0.5
    zeros_t = jnp.zeros((tt, HG_DIM), F32)
    dec_rows = jnp.concatenate(
        [jnp.zeros((3 * tt, HG_DIM), F32), jnp.ones((3 * tt, HG_DIM), F32)], axis=1)
    heads = range(HG_HEADS)

    def chunk_stages(c):
        rows = slice(c * CHUNK, (c + 1) * CHUNK)
        q_all = proj_scr[rows, S5_WIDTH:S5_WIDTH + HG_WIDTH]
        fz = proj_scr[rows, S5_WIDTH + HG_WIDTH:S5_WIDTH + 2 * HG_WIDTH]
        iv_all = proj_scr[rows, S5_WIDTH + 2 * HG_WIDTH:S5_WIDTH + 3 * HG_WIDTH]
        og_all = proj_scr[rows, S5_WIDTH + 3 * HG_WIDTH:S5_WIDTH + 4 * HG_WIDTH]
        f = lb + (1.0 - lb) * _sigmoid(fz)
        lf_all = jnp.log(f)
        k_all = 1.0 - f
        lf_pieces = _split3(lf_all)
        g_all = sum(_dot(mask_ref[...], p) for p in lf_pieces)
        vb_all = iv_all.astype(BF16)
        qg_all = (q_all * jnp.exp(g_all)).astype(BF16)
        hs = [slice(h * HG_DIM, (h + 1) * HG_DIM) for h in heads]

        qs, ks = [], []
        for h in heads:
            gh, qh, kh = g_all[:, hs[h]], q_all[:, hs[h]], k_all[:, hs[h]]
            if tt == CHUNK:
                nblk = CHUNK // SUB
                blocks = [slice(b * SUB, (b + 1) * SUB) for b in range(nblk)]
                refs = [gh[b * SUB + SUB // 2:b * SUB + SUB // 2 + 1, :] for b in range(nblk)]
                ref_rows = jnp.concatenate(
                    [jnp.broadcast_to(r, (SUB, HG_DIM)) for r in refs], axis=0)
                q_loc = (qh * jnp.exp(gh - ref_rows)).astype(BF16)
                k_loc = kh * jnp.exp(ref_rows - gh)
                no_keys = jnp.zeros((SUB, HG_DIM), BF16)
                for i in range(nblk):
                    qs.append(q_loc[blocks[i]])
                    parts = [(k_loc[blocks[j]] * jnp.exp(refs[i] - refs[j])).astype(BF16)
                             for j in range(i)]
                    parts.append(k_loc[blocks[i]].astype(BF16))
                    parts += [no_keys] * (nblk - 1 - i)
                    ks.append(jnp.concatenate(parts, axis=0))
            else:
                g3 = gh.reshape(bpc, tt, HG_DIM)
                ref3 = jnp.broadcast_to(g3[:, tt // 2:tt // 2 + 1, :], (bpc, tt, HG_DIM))
                ref_rows = ref3.reshape(CHUNK, HG_DIM)
                qs.append((qh * jnp.exp(gh - ref_rows)).astype(BF16))
                ks.append((kh * jnp.exp(ref_rows - gh)).astype(BF16))

        yield
        att = [_dot_nt(a, b) for a, b in zip(qs, ks)]

        yield
        o_inter = []
        for h in heads:
            gh, kh = g_all[:, hs[h]], k_all[:, hs[h]]
            oi = []
            for bb in range(bpc):
                rs = slice(bb * tt, (bb + 1) * tt)
                b_idx = c * bpc + bb
                s_old = s0_ref[b_idx, h] if has_state else s_ref[b_idx, h]
                g_last = gh[bb * tt + tt - 1:bb * tt + tt, :]
                kd = kh[rs] * jnp.exp(g_last - gh[rs])
                if state_t:
                    oi.append(_dot_nt(qg_all[rs, hs[h]], s_old.astype(BF16)))
                    s_ref[b_idx, h] = (jnp.exp(g_last) * s_old
                                       + _dot_tn(vb_all[rs, hs[h]], kd.astype(BF16)))
                else:
                    oi.append(_dot(qg_all[rs, hs[h]], s_old.astype(BF16)))
                    lhs = jnp.concatenate(
                        [kd] + [p[rs, hs[h]].astype(F32) for p in lf_pieces], axis=0)
                    rhs = jnp.concatenate(
                        [jnp.concatenate([iv_all[rs, hs[h]], zeros_t], axis=1), dec_rows], axis=0)
                    upd = _dot_tn(lhs.astype(BF16), rhs.astype(BF16))
                    s_ref[b_idx, h] = jnp.exp(upd[:, HG_DIM:]) * s_old + upd[:, :HG_DIM]
            o_inter.append(oi[0] if bpc == 1 else jnp.concatenate(oi, axis=0))

        yield
        nblk = len(att) // HG_HEADS
        for h in heads:
            sc = att[h * nblk:(h + 1) * nblk]
            sc = sc[0] if nblk == 1 else jnp.concatenate(sc, axis=0)
            sc = jnp.where(seg_mask, sc, 0.0).astype(BF16)
            o = _dot(sc, vb_all[:, hs[h]]) + o_inter[h]
            o = _rms(o, gnorm_ref[...])
            ogh = og_all[:, hs[h]]
            mix_scr[rows, S5_WIDTH + h * HG_DIM:S5_WIDTH + (h + 1) * HG_DIM] = (
                o * (ogh * _sigmoid(ogh))).astype(BF16)

    per_chunk = n_pieces // nchunks
    gens = [chunk_stages(c) for c in range(nchunks)]
    next(gens[0])
    for c in range(nchunks):
        pieces = range(c * per_chunk, (c + 1) * per_chunk)
        next(gens[c])
        for p in pieces[:per_chunk // 2]:
            readout_piece(p)
        if c + 1 < nchunks:
            next(gens[c + 1])
        next(gens[c])
        for p in pieces[per_chunk // 2:]:
            readout_piece(p)
        for _ in gens[c]:
            pass
    readout_finish()

    if state_t:
        @pl.when(tstep == pl.num_programs(1) - 1)
        def _untranspose_state():
            for b in range(nb):
                for h in heads:
                    s_ref[b, h] = s_ref[b, h].T

    x1 = x_ref[...].reshape(rows_tile, D_MODEL) + _dot(mix_scr[...], wout_ref[...])
    x1_ref[...] = x1.reshape(nb, tt, D_MODEL)


def _const_spec(shape):
    nd = len(shape)
    return pl.BlockSpec(shape, lambda i, j, _nd=nd: (0,) * _nd, pipeline_mode=pl.Buffered(1))


def _mixer(x, state, w, *, nb, tt):
    B, T, _ = x.shape
    has_state = state is not None
    grid = (B // nb, T // tt)
    bpc = CHUNK // tt
    nchunks = nb // bpc
    pitch = _seq_pitch(tt)
    scr_rows = nchunks * pitch * bpc

    consts = [w["norm_mix"], w["w_in"], w["wb"], w["a_re"], w["a_im"], w["wc"], w["d_skip"],
              w["w_glu"], w["lb"], w["g_norm"], w["w_out"], w["mask_" + str(tt)]]
    in_specs = [pl.BlockSpec((nb, tt, D_MODEL), lambda i, j: (i, j, 0))]
    args = [x]
    if has_state:
        in_specs += [pl.BlockSpec((nb, S5_FLAT), lambda i, j: (i, 0)),
                     pl.BlockSpec((nb, S5_FLAT), lambda i, j: (i, 0)),
                     pl.BlockSpec((None, nb, HG_HEADS, HG_DIM, HG_DIM),
                                  lambda i, j: (0, i, 0, 0, 0))]
        args += list(state)
    in_specs += [_const_spec(c.shape) for c in consts]
    args += consts

    out_shape = (jax.ShapeDtypeStruct((B, T, D_MODEL), F32),
                 jax.ShapeDtypeStruct((B, S5_FLAT), F32),
                 jax.ShapeDtypeStruct((B, S5_FLAT), F32),
                 jax.ShapeDtypeStruct((B, HG_HEADS, HG_DIM, HG_DIM), F32))
    out_specs = (pl.BlockSpec((nb, tt, D_MODEL), lambda i, j: (i, j, 0)),
                 pl.BlockSpec((nb, S5_FLAT), lambda i, j: (i, 0)),
                 pl.BlockSpec((nb, S5_FLAT), lambda i, j: (i, 0)),
                 pl.BlockSpec((nb, HG_HEADS, HG_DIM, HG_DIM), lambda i, j: (i, 0, 0, 0)))
    scratch = [pltpu.VMEM((nb * tt, IN_COLS), F32),
               pltpu.VMEM((N_SLABS, scr_rows, LANES), F32),
               pltpu.VMEM((N_SLABS, scr_rows, LANES), F32),
               pltpu.VMEM((nb * tt, D_MODEL), BF16)]
    return pl.pallas_call(
        functools.partial(_mixer_kernel, nb=nb, tt=tt, has_state=has_state),
        grid=grid, in_specs=in_specs, out_specs=out_specs, out_shape=out_shape,
        scratch_shapes=scratch,
        compiler_params=pltpu.CompilerParams(
            dimension_semantics=("arbitrary", "arbitrary"), vmem_limit_bytes=VMEM_LIMIT),
        name="mixer_t%d" % tt,
    )(*args)


FF_CHUNK = 256


def _ffn_kernel(xa_ref, xb_ref, nffn_ref, wg_ref, wu_ref, wd_ref, nfin_ref, ya_ref, yb_ref,
                act_scr, *, steps_a):
    def tile(x_ref, y_ref):
        x = x_ref[...]
        h2 = _rms(x, nffn_ref[...]).astype(BF16)
        d_ff = wg_ref.shape[1]
        for f0 in range(0, d_ff, FF_CHUNK):
            gate = _dot(h2, wg_ref[:, f0:f0 + FF_CHUNK].astype(BF16))
            up = _dot(h2, wu_ref[:, f0:f0 + FF_CHUNK].astype(BF16))
            act_scr[:, f0:f0 + FF_CHUNK] = (gate * _sigmoid(gate) * up).astype(BF16)
        acc = x + _dot(act_scr[...], wd_ref[...].astype(BF16))
        y_ref[...] = _rms(acc, nfin_ref[...])

    step = pl.program_id(0)

    @pl.when(step < steps_a)
    def _first():
        tile(xa_ref, ya_ref)

    @pl.when(step >= steps_a)
    def _second():
        tile(xb_ref, yb_ref)


def _ffn(xa, xb, w, *, tm):
    steps_a, steps_b = xa.shape[0] // tm, xb.shape[0] // tm
    d_ff = w["w_gate"].shape[-1]

    def cspec(shape):
        return pl.BlockSpec(shape, lambda i: (0, 0), pipeline_mode=pl.Buffered(1))

    def layer_spec(shape):
        return pl.BlockSpec((None,) + shape, lambda i: (0, 0, 0), pipeline_mode=pl.Buffered(1))

    spec_a = pl.BlockSpec((tm, D_MODEL), lambda i: (jnp.minimum(i, steps_a - 1), 0))
    spec_b = pl.BlockSpec((tm, D_MODEL), lambda i: (jnp.maximum(i - steps_a, 0), 0))
    return pl.pallas_call(
        functools.partial(_ffn_kernel, steps_a=steps_a),
        grid=(steps_a + steps_b,),
        in_specs=[spec_a, spec_b,
                  cspec((1, D_MODEL)), layer_spec((D_MODEL, d_ff)), layer_spec((D_MODEL, d_ff)),
                  layer_spec((d_ff, D_MODEL)), cspec((1, D_MODEL))],
        out_specs=(spec_a, spec_b),
        out_shape=(jax.ShapeDtypeStruct(xa.shape, F32), jax.ShapeDtypeStruct(xb.shape, F32)),
        scratch_shapes=[pltpu.VMEM((tm, d_ff), BF16)],
        compiler_params=pltpu.CompilerParams(
            dimension_semantics=("arbitrary",), vmem_limit_bytes=VMEM_LIMIT),
        name="ffn",
    )(xa, xb, w["norm_ffn"], w["w_gate"], w["w_up"], w["w_down"], w["norm_final"])


HALF_GROUPS = S5_GROUPS // 2


def _block_diag_halves(stacks):
    rows = HALF_GROUPS * S5_GROUP
    cols = HALF_GROUPS * S5_STATE
    x = jnp.concatenate([m.reshape(2 * rows, S5_STATE) for m in stacks], axis=0).astype(BF16)
    spread = jnp.asarray(np.tile(np.eye(S5_STATE), (1, HALF_GROUPS)), dtype=BF16)
    diag = (np.arange(rows)[:, None] // S5_GROUP) == (np.arange(cols)[None, :] // S5_STATE)
    y = jnp.dot(x, spread, preferred_element_type=BF16)
    y = jnp.where(np.tile(diag, (2 * len(stacks), 1)), y, jnp.zeros((), BF16))
    return y.reshape(len(stacks), 2, rows, cols)


def _segment_mask(tt):
    r = np.arange(CHUNK)
    same = (r[:, None] // tt) == (r[None, :] // tt)
    return jnp.asarray(same & (r[:, None] >= r[None, :]), dtype=BF16)


def _prepare(lb_param, norm_mix, w_in, s5_a_re, s5_a_im, s5_log_dt, s5_b_re, s5_b_im, s5_c_re,
             s5_c_im, s5_d, s5_w_glu, hg_norm, w_out, norm_ffn, w_gate, w_up, w_down, norm_final):
    l = 0
    a_re = s5_a_re[l].astype(F32)
    a_im = s5_a_im[l].astype(F32)
    dt = jnp.exp(s5_log_dt[l].astype(F32))[:, None]
    mag = jnp.exp(a_re * dt)
    ab_re = mag * jnp.cos(a_im * dt)
    ab_im = mag * jnp.sin(a_im * dt)
    den = a_re * a_re + a_im * a_im
    nr = ab_re - 1.0
    ni = ab_im
    f_re = (nr * a_re + ni * a_im) / den
    f_im = (ni * a_re - nr * a_im) / den
    b_re = s5_b_re[l].astype(F32).transpose(0, 2, 1)
    b_im = s5_b_im[l].astype(F32).transpose(0, 2, 1)
    bb_re = f_re[:, None, :] * b_re - f_im[:, None, :] * b_im
    bb_im = f_re[:, None, :] * b_im + f_im[:, None, :] * b_re
    blk = _block_diag_halves([bb_re, bb_im, s5_c_re[l].astype(F32), -s5_c_im[l].astype(F32)])
    wb = blk[0:2]
    wc = jnp.swapaxes(blk[2:4], 2, 3)
    lb_all = jnp.cumsum(jax.nn.softmax(lb_param.astype(F32), axis=0), axis=0)
    return {
        "norm_mix": norm_mix[l].reshape(1, D_MODEL).astype(F32),
        "w_in": w_in[l].astype(BF16),
        "wb": wb,
        "a_re": ab_re.reshape(1, S5_FLAT),
        "a_im": ab_im.reshape(1, S5_FLAT),
        "wc": wc,
        "d_skip": s5_d[l].reshape(1, S5_WIDTH).astype(F32),
        "w_glu": s5_w_glu[l].astype(BF16),
        "lb": lb_all[l].reshape(1, HG_WIDTH),
        "g_norm": hg_norm[l].reshape(1, HG_DIM).astype(F32),
        "w_out": w_out[l].astype(BF16),
        "mask_64": _segment_mask(64),
        "mask_8": _segment_mask(8),
        "norm_ffn": norm_ffn[l].reshape(1, D_MODEL).astype(F32),
        "w_gate": w_gate.astype(F32),
        "w_up": w_up.astype(F32),
        "w_down": w_down.astype(F32),
        "norm_final": norm_final.reshape(1, D_MODEL).astype(F32),
    }


def _states(h_re, h_im, s_new):
    b = h_re.shape[0]
    return (h_re.reshape(1, b, S5_GROUPS, S5_STATE), h_im.reshape(1, b, S5_GROUPS, S5_STATE),
            s_new.reshape(1, b, HG_HEADS, HG_DIM, HG_DIM))


def kernel(x_prompt, x_sample, state_s5_re, state_s5_im, state_hgrn, lb_param, norm_mix, w_in,
           s5_a_re, s5_a_im, s5_log_dt, s5_b_re, s5_b_im, s5_c_re, s5_c_im, s5_d, s5_w_glu,
           hg_norm, w_out, norm_ffn, w_gate, w_up, w_down, norm_final):
    assert norm_mix.shape[0] == 1, "single-layer trunk"
    w = _prepare(lb_param, norm_mix, w_in, s5_a_re, s5_a_im, s5_log_dt, s5_b_re, s5_b_im,
                 s5_c_re, s5_c_im, s5_d, s5_w_glu, hg_norm, w_out, norm_ffn, w_gate, w_up,
                 w_down, norm_final)
    bp, tp, _ = x_prompt.shape
    bs, ts, _ = x_sample.shape
    x1_p, *st_p = _mixer(x_prompt, None, w, nb=8, tt=64)
    st = (state_s5_re[0].reshape(bs, S5_FLAT), state_s5_im[0].reshape(bs, S5_FLAT), state_hgrn)
    x1_s, *st_s = _mixer(x_sample, st, w, nb=16, tt=8)
    y_p, y_s = _ffn(x1_p.reshape(bp * tp, D_MODEL), x1_s.reshape(bs * ts, D_MODEL), w, tm=512)
    return (y_p.reshape(bp, tp, D_MODEL), y_s.reshape(bs, ts, D_MODEL),
            *_states(*st_p), *_states(*st_s))
```

```python
import functools
import math

import jax
import jax.numpy as jnp
import numpy as np
from jax import lax
from jax.experimental import pallas as pl
from jax.experimental.pallas import tpu as pltpu

F32 = jnp.float32
BF16 = jnp.bfloat16

D_MODEL = 1024
S5_WIDTH = 512
S5_GROUP = 16
S5_GROUPS = 32
S5_STATE = 64
S5_FLAT = S5_GROUPS * S5_STATE
HG_WIDTH = 512
HG_DIM = 128
HG_HEADS = 4
IN_COLS = S5_WIDTH + 4 * HG_WIDTH
EPS = 1e-6

LANES = 128
SUBLANES = 8
CHUNK = 64
N_SLABS = S5_FLAT // LANES
HALF_U = S5_WIDTH // 2
HALF_SLABS = N_SLABS // 2
SUB = 16
READ_K = 256
SCAN_COLS = 256
SCAN_PAD = 4
VMEM_LIMIT = 56 * 1024 * 1024


def _seq_pitch(tt):
    return tt + SCAN_PAD if tt == CHUNK else tt


def _dot(a, b):
    return jnp.dot(a, b, preferred_element_type=F32)


def _dot_nt(a, b):
    return lax.dot_general(a, b, (((1,), (1,)), ((), ())), preferred_element_type=F32)


def _dot_tn(a, b):
    return lax.dot_general(a, b, (((0,), (0,)), ((), ())), preferred_element_type=F32)


def _split3(x):
    hi = x.astype(BF16)
    r1 = x - hi.astype(F32)
    mid = r1.astype(BF16)
    lo = (r1 - mid.astype(F32)).astype(BF16)
    return hi, mid, lo


def _rms(x, g):
    return x * lax.rsqrt(jnp.mean(x * x, axis=-1, keepdims=True) + EPS) * g


def _sigmoid(x):
    return jax.nn.sigmoid(x)


def _gelu_tanh(x):
    c = math.sqrt(2.0 / math.pi)
    return 0.5 * x * (1.0 + jnp.tanh(c * (x + 0.044715 * (x * x * x))))


def _mixer_kernel(*refs, nb, tt, has_state):
    it = iter(refs)
    x_ref = next(it)
    if has_state:
        h0re_ref, h0im_ref, s0_ref = next(it), next(it), next(it)
    (norm_ref, win_ref, wb_ref, are_ref, aim_ref, wc_ref, dskip_ref, wglu_ref, lb_ref,
     gnorm_ref, wout_ref, mask_ref) = (next(it) for _ in range(12))
    x1_ref, hre_ref, him_ref, s_ref = (next(it) for _ in range(4))
    proj_scr, bre_scr, bim_scr, mix_scr = (next(it) for _ in range(4))

    bpc = CHUNK // tt
    nchunks = nb // bpc
    pitch = _seq_pitch(tt)
    cpitch = pitch * bpc
    ngroups = nb // SUBLANES
    state_t = tt == CHUNK
    assert not (state_t and has_state)

    tstep = pl.program_id(1)

    @pl.when(tstep == 0)
    def _init():
        if has_state:
            hre_ref[...] = h0re_ref[...]
            him_ref[...] = h0im_ref[...]
        else:
            hre_ref[...] = jnp.zeros_like(hre_ref)
            him_ref[...] = jnp.zeros_like(him_ref)
            s_ref[...] = jnp.zeros_like(s_ref)

    rows_tile = nb * tt
    scr_row0 = [c * cpitch for c in range(nchunks)]

    x_in = x_ref[...].reshape(rows_tile, D_MODEL)
    hn = (x_in * norm_ref[...]).astype(BF16)
    r_in = lax.rsqrt(jnp.mean(x_in * x_in, axis=-1, keepdims=True) + EPS)
    proj_scr[:, 0:S5_WIDTH] = r_in * _dot(hn, win_ref[:, 0:S5_WIDTH])
    ub = proj_scr[:, 0:S5_WIDTH].astype(BF16)
    for hf in range(2):
        ubh = ub[:, hf * HALF_U:(hf + 1) * HALF_U]
        for part, scr in enumerate((bre_scr, bim_scr)):
            bu = _dot(ubh, wb_ref[part, hf])
            for jj in range(HALF_SLABS):
                for c in range(nchunks):
                    scr[hf * HALF_SLABS + jj, scr_row0[c]:scr_row0[c] + CHUNK, :] = (
                        bu[c * CHUNK:(c + 1) * CHUNK, jj * LANES:(jj + 1) * LANES])

    n_rest = (IN_COLS - S5_WIDTH) // SCAN_COLS
    steps_per_iter = tt // n_rest
    for i in range(n_rest):
        n0 = S5_WIDTH + i * SCAN_COLS
        proj_scr[:, n0:n0 + SCAN_COLS] = r_in * _dot(hn, win_ref[:, n0:n0 + SCAN_COLS])
        t0 = i * steps_per_iter
        for g in range(ngroups):
            base = g * SUBLANES * pitch
            grows = slice(g * SUBLANES, (g + 1) * SUBLANES)
            for j in range(N_SLABS):
                lanes = slice(j * LANES, (j + 1) * LANES)
                a_re = jnp.broadcast_to(are_ref[:, lanes], (SUBLANES, LANES))
                a_im = jnp.broadcast_to(aim_ref[:, lanes], (SUBLANES, LANES))
                h_re = hre_ref[grows, lanes]
                h_im = him_ref[grows, lanes]
                for dt in range(steps_per_iter):
                    idx = pl.ds(base + t0 + dt, SUBLANES, stride=pitch)
                    n_re = a_re * h_re - a_im * h_im + bre_scr[j, idx, :]
                    n_im = a_re * h_im + a_im * h_re + bim_scr[j, idx, :]
                    bre_scr[j, idx, :] = n_re
                    bim_scr[j, idx, :] = n_im
                    h_re, h_im = n_re, n_im
                hre_ref[grows, lanes] = h_re
                him_ref[grows, lanes] = h_im

    ys = [None, None]
    kb_per_half = 2 * HALF_SLABS * LANES // READ_K

    def readout_piece(p):
        hf, kb = divmod(p, kb_per_half)
        part, kb = divmod(kb, kb_per_half // 2)
        scr = (bre_scr, bim_scr)[part]
        j0 = hf * HALF_SLABS + (READ_K // LANES) * kb
        piece = jnp.concatenate(
            [jnp.concatenate([scr[j0 + d, scr_row0[c]:scr_row0[c] + CHUNK, :]
                              for c in range(nchunks)], axis=0)
             for d in range(READ_K // LANES)], axis=1).astype(BF16)
        term = _dot(piece, wc_ref[part, hf, kb * READ_K:(kb + 1) * READ_K, :])
        ys[hf] = term if ys[hf] is None else ys[hf] + term

    def readout_finish():
        y = jnp.concatenate(ys, axis=1) + dskip_ref[...] * proj_scr[:, 0:S5_WIDTH]
        gact = _gelu_tanh(y)
        out5 = gact * _sigmoid(_dot(gact.astype(BF16), wglu_ref[...]))
        mix_scr[:, 0:S5_WIDTH] = out5.astype(BF16)

    n_pieces = 2 * kb_per_half

    lb = lb_ref[...]
    seg_mask = mask_ref[...].astype(F32) > 0.5
    zeros_t = jnp.zeros((tt, HG_DIM), F32)
    dec_rows = jnp.concatenate(
        [jnp.zeros((3 * tt, HG_DIM), F32), jnp.ones((3 * tt, HG_DIM), F32)], axis=1)
    heads = range(HG_HEADS)

    def chunk_stages(c):
        rows = slice(c * CHUNK, (c + 1) * CHUNK)
        q_all = proj_scr[rows, S5_WIDTH:S5_WIDTH + HG_WIDTH]
        fz = proj_scr[rows, S5_WIDTH + HG_WIDTH:S5_WIDTH + 2 * HG_WIDTH]
        iv_all = proj_scr[rows, S5_WIDTH + 2 * HG_WIDTH:S5_WIDTH + 3 * HG_WIDTH]
        og_all = proj_scr[rows, S5_WIDTH + 3 * HG_WIDTH:S5_WIDTH + 4 * HG_WIDTH]
        f = lb + (1.0 - lb) * _sigmoid(fz)
        lf_all = jnp.log(f)
        k_all = 1.0 - f
        lf_pieces = _split3(lf_all)
        g_all = sum(_dot(mask_ref[...], p) for p in lf_pieces)
        vb_all = iv_all.astype(BF16)
        qg_all = (q_all * jnp.exp(g_all)).astype(BF16)
        hs = [slice(h * HG_DIM, (h + 1) * HG_DIM) for h in heads]

        qs, ks = [], []
        for h in heads:
            gh, qh, kh = g_all[:, hs[h]], q_all[:, hs[h]], k_all[:, hs[h]]
            if tt == CHUNK:
                nblk = CHUNK // SUB
                blocks = [slice(b * SUB, (b + 1) * SUB) for b in range(nblk)]
                refs = [gh[b * SUB + SUB // 2:b * SUB + SUB // 2 + 1, :] for b in range(nblk)]
                ref_rows = jnp.concatenate(
                    [jnp.broadcast_to(r, (SUB, HG_DIM)) for r in refs], axis=0)
                q_loc = (qh * jnp.exp(gh - ref_rows)).astype(BF16)
                k_loc = kh * jnp.exp(ref_rows - gh)
                no_keys = jnp.zeros((SUB, HG_DIM), BF16)
                for i in range(nblk):
                    qs.append(q_loc[blocks[i]])
                    parts = [(k_loc[blocks[j]] * jnp.exp(refs[i] - refs[j])).astype(BF16)
                             for j in range(i)]
                    parts.append(k_loc[blocks[i]].astype(BF16))
                    parts += [no_keys] * (nblk - 1 - i)
                    ks.append(jnp.concatenate(parts, axis=0))
            else:
                g3 = gh.reshape(bpc, tt, HG_DIM)
                ref3 = jnp.broadcast_to(g3[:, tt // 2:tt // 2 + 1, :], (bpc, tt, HG_DIM))
                ref_rows = ref3.reshape(CHUNK, HG_DIM)
                qs.append((qh * jnp.exp(gh - ref_rows)).astype(BF16))
                ks.append((kh * jnp.exp(ref_rows - gh)).astype(BF16))

        yield
        att = [_dot_nt(a, b) for a, b in zip(qs, ks)]

        yield
        o_inter = []
        for h in heads:
            gh, kh = g_all[:, hs[h]], k_all[:, hs[h]]
            oi = []
            for bb in range(bpc):
                rs = slice(bb * tt, (bb + 1) * tt)
                b_idx = c * bpc + bb
                s_old = s0_ref[b_idx, h] if has_state else s_ref[b_idx, h]
                g_last = gh[bb * tt + tt - 1:bb * tt + tt, :]
                kd = kh[rs] * jnp.exp(g_last - gh[rs])
                if state_t:
                    oi.append(_dot_nt(qg_all[rs, hs[h]], s_old.astype(BF16)))
                    s_ref[b_idx, h] = (jnp.exp(g_last) * s_old
                                       + _dot_tn(vb_all[rs, hs[h]], kd.astype(BF16)))
                else:
                    oi.append(_dot(qg_all[rs, hs[h]], s_old.astype(BF16)))
                    lhs = jnp.concatenate(
                        [kd] + [p[rs, hs[h]].astype(F32) for p in lf_pieces], axis=0)
                    rhs = jnp.concatenate(
                        [jnp.concatenate([iv_all[rs, hs[h]], zeros_t], axis=1), dec_rows], axis=0)
                    upd = _dot_tn(lhs.astype(BF16), rhs.astype(BF16))
                    s_ref[b_idx, h] = jnp.exp(upd[:, HG_DIM:]) * s_old + upd[:, :HG_DIM]
            o_inter.append(oi[0] if bpc == 1 else jnp.concatenate(oi, axis=0))

        yield
        nblk = len(att) // HG_HEADS
        for h in heads:
            sc = att[h * nblk:(h + 1) * nblk]
            sc = sc[0] if nblk == 1 else jnp.concatenate(sc, axis=0)
            sc = jnp.where(seg_mask, sc, 0.0).astype(BF16)
            o = _dot(sc, vb_all[:, hs[h]]) + o_inter[h]
            o = _rms(o, gnorm_ref[...])
            ogh = og_all[:, hs[h]]
            mix_scr[rows, S5_WIDTH + h * HG_DIM:S5_WIDTH + (h + 1) * HG_DIM] = (
                o * (ogh * _sigmoid(ogh))).astype(BF16)

    per_chunk = n_pieces // nchunks
    gens = [chunk_stages(c) for c in range(nchunks)]
    next(gens[0])
    for c in range(nchunks):
        pieces = range(c * per_chunk, (c + 1) * per_chunk)
        next(gens[c])
        for p in pieces[:per_chunk // 2]:
            readout_piece(p)
        if c + 1 < nchunks:
            next(gens[c + 1])
        next(gens[c])
        for p in pieces[per_chunk // 2:]:
            readout_piece(p)
        for _ in gens[c]:
            pass
    readout_finish()

    if state_t:
        @pl.when(tstep == pl.num_programs(1) - 1)
        def _untranspose_state():
            for b in range(nb):
                for h in heads:
                    s_ref[b, h] = s_ref[b, h].T

    x1 = x_ref[...].reshape(rows_tile, D_MODEL) + _dot(mix_scr[...], wout_ref[...])
    x1_ref[...] = x1.reshape(nb, tt, D_MODEL)


def _const_spec(shape):
    nd = len(shape)
    return pl.BlockSpec(shape, lambda i, j, _nd=nd: (0,) * _nd, pipeline_mode=pl.Buffered(1))


def _mixer(x, state, w, *, nb, tt):
    B, T, _ = x.shape
    has_state = state is not None
    grid = (B // nb, T // tt)
    bpc = CHUNK // tt
    nchunks = nb // bpc
    pitch = _seq_pitch(tt)
    scr_rows = nchunks * pitch * bpc

    consts = [w["norm_mix"], w["w_in"], w["wb"], w["a_re"], w["a_im"], w["wc"], w["d_skip"],
              w["w_glu"], w["lb"], w["g_norm"], w["w_out"], w["mask_" + str(tt)]]
    in_specs = [pl.BlockSpec((nb, tt, D_MODEL), lambda i, j: (i, j, 0))]
    args = [x]
    if has_state:
        in_specs += [pl.BlockSpec((nb, S5_FLAT), lambda i, j: (i, 0)),
                     pl.BlockSpec((nb, S5_FLAT), lambda i, j: (i, 0)),
                     pl.BlockSpec((None, nb, HG_HEADS, HG_DIM, HG_DIM),
                                  lambda i, j: (0, i, 0, 0, 0))]
        args += list(state)
    in_specs += [_const_spec(c.shape) for c in consts]
    args += consts

    out_shape = (jax.ShapeDtypeStruct((B, T, D_MODEL), F32),
                 jax.ShapeDtypeStruct((B, S5_FLAT), F32),
                 jax.ShapeDtypeStruct((B, S5_FLAT), F32),
                 jax.ShapeDtypeStruct((B, HG_HEADS, HG_DIM, HG_DIM), F32))
    out_specs = (pl.BlockSpec((nb, tt, D_MODEL), lambda i, j: (i, j, 0)),
                 pl.BlockSpec((nb, S5_FLAT), lambda i, j: (i, 0)),
                 pl.BlockSpec((nb, S5_FLAT), lambda i, j: (i, 0)),
                 pl.BlockSpec((nb, HG_HEADS, HG_DIM, HG_DIM), lambda i, j: (i, 0, 0, 0)))
    scratch = [pltpu.VMEM((nb * tt, IN_COLS), F32),
               pltpu.VMEM((N_SLABS, scr_rows, LANES), F32),
               pltpu.VMEM((N_SLABS, scr_rows, LANES), F32),
               pltpu.VMEM((nb * tt, D_MODEL), BF16)]
    return pl.pallas_call(
        functools.partial(_mixer_kernel, nb=nb, tt=tt, has_state=has_state),
        grid=grid, in_specs=in_specs, out_specs=out_specs, out_shape=out_shape,
        scratch_shapes=scratch,
        compiler_params=pltpu.CompilerParams(
            dimension_semantics=("arbitrary", "arbitrary"), vmem_limit_bytes=VMEM_LIMIT),
        name="mixer_t%d" % tt,
    )(*args)


FF_CHUNK = 256


def _ffn_kernel(xa_ref, xb_ref, nffn_ref, wg_ref, wu_ref, wd_ref, nfin_ref, ya_ref, yb_ref,
                act_scr, *, steps_a):
    def tile(x_ref, y_ref):
        x = x_ref[...]
        xg = (x * nffn_ref[...]).astype(BF16)
        r = lax.rsqrt(jnp.mean(x * x, axis=-1, keepdims=True) + EPS)
        d_ff = wg_ref.shape[1]
        for f0 in range(0, d_ff, FF_CHUNK):
            gate = r * _dot(xg, wg_ref[:, f0:f0 + FF_CHUNK].astype(BF16))
            up = r * _dot(xg, wu_ref[:, f0:f0 + FF_CHUNK].astype(BF16))
            act_scr[:, f0:f0 + FF_CHUNK] = (gate * _sigmoid(gate) * up).astype(BF16)
        acc = x + _dot(act_scr[...], wd_ref[...].astype(BF16))
        y_ref[...] = _rms(acc, nfin_ref[...])

    step = pl.program_id(0)

    @pl.when(step < steps_a)
    def _first():
        tile(xa_ref, ya_ref)

    @pl.when(step >= steps_a)
    def _second():
        tile(xb_ref, yb_ref)


def _ffn(xa, xb, w, *, tm):
    steps_a, steps_b = xa.shape[0] // tm, xb.shape[0] // tm
    d_ff = w["w_gate"].shape[-1]

    def cspec(shape):
        return pl.BlockSpec(shape, lambda i: (0, 0), pipeline_mode=pl.Buffered(1))

    def layer_spec(shape):
        return pl.BlockSpec((None,) + shape, lambda i: (0, 0, 0), pipeline_mode=pl.Buffered(1))

    spec_a = pl.BlockSpec((tm, D_MODEL), lambda i: (jnp.minimum(i, steps_a - 1), 0))
    spec_b = pl.BlockSpec((tm, D_MODEL), lambda i: (jnp.maximum(i - steps_a, 0), 0))
    return pl.pallas_call(
        functools.partial(_ffn_kernel, steps_a=steps_a),
        grid=(steps_a + steps_b,),
        in_specs=[spec_a, spec_b,
                  cspec((1, D_MODEL)), layer_spec((D_MODEL, d_ff)), layer_spec((D_MODEL, d_ff)),
                  layer_spec((d_ff, D_MODEL)), cspec((1, D_MODEL))],
        out_specs=(spec_a, spec_b),
        out_shape=(jax.ShapeDtypeStruct(xa.shape, F32), jax.ShapeDtypeStruct(xb.shape, F32)),
        scratch_shapes=[pltpu.VMEM((tm, d_ff), BF16)],
        compiler_params=pltpu.CompilerParams(
            dimension_semantics=("arbitrary",), vmem_limit_bytes=VMEM_LIMIT),
        name="ffn",
    )(xa, xb, w["norm_ffn"], w["w_gate"], w["w_up"], w["w_down"], w["norm_final"])


HALF_GROUPS = S5_GROUPS // 2


def _block_diag_halves(stacks):
    rows = HALF_GROUPS * S5_GROUP
    cols = HALF_GROUPS * S5_STATE
    x = jnp.concatenate([m.reshape(2 * rows, S5_STATE) for m in stacks], axis=0).astype(BF16)
    spread = jnp.asarray(np.tile(np.eye(S5_STATE), (1, HALF_GROUPS)), dtype=BF16)
    diag = (np.arange(rows)[:, None] // S5_GROUP) == (np.arange(cols)[None, :] // S5_STATE)
    y = jnp.dot(x, spread, preferred_element_type=BF16)
    y = jnp.where(np.tile(diag, (2 * len(stacks), 1)), y, jnp.zeros((), BF16))
    return y.reshape(len(stacks), 2, rows, cols)


def _segment_mask(tt):
    r = np.arange(CHUNK)
    same = (r[:, None] // tt) == (r[None, :] // tt)
    return jnp.asarray(same & (r[:, None] >= r[None, :]), dtype=BF16)


def _prepare(lb_param, norm_mix, w_in, s5_a_re, s5_a_im, s5_log_dt, s5_b_re, s5_b_im, s5_c_re,
             s5_c_im, s5_d, s5_w_glu, hg_norm, w_out, norm_ffn, w_gate, w_up, w_down, norm_final):
    l = 0
    a_re = s5_a_re[l].astype(F32)
    a_im = s5_a_im[l].astype(F32)
    dt = jnp.exp(s5_log_dt[l].astype(F32))[:, None]
    mag = jnp.exp(a_re * dt)
    ab_re = mag * jnp.cos(a_im * dt)
    ab_im = mag * jnp.sin(a_im * dt)
    den = a_re * a_re + a_im * a_im
    nr = ab_re - 1.0
    ni = ab_im
    f_re = (nr * a_re + ni * a_im) / den
    f_im = (ni * a_re - nr * a_im) / den
    b_re = s5_b_re[l].astype(F32).transpose(0, 2, 1)
    b_im = s5_b_im[l].astype(F32).transpose(0, 2, 1)
    bb_re = f_re[:, None, :] * b_re - f_im[:, None, :] * b_im
    bb_im = f_re[:, None, :] * b_im + f_im[:, None, :] * b_re
    blk = _block_diag_halves([bb_re, bb_im, s5_c_re[l].astype(F32), -s5_c_im[l].astype(F32)])
    wb = blk[0:2]
    wc = jnp.swapaxes(blk[2:4], 2, 3)
    lb_all = jnp.cumsum(jax.nn.softmax(lb_param.astype(F32), axis=0), axis=0)
    return {
        "norm_mix": norm_mix[l].reshape(1, D_MODEL).astype(F32),
        "w_in": w_in[l].astype(BF16),
        "wb": wb,
        "a_re": ab_re.reshape(1, S5_FLAT),
        "a_im": ab_im.reshape(1, S5_FLAT),
        "wc": wc,
        "d_skip": s5_d[l].reshape(1, S5_WIDTH).astype(F32),
        "w_glu": s5_w_glu[l].astype(BF16),
        "lb": lb_all[l].reshape(1, HG_WIDTH),
        "g_norm": hg_norm[l].reshape(1, HG_DIM).astype(F32),
        "w_out": w_out[l].astype(BF16),
        "mask_64": _segment_mask(64),
        "mask_8": _segment_mask(8),
        "norm_ffn": norm_ffn[l].reshape(1, D_MODEL).astype(F32),
        "w_gate": w_gate.astype(F32),
        "w_up": w_up.astype(F32),
        "w_down": w_down.astype(F32),
        "norm_final": norm_final.reshape(1, D_MODEL).astype(F32),
    }


def _states(h_re, h_im, s_new):
    b = h_re.shape[0]
    return (h_re.reshape(1, b, S5_GROUPS, S5_STATE), h_im.reshape(1, b, S5_GROUPS, S5_STATE),
            s_new.reshape(1, b, HG_HEADS, HG_DIM, HG_DIM))


def kernel(x_prompt, x_sample, state_s5_re, state_s5_im, state_hgrn, lb_param, norm_mix, w_in,
           s5_a_re, s5_a_im, s5_log_dt, s5_b_re, s5_b_im, s5_c_re, s5_c_im, s5_d, s5_w_glu,
           hg_norm, w_out, norm_ffn, w_gate, w_up, w_down, norm_final):
    assert norm_mix.shape[0] == 1, "single-layer trunk"
    w = _prepare(lb_param, norm_mix, w_in, s5_a_re, s5_a_im, s5_log_dt, s5_b_re, s5_b_im,
                 s5_c_re, s5_c_im, s5_d, s5_w_glu, hg_norm, w_out, norm_ffn, w_gate, w_up,
                 w_down, norm_final)
    bp, tp, _ = x_prompt.shape
    bs, ts, _ = x_sample.shape
    x1_p, *st_p = _mixer(x_prompt, None, w, nb=8, tt=64)
    st = (state_s5_re[0].reshape(bs, S5_FLAT), state_s5_im[0].reshape(bs, S5_FLAT), state_hgrn)
    x1_s, *st_s = _mixer(x_sample, st, w, nb=16, tt=8)
    y_p, y_s = _ffn(x1_p.reshape(bp * tp, D_MODEL), x1_s.reshape(bs * ts, D_MODEL), w, tm=512)
    return (y_p.reshape(bp, tp, D_MODEL), y_s.reshape(bs, ts, D_MODEL),
            *_states(*st_p), *_states(*st_s))
```

```python
import functools
import math

import jax
import jax.numpy as jnp
import numpy as np
from jax import lax
from jax.experimental import pallas as pl
from jax.experimental.pallas import tpu as pltpu

F32 = jnp.float32
BF16 = jnp.bfloat16

D_MODEL = 1024
S5_WIDTH = 512
S5_GROUP = 16
S5_GROUPS = 32
S5_STATE = 64
S5_FLAT = S5_GROUPS * S5_STATE
HG_WIDTH = 512
HG_DIM = 128
HG_HEADS = 4
IN_COLS = S5_WIDTH + 4 * HG_WIDTH
EPS = 1e-6

LANES = 128
SUBLANES = 8
CHUNK = 64
N_SLABS = S5_FLAT // LANES
HALF_U = S5_WIDTH // 2
HALF_SLABS = N_SLABS // 2
SUB = 16
READ_K = 256
SCAN_COLS = 256
SCAN_PAD = 4
VMEM_LIMIT = 56 * 1024 * 1024


def _seq_pitch(tt):
    return tt + SCAN_PAD if tt == CHUNK else tt


def _dot(a, b):
    return jnp.dot(a, b, preferred_element_type=F32)


def _dot_nt(a, b):
    return lax.dot_general(a, b, (((1,), (1,)), ((), ())), preferred_element_type=F32)


def _dot_tn(a, b):
    return lax.dot_general(a, b, (((0,), (0,)), ((), ())), preferred_element_type=F32)


def _split3(x):
    hi = x.astype(BF16)
    r1 = x - hi.astype(F32)
    mid = r1.astype(BF16)
    lo = (r1 - mid.astype(F32)).astype(BF16)
    return hi, mid, lo


def _rms(x, g):
    return x * lax.rsqrt(jnp.mean(x * x, axis=-1, keepdims=True) + EPS) * g


def _sigmoid(x):
    return jax.nn.sigmoid(x)


def _gelu_tanh(x):
    c = math.sqrt(2.0 / math.pi)
    return 0.5 * x * (1.0 + jnp.tanh(c * (x + 0.044715 * (x * x * x))))


def _mixer_kernel(*refs, nb, tt, has_state):
    it = iter(refs)
    x_ref = next(it)
    if has_state:
        h0re_ref, h0im_ref, s0_ref = next(it), next(it), next(it)
    (norm_ref, win_ref, wb_ref, are_ref, aim_ref, wc_ref, dskip_ref, wglu_ref, lb_ref,
     gnorm_ref, wout_ref, mask_ref) = (next(it) for _ in range(12))
    x1_ref, hre_ref, him_ref, s_ref = (next(it) for _ in range(4))
    proj_scr, bre_scr, bim_scr, mix_scr = (next(it) for _ in range(4))

    bpc = CHUNK // tt
    nchunks = nb // bpc
    pitch = _seq_pitch(tt)
    cpitch = pitch * bpc
    ngroups = nb // SUBLANES
    state_t = tt == CHUNK
    assert not (state_t and has_state)

    tstep = pl.program_id(1)

    @pl.when(tstep == 0)
    def _init():
        if has_state:
            hre_ref[...] = h0re_ref[...]
            him_ref[...] = h0im_ref[...]
        else:
            hre_ref[...] = jnp.zeros_like(hre_ref)
            him_ref[...] = jnp.zeros_like(him_ref)
            s_ref[...] = jnp.zeros_like(s_ref)

    rows_tile = nb * tt
    scr_row0 = [c * cpitch for c in range(nchunks)]

    x_in = x_ref[...].reshape(rows_tile, D_MODEL)
    hn = (x_in * norm_ref[...]).astype(BF16)
    r_in = lax.rsqrt(jnp.mean(x_in * x_in, axis=-1, keepdims=True) + EPS)
    proj_scr[:, 0:S5_WIDTH] = r_in * _dot(hn, win_ref[:, 0:S5_WIDTH])
    ub = proj_scr[:, 0:S5_WIDTH].astype(BF16)
    for hf in range(2):
        ubh = ub[:, hf * HALF_U:(hf + 1) * HALF_U]
        for part, scr in enumerate((bre_scr, bim_scr)):
            bu = _dot(ubh, wb_ref[part, hf])
            for jj in range(HALF_SLABS):
                for c in range(nchunks):
                    scr[hf * HALF_SLABS + jj, scr_row0[c]:scr_row0[c] + CHUNK, :] = (
                        bu[c * CHUNK:(c + 1) * CHUNK, jj * LANES:(jj + 1) * LANES])

    n_rest = (IN_COLS - S5_WIDTH) // SCAN_COLS
    steps_per_iter = tt // n_rest
    for i in range(n_rest):
        n0 = S5_WIDTH + i * SCAN_COLS
        proj_scr[:, n0:n0 + SCAN_COLS] = r_in * _dot(hn, win_ref[:, n0:n0 + SCAN_COLS])
        t0 = i * steps_per_iter
        for g in range(ngroups):
            base = g * SUBLANES * pitch
            grows = slice(g * SUBLANES, (g + 1) * SUBLANES)
            for j in range(N_SLABS):
                lanes = slice(j * LANES, (j + 1) * LANES)
                a_re = jnp.broadcast_to(are_ref[:, lanes], (SUBLANES, LANES))
                a_im = jnp.broadcast_to(aim_ref[:, lanes], (SUBLANES, LANES))
                h_re = hre_ref[grows, lanes]
                h_im = him_ref[grows, lanes]
                for dt in range(steps_per_iter):
                    idx = pl.ds(base + t0 + dt, SUBLANES, stride=pitch)
                    n_re = a_re * h_re - a_im * h_im + bre_scr[j, idx, :]
                    n_im = a_re * h_im + a_im * h_re + bim_scr[j, idx, :]
                    bre_scr[j, idx, :] = n_re
                    bim_scr[j, idx, :] = n_im
                    h_re, h_im = n_re, n_im
                hre_ref[grows, lanes] = h_re
                him_ref[grows, lanes] = h_im

    ys = [None, None]
    kb_per_half = 2 * HALF_SLABS * LANES // READ_K

    def readout_piece(p):
        hf, kb = divmod(p, kb_per_half)
        part, kb = divmod(kb, kb_per_half // 2)
        scr = (bre_scr, bim_scr)[part]
        j0 = hf * HALF_SLABS + (READ_K // LANES) * kb
        piece = jnp.concatenate(
            [jnp.concatenate([scr[j0 + d, scr_row0[c]:scr_row0[c] + CHUNK, :]
                              for c in range(nchunks)], axis=0)
             for d in range(READ_K // LANES)], axis=1).astype(BF16)
        term = _dot(piece, wc_ref[part, hf, kb * READ_K:(kb + 1) * READ_K, :])
        ys[hf] = term if ys[hf] is None else ys[hf] + term

    def readout_finish():
        y = jnp.concatenate(ys, axis=1) + dskip_ref[...] * proj_scr[:, 0:S5_WIDTH]
        gact = _gelu_tanh(y)
        out5 = gact * _sigmoid(_dot(gact.astype(BF16), wglu_ref[...]))
        mix_scr[:, 0:S5_WIDTH] = out5.astype(BF16)

    n_pieces = 2 * kb_per_half

    lb = lb_ref[...]
    seg_mask = mask_ref[...].astype(F32) > 0.5
    zeros_t = jnp.zeros((tt, HG_DIM), F32)
    dec_rows = jnp.concatenate(
        [jnp.zeros((3 * tt, HG_DIM), F32), jnp.ones((3 * tt, HG_DIM), F32)], axis=1)
    heads = range(HG_HEADS)

    def chunk_stages(c):
        rows = slice(c * CHUNK, (c + 1) * CHUNK)
        q_all = proj_scr[rows, S5_WIDTH:S5_WIDTH + HG_WIDTH]
        fz = proj_scr[rows, S5_WIDTH + HG_WIDTH:S5_WIDTH + 2 * HG_WIDTH]
        iv_all = proj_scr[rows, S5_WIDTH + 2 * HG_WIDTH:S5_WIDTH + 3 * HG_WIDTH]
        og_all = proj_scr[rows, S5_WIDTH + 3 * HG_WIDTH:S5_WIDTH + 4 * HG_WIDTH]
        f = lb + (1.0 - lb) * _sigmoid(fz)
        lf_all = jnp.log(f)
        k_all = 1.0 - f
        lf_pieces = _split3(lf_all)
        g_all = sum(_dot(mask_ref[...], p) for p in lf_pieces)
        vb_all = iv_all.astype(BF16)
        qg_all = (q_all * jnp.exp(g_all)).astype(BF16)
        hs = [slice(h * HG_DIM, (h + 1) * HG_DIM) for h in heads]

        qs, ks = [], []
        for h in heads:
            gh, qh, kh = g_all[:, hs[h]], q_all[:, hs[h]], k_all[:, hs[h]]
            if tt == CHUNK:
                nblk = CHUNK // SUB
                blocks = [slice(b * SUB, (b + 1) * SUB) for b in range(nblk)]
                refs = [gh[b * SUB + SUB // 2:b * SUB + SUB // 2 + 1, :] for b in range(nblk)]
                ref_rows = jnp.concatenate(
                    [jnp.broadcast_to(r, (SUB, HG_DIM)) for r in refs], axis=0)
                q_loc = (qh * jnp.exp(gh - ref_rows)).astype(BF16)
                k_loc = kh * jnp.exp(ref_rows - gh)
                no_keys = jnp.zeros((SUB, HG_DIM), BF16)
                for i in range(nblk):
                    qs.append(q_loc[blocks[i]])
                    parts = [(k_loc[blocks[j]] * jnp.exp(refs[i] - refs[j])).astype(BF16)
                             for j in range(i)]
                    parts.append(k_loc[blocks[i]].astype(BF16))
                    parts += [no_keys] * (nblk - 1 - i)
                    ks.append(jnp.concatenate(parts, axis=0))
            else:
                g3 = gh.reshape(bpc, tt, HG_DIM)
                ref3 = jnp.broadcast_to(g3[:, tt // 2:tt // 2 + 1, :], (bpc, tt, HG_DIM))
                ref_rows = ref3.reshape(CHUNK, HG_DIM)
                qs.append((qh * jnp.exp(gh - ref_rows)).astype(BF16))
                ks.append((kh * jnp.exp(ref_rows - gh)).astype(BF16))

        yield
        att = [_dot_nt(a, b) for a, b in zip(qs, ks)]

        yield
        o_inter = []
        for h in heads:
            gh, kh = g_all[:, hs[h]], k_all[:, hs[h]]
            oi = []
            for bb in range(bpc):
                rs = slice(bb * tt, (bb + 1) * tt)
                b_idx = c * bpc + bb
                s_old = s0_ref[b_idx, h] if has_state else s_ref[b_idx, h]
                g_last = gh[bb * tt + tt - 1:bb * tt + tt, :]
                kd = kh[rs] * jnp.exp(g_last - gh[rs])
                if state_t:
                    oi.append(_dot_nt(qg_all[rs, hs[h]], s_old.astype(BF16)))
                    s_ref[b_idx, h] = (jnp.exp(g_last) * s_old
                                       + _dot_tn(vb_all[rs, hs[h]], kd.astype(BF16)))
                else:
                    oi.append(_dot(qg_all[rs, hs[h]], s_old.astype(BF16)))
                    lhs = jnp.concatenate(
                        [kd] + [p[rs, hs[h]].astype(F32) for p in lf_pieces], axis=0)
                    rhs = jnp.concatenate(
                        [jnp.concatenate([iv_all[rs, hs[h]], zeros_t], axis=1), dec_rows], axis=0)
                    upd = _dot_tn(lhs.astype(BF16), rhs.astype(BF16))
                    s_ref[b_idx, h] = jnp.exp(upd[:, HG_DIM:]) * s_old + upd[:, :HG_DIM]
            o_inter.append(oi[0] if bpc == 1 else jnp.concatenate(oi, axis=0))

        yield
        nblk = len(att) // HG_HEADS
        for h in heads:
            sc = att[h * nblk:(h + 1) * nblk]
            sc = sc[0] if nblk == 1 else jnp.concatenate(sc, axis=0)
            sc = jnp.where(seg_mask, sc, 0.0).astype(BF16)
            o = _dot(sc, vb_all[:, hs[h]]) + o_inter[h]
            o = _rms(o, gnorm_ref[...])
            ogh = og_all[:, hs[h]]
            mix_scr[rows, S5_WIDTH + h * HG_DIM:S5_WIDTH + (h + 1) * HG_DIM] = (
                o * (ogh * _sigmoid(ogh))).astype(BF16)

    per_chunk = n_pieces // nchunks
    gens = [chunk_stages(c) for c in range(nchunks)]
    next(gens[0])
    for c in range(nchunks):
        pieces = range(c * per_chunk, (c + 1) * per_chunk)
        next(gens[c])
        for p in pieces[:per_chunk // 2]:
            readout_piece(p)
        if c + 1 < nchunks:
            next(gens[c + 1])
        next(gens[c])
        for p in pieces[per_chunk // 2:]:
            readout_piece(p)
        for _ in gens[c]:
            pass
    readout_finish()

    if state_t:
        @pl.when(tstep == pl.num_programs(1) - 1)
        def _untranspose_state():
            for b in range(nb):
                for h in heads:
                    s_ref[b, h] = s_ref[b, h].T

    x1 = x_ref[...].reshape(rows_tile, D_MODEL) + _dot(mix_scr[...], wout_ref[...])
    x1_ref[...] = x1.reshape(nb, tt, D_MODEL)


def _const_spec(shape):
    nd = len(shape)
    return pl.BlockSpec(shape, lambda i, j, _nd=nd: (0,) * _nd, pipeline_mode=pl.Buffered(1))


def _mixer(x, state, w, *, nb, tt):
    B, T, _ = x.shape
    has_state = state is not None
    grid = (B // nb, T // tt)
    bpc = CHUNK // tt
    nchunks = nb // bpc
    pitch = _seq_pitch(tt)
    scr_rows = nchunks * pitch * bpc

    consts = [w["norm_mix"], w["w_in"], w["wb"], w["a_re"], w["a_im"], w["wc"], w["d_skip"],
              w["w_glu"], w["lb"], w["g_norm"], w["w_out"], w["mask_" + str(tt)]]
    in_specs = [pl.BlockSpec((nb, tt, D_MODEL), lambda i, j: (i, j, 0))]
    args = [x]
    if has_state:
        in_specs += [pl.BlockSpec((nb, S5_FLAT), lambda i, j: (i, 0)),
                     pl.BlockSpec((nb, S5_FLAT), lambda i, j: (i, 0)),
                     pl.BlockSpec((None, nb, HG_HEADS, HG_DIM, HG_DIM),
                                  lambda i, j: (0, i, 0, 0, 0))]
        args += list(state)
    in_specs += [_const_spec(c.shape) for c in consts]
    args += consts

    out_shape = (jax.ShapeDtypeStruct((B, T, D_MODEL), F32),
                 jax.ShapeDtypeStruct((B, S5_FLAT), F32),
                 jax.ShapeDtypeStruct((B, S5_FLAT), F32),
                 jax.ShapeDtypeStruct((B, HG_HEADS, HG_DIM, HG_DIM), F32))
    out_specs = (pl.BlockSpec((nb, tt, D_MODEL), lambda i, j: (i, j, 0)),
                 pl.BlockSpec((nb, S5_FLAT), lambda i, j: (i, 0)),
                 pl.BlockSpec((nb, S5_FLAT), lambda i, j: (i, 0)),
                 pl.BlockSpec((nb, HG_HEADS, HG_DIM, HG_DIM), lambda i, j: (i, 0, 0, 0)))
    scratch = [pltpu.VMEM((nb * tt, IN_COLS), F32),
               pltpu.VMEM((N_SLABS, scr_rows, LANES), F32),
               pltpu.VMEM((N_SLABS, scr_rows, LANES), F32),
               pltpu.VMEM((nb * tt, D_MODEL), BF16)]
    return pl.pallas_call(
        functools.partial(_mixer_kernel, nb=nb, tt=tt, has_state=has_state),
        grid=grid, in_specs=in_specs, out_specs=out_specs, out_shape=out_shape,
        scratch_shapes=scratch,
        compiler_params=pltpu.CompilerParams(
            dimension_semantics=("arbitrary", "arbitrary"), vmem_limit_bytes=VMEM_LIMIT),
        name="mixer_t%d" % tt,
    )(*args)


FF_CHUNK = 256


def _ffn_kernel(xa_ref, xb_ref, nffn_ref, wg_hbm, wu_hbm, wd_hbm, nfin_ref, ya_ref, yb_ref,
                act_scr, wg_bf, wu_bf, wd_bf, stage_g, stage_u, stage_d, sems, *, steps_a):
    d_ff = wg_bf.shape[1]
    n_chunks = d_ff // FF_CHUNK

    def chunk_copies(k):
        slot, cols = k % 2, slice(k * FF_CHUNK, (k + 1) * FF_CHUNK)
        return (pltpu.make_async_copy(wg_hbm.at[0, :, cols], stage_g.at[slot], sems.at[0, slot]),
                pltpu.make_async_copy(wu_hbm.at[0, :, cols], stage_u.at[slot], sems.at[1, slot]),
                pltpu.make_async_copy(wd_hbm.at[0, cols, :], stage_d.at[slot], sems.at[2, slot]))

    def tile(x_ref, y_ref, load_weights):
        if load_weights:
            for k in range(min(2, n_chunks)):
                for cp in chunk_copies(k):
                    cp.start()
        x = x_ref[...]
        xg = (x * nffn_ref[...]).astype(BF16)
        r = lax.rsqrt(jnp.mean(x * x, axis=-1, keepdims=True) + EPS)
        for k in range(n_chunks):
            cols = slice(k * FF_CHUNK, (k + 1) * FF_CHUNK)
            if load_weights:
                for cp in chunk_copies(k):
                    cp.wait()
                slot = k % 2
                wg_bf[:, cols] = stage_g[slot].astype(BF16)
                wu_bf[:, cols] = stage_u[slot].astype(BF16)
                wd_bf[cols, :] = stage_d[slot].astype(BF16)
                if k + 2 < n_chunks:
                    for cp in chunk_copies(k + 2):
                        cp.start()
            gate = r * _dot(xg, wg_bf[:, cols])
            up = r * _dot(xg, wu_bf[:, cols])
            act_scr[:, cols] = (gate * _sigmoid(gate) * up).astype(BF16)
        acc = x + _dot(act_scr[...], wd_bf[...])
        y_ref[...] = _rms(acc, nfin_ref[...])

    step = pl.program_id(0)

    @pl.when(step == 0)
    def _first_tile():
        tile(xa_ref, ya_ref, True)

    @pl.when(jnp.logical_and(step > 0, step < steps_a))
    def _first_array():
        tile(xa_ref, ya_ref, False)

    @pl.when(step >= steps_a)
    def _second_array():
        tile(xb_ref, yb_ref, False)


def _ffn(xa, xb, w, *, tm):
    steps_a, steps_b = xa.shape[0] // tm, xb.shape[0] // tm
    d_ff = w["w_gate"].shape[-1]

    def cspec(shape):
        return pl.BlockSpec(shape, lambda i: (0, 0), pipeline_mode=pl.Buffered(1))

    hbm = pl.BlockSpec(memory_space=pl.ANY)

    spec_a = pl.BlockSpec((tm, D_MODEL), lambda i: (jnp.minimum(i, steps_a - 1), 0))
    spec_b = pl.BlockSpec((tm, D_MODEL), lambda i: (jnp.maximum(i - steps_a, 0), 0))
    return pl.pallas_call(
        functools.partial(_ffn_kernel, steps_a=steps_a),
        grid=(steps_a + steps_b,),
        in_specs=[spec_a, spec_b,
                  cspec((1, D_MODEL)), hbm, hbm, hbm, cspec((1, D_MODEL))],
        out_specs=(spec_a, spec_b),
        out_shape=(jax.ShapeDtypeStruct(xa.shape, F32), jax.ShapeDtypeStruct(xb.shape, F32)),
        scratch_shapes=[pltpu.VMEM((tm, d_ff), BF16),
                        pltpu.VMEM((D_MODEL, d_ff), BF16), pltpu.VMEM((D_MODEL, d_ff), BF16),
                        pltpu.VMEM((d_ff, D_MODEL), BF16),
                        pltpu.VMEM((2, D_MODEL, FF_CHUNK), F32), pltpu.VMEM((2, D_MODEL, FF_CHUNK), F32),
                        pltpu.VMEM((2, FF_CHUNK, D_MODEL), F32),
                        pltpu.SemaphoreType.DMA((3, 2))],
        compiler_params=pltpu.CompilerParams(
            dimension_semantics=("arbitrary",), vmem_limit_bytes=VMEM_LIMIT),
        name="ffn",
    )(xa, xb, w["norm_ffn"], w["w_gate"], w["w_up"], w["w_down"], w["norm_final"])


HALF_GROUPS = S5_GROUPS // 2


def _block_diag_halves(stacks):
    rows = HALF_GROUPS * S5_GROUP
    cols = HALF_GROUPS * S5_STATE
    x = jnp.concatenate([m.reshape(2 * rows, S5_STATE) for m in stacks], axis=0).astype(BF16)
    spread = jnp.asarray(np.tile(np.eye(S5_STATE), (1, HALF_GROUPS)), dtype=BF16)
    diag = (np.arange(rows)[:, None] // S5_GROUP) == (np.arange(cols)[None, :] // S5_STATE)
    y = jnp.dot(x, spread, preferred_element_type=BF16)
    y = jnp.where(np.tile(diag, (2 * len(stacks), 1)), y, jnp.zeros((), BF16))
    return y.reshape(len(stacks), 2, rows, cols)


def _segment_mask(tt):
    r = np.arange(CHUNK)
    same = (r[:, None] // tt) == (r[None, :] // tt)
    return jnp.asarray(same & (r[:, None] >= r[None, :]), dtype=BF16)


def _prepare(lb_param, norm_mix, w_in, s5_a_re, s5_a_im, s5_log_dt, s5_b_re, s5_b_im, s5_c_re,
             s5_c_im, s5_d, s5_w_glu, hg_norm, w_out, norm_ffn, w_gate, w_up, w_down, norm_final):
    l = 0
    a_re = s5_a_re[l].astype(F32)
    a_im = s5_a_im[l].astype(F32)
    dt = jnp.exp(s5_log_dt[l].astype(F32))[:, None]
    mag = jnp.exp(a_re * dt)
    ab_re = mag * jnp.cos(a_im * dt)
    ab_im = mag * jnp.sin(a_im * dt)
    den = a_re * a_re + a_im * a_im
    nr = ab_re - 1.0
    ni = ab_im
    f_re = (nr * a_re + ni * a_im) / den
    f_im = (ni * a_re - nr * a_im) / den
    b_re = s5_b_re[l].astype(F32).transpose(0, 2, 1)
    b_im = s5_b_im[l].astype(F32).transpose(0, 2, 1)
    bb_re = f_re[:, None, :] * b_re - f_im[:, None, :] * b_im
    bb_im = f_re[:, None, :] * b_im + f_im[:, None, :] * b_re
    blk = _block_diag_halves([bb_re, bb_im, s5_c_re[l].astype(F32), -s5_c_im[l].astype(F32)])
    wb = blk[0:2]
    wc = jnp.swapaxes(blk[2:4], 2, 3)
    lb_all = jnp.cumsum(jax.nn.softmax(lb_param.astype(F32), axis=0), axis=0)
    return {
        "norm_mix": norm_mix[l].reshape(1, D_MODEL).astype(F32),
        "w_in": w_in[l].astype(BF16),
        "wb": wb,
        "a_re": ab_re.reshape(1, S5_FLAT),
        "a_im": ab_im.reshape(1, S5_FLAT),
        "wc": wc,
        "d_skip": s5_d[l].reshape(1, S5_WIDTH).astype(F32),
        "w_glu": s5_w_glu[l].astype(BF16),
        "lb": lb_all[l].reshape(1, HG_WIDTH),
        "g_norm": hg_norm[l].reshape(1, HG_DIM).astype(F32),
        "w_out": w_out[l].astype(BF16),
        "mask_64": _segment_mask(64),
        "mask_8": _segment_mask(8),
        "norm_ffn": norm_ffn[l].reshape(1, D_MODEL).astype(F32),
        "w_gate": w_gate.astype(F32),
        "w_up": w_up.astype(F32),
        "w_down": w_down.astype(F32),
        "norm_final": norm_final.reshape(1, D_MODEL).astype(F32),
    }


def _states(h_re, h_im, s_new):
    b = h_re.shape[0]
    return (h_re.reshape(1, b, S5_GROUPS, S5_STATE), h_im.reshape(1, b, S5_GROUPS, S5_STATE),
            s_new.reshape(1, b, HG_HEADS, HG_DIM, HG_DIM))


def kernel(x_prompt, x_sample, state_s5_re, state_s5_im, state_hgrn, lb_param, norm_mix, w_in,
           s5_a_re, s5_a_im, s5_log_dt, s5_b_re, s5_b_im, s5_c_re, s5_c_im, s5_d, s5_w_glu,
           hg_norm, w_out, norm_ffn, w_gate, w_up, w_down, norm_final):
    assert norm_mix.shape[0] == 1, "single-layer trunk"
    w = _prepare(lb_param, norm_mix, w_in, s5_a_re, s5_a_im, s5_log_dt, s5_b_re, s5_b_im,
                 s5_c_re, s5_c_im, s5_d, s5_w_glu, hg_norm, w_out, norm_ffn, w_gate, w_up,
                 w_down, norm_final)
    bp, tp, _ = x_prompt.shape
    bs, ts, _ = x_sample.shape
    x1_p, *st_p = _mixer(x_prompt, None, w, nb=8, tt=64)
    st = (state_s5_re[0].reshape(bs, S5_FLAT), state_s5_im[0].reshape(bs, S5_FLAT), state_hgrn)
    x1_s, *st_s = _mixer(x_sample, st, w, nb=16, tt=8)
    y_p, y_s = _ffn(x1_p.reshape(bp * tp, D_MODEL), x1_s.reshape(bs * ts, D_MODEL), w, tm=512)
    return (y_p.reshape(bp, tp, D_MODEL), y_s.reshape(bs, ts, D_MODEL),
            *_states(*st_p), *_states(*st_s))
```

```python
import functools
import math

import jax
import jax.numpy as jnp
import numpy as np
from jax import lax
from jax.experimental import pallas as pl
from jax.experimental.pallas import tpu as pltpu

F32 = jnp.float32
BF16 = jnp.bfloat16

D_MODEL = 1024
S5_WIDTH = 512
S5_GROUP = 16
S5_GROUPS = 32
S5_STATE = 64
S5_FLAT = S5_GROUPS * S5_STATE
HG_WIDTH = 512
HG_DIM = 128
HG_HEADS = 4
IN_COLS = S5_WIDTH + 4 * HG_WIDTH
EPS = 1e-6

LANES = 128
SUBLANES = 8
CHUNK = 64
N_SLABS = S5_FLAT // LANES
HALF_U = S5_WIDTH // 2
HALF_SLABS = N_SLABS // 2
SUB = 16
READ_K = 256
SCAN_COLS = 256
SCAN_PAD = 4
VMEM_LIMIT = 56 * 1024 * 1024


def _seq_pitch(tt):
    return tt + SCAN_PAD if tt == CHUNK else tt


def _dot(a, b):
    return jnp.dot(a, b, preferred_element_type=F32)


def _dot_nt(a, b):
    return lax.dot_general(a, b, (((1,), (1,)), ((), ())), preferred_element_type=F32)


def _dot_tn(a, b):
    return lax.dot_general(a, b, (((0,), (0,)), ((), ())), preferred_element_type=F32)


def _split3(x):
    hi = x.astype(BF16)
    r1 = x - hi.astype(F32)
    mid = r1.astype(BF16)
    lo = (r1 - mid.astype(F32)).astype(BF16)
    return hi, mid, lo


def _rms(x, g):
    return x * lax.rsqrt(jnp.mean(x * x, axis=-1, keepdims=True) + EPS) * g


def _sigmoid(x):
    return jax.nn.sigmoid(x)


def _gelu_tanh(x):
    c = math.sqrt(2.0 / math.pi)
    return 0.5 * x * (1.0 + jnp.tanh(c * (x + 0.044715 * (x * x * x))))


def _mixer_kernel(*refs, nb, tt, has_state):
    it = iter(refs)
    x_ref = next(it)
    if has_state:
        h0re_ref, h0im_ref, s0_ref = next(it), next(it), next(it)
    (norm_ref, win_ref, wb_ref, are_ref, aim_ref, wc_ref, dskip_ref, wglu_ref, lb_ref,
     gnorm_ref, wout_ref, mask_ref) = (next(it) for _ in range(12))
    x1_ref, hre_ref, him_ref, s_ref = (next(it) for _ in range(4))
    proj_scr, bre_scr, bim_scr, mix_scr = (next(it) for _ in range(4))

    bpc = CHUNK // tt
    nchunks = nb // bpc
    pitch = _seq_pitch(tt)
    cpitch = pitch * bpc
    ngroups = nb // SUBLANES
    state_t = tt == CHUNK
    assert not (state_t and has_state)

    tstep = pl.program_id(1)

    @pl.when(tstep == 0)
    def _init():
        if has_state:
            hre_ref[...] = h0re_ref[...]
            him_ref[...] = h0im_ref[...]
        else:
            hre_ref[...] = jnp.zeros_like(hre_ref)
            him_ref[...] = jnp.zeros_like(him_ref)
            s_ref[...] = jnp.zeros_like(s_ref)

    rows_tile = nb * tt
    scr_row0 = [c * cpitch for c in range(nchunks)]

    x_in = x_ref[...].reshape(rows_tile, D_MODEL)
    hn = (x_in * norm_ref[...]).astype(BF16)
    r_in = lax.rsqrt(jnp.mean(x_in * x_in, axis=-1, keepdims=True) + EPS)
    proj_scr[:, 0:S5_WIDTH] = r_in * _dot(hn, win_ref[:, 0:S5_WIDTH])
    ub = proj_scr[:, 0:S5_WIDTH].astype(BF16)
    for hf in range(2):
        ubh = ub[:, hf * HALF_U:(hf + 1) * HALF_U]
        for part, scr in enumerate((bre_scr, bim_scr)):
            bu = _dot(ubh, wb_ref[part, hf])
            for jj in range(HALF_SLABS):
                for c in range(nchunks):
                    scr[hf * HALF_SLABS + jj, scr_row0[c]:scr_row0[c] + CHUNK, :] = (
                        bu[c * CHUNK:(c + 1) * CHUNK, jj * LANES:(jj + 1) * LANES])

    n_rest = (IN_COLS - S5_WIDTH) // SCAN_COLS
    steps_per_iter = tt // n_rest
    for i in range(n_rest):
        n0 = S5_WIDTH + i * SCAN_COLS
        proj_scr[:, n0:n0 + SCAN_COLS] = r_in * _dot(hn, win_ref[:, n0:n0 + SCAN_COLS])
        t0 = i * steps_per_iter
        for g in range(ngroups):
            base = g * SUBLANES * pitch
            grows = slice(g * SUBLANES, (g + 1) * SUBLANES)
            for j in range(N_SLABS):
                lanes = slice(j * LANES, (j + 1) * LANES)
                a_re = jnp.broadcast_to(are_ref[:, lanes], (SUBLANES, LANES))
                a_im = jnp.broadcast_to(aim_ref[:, lanes], (SUBLANES, LANES))
                h_re = hre_ref[grows, lanes]
                h_im = him_ref[grows, lanes]
                for dt in range(steps_per_iter):
                    idx = pl.ds(base + t0 + dt, SUBLANES, stride=pitch)
                    n_re = a_re * h_re - a_im * h_im + bre_scr[j, idx, :]
                    n_im = a_re * h_im + a_im * h_re + bim_scr[j, idx, :]
                    bre_scr[j, idx, :] = n_re
                    bim_scr[j, idx, :] = n_im
                    h_re, h_im = n_re, n_im
                hre_ref[grows, lanes] = h_re
                him_ref[grows, lanes] = h_im

    ys = [None, None]
    kb_per_half = 2 * HALF_SLABS * LANES // READ_K

    def readout_piece(p):
        hf, kb = divmod(p, kb_per_half)
        part, kb = divmod(kb, kb_per_half // 2)
        scr = (bre_scr, bim_scr)[part]
        j0 = hf * HALF_SLABS + (READ_K // LANES) * kb
        piece = jnp.concatenate(
            [jnp.concatenate([scr[j0 + d, scr_row0[c]:scr_row0[c] + CHUNK, :]
                              for c in range(nchunks)], axis=0)
             for d in range(READ_K // LANES)], axis=1).astype(BF16)
        term = _dot(piece, wc_ref[part, hf, kb * READ_K:(kb + 1) * READ_K, :])
        ys[hf] = term if ys[hf] is None else ys[hf] + term

    def readout_finish():
        y = jnp.concatenate(ys, axis=1) + dskip_ref[...] * proj_scr[:, 0:S5_WIDTH]
        gact = _gelu_tanh(y)
        out5 = gact * _sigmoid(_dot(gact.astype(BF16), wglu_ref[...]))
        mix_scr[:, 0:S5_WIDTH] = out5.astype(BF16)

    n_pieces = 2 * kb_per_half

    lb = lb_ref[...]
    seg_mask = mask_ref[...].astype(F32) > 0.5
    zeros_t = jnp.zeros((tt, HG_DIM), F32)
    dec_rows = jnp.concatenate(
        [jnp.zeros((3 * tt, HG_DIM), F32), jnp.ones((3 * tt, HG_DIM), F32)], axis=1)
    heads = range(HG_HEADS)

    def chunk_stages(c):
        rows = slice(c * CHUNK, (c + 1) * CHUNK)
        q_all = proj_scr[rows, S5_WIDTH:S5_WIDTH + HG_WIDTH]
        fz = proj_scr[rows, S5_WIDTH + HG_WIDTH:S5_WIDTH + 2 * HG_WIDTH]
        iv_all = proj_scr[rows, S5_WIDTH + 2 * HG_WIDTH:S5_WIDTH + 3 * HG_WIDTH]
        og_all = proj_scr[rows, S5_WIDTH + 3 * HG_WIDTH:S5_WIDTH + 4 * HG_WIDTH]
        f = lb + (1.0 - lb) * _sigmoid(fz)
        lf_all = jnp.log(f)
        k_all = 1.0 - f
        lf_pieces = _split3(lf_all)
        g_all = sum(_dot(mask_ref[...], p) for p in lf_pieces)
        vb_all = iv_all.astype(BF16)
        qg_all = (q_all * jnp.exp(g_all)).astype(BF16)
        hs = [slice(h * HG_DIM, (h + 1) * HG_DIM) for h in heads]

        qs, ks = [], []
        for h in heads:
            gh, qh, kh = g_all[:, hs[h]], q_all[:, hs[h]], k_all[:, hs[h]]
            if tt == CHUNK:
                nblk = CHUNK // SUB
                blocks = [slice(b * SUB, (b + 1) * SUB) for b in range(nblk)]
                refs = [gh[b * SUB + SUB // 2:b * SUB + SUB // 2 + 1, :] for b in range(nblk)]
                ref_rows = jnp.concatenate(
                    [jnp.broadcast_to(r, (SUB, HG_DIM)) for r in refs], axis=0)
                q_loc = (qh * jnp.exp(gh - ref_rows)).astype(BF16)
                k_loc = kh * jnp.exp(ref_rows - gh)
                no_keys = jnp.zeros((SUB, HG_DIM), BF16)
                for i in range(nblk):
                    qs.append(q_loc[blocks[i]])
                    parts = [(k_loc[blocks[j]] * jnp.exp(refs[i] - refs[j])).astype(BF16)
                             for j in range(i)]
                    parts.append(k_loc[blocks[i]].astype(BF16))
                    parts += [no_keys] * (nblk - 1 - i)
                    ks.append(jnp.concatenate(parts, axis=0))
            else:
                g3 = gh.reshape(bpc, tt, HG_DIM)
                ref3 = jnp.broadcast_to(g3[:, tt // 2:tt // 2 + 1, :], (bpc, tt, HG_DIM))
                ref_rows = ref3.reshape(CHUNK, HG_DIM)
                qs.append((qh * jnp.exp(gh - ref_rows)).astype(BF16))
                ks.append((kh * jnp.exp(ref_rows - gh)).astype(BF16))

        yield
        att = [_dot_nt(a, b) for a, b in zip(qs, ks)]

        yield
        o_inter = []
        for h in heads:
            gh, kh = g_all[:, hs[h]], k_all[:, hs[h]]
            oi = []
            for bb in range(bpc):
                rs = slice(bb * tt, (bb + 1) * tt)
                b_idx = c * bpc + bb
                s_old = s0_ref[b_idx, h] if has_state else s_ref[b_idx, h]
                g_last = gh[bb * tt + tt - 1:bb * tt + tt, :]
                kd = kh[rs] * jnp.exp(g_last - gh[rs])
                if state_t:
                    oi.append(_dot_nt(qg_all[rs, hs[h]], s_old.astype(BF16)))
                    s_ref[b_idx, h] = (jnp.exp(g_last) * s_old
                                       + _dot_tn(vb_all[rs, hs[h]], kd.astype(BF16)))
                else:
                    oi.append(_dot(qg_all[rs, hs[h]], s_old.astype(BF16)))
                    lhs = jnp.concatenate(
                        [kd] + [p[rs, hs[h]].astype(F32) for p in lf_pieces], axis=0)
                    rhs = jnp.concatenate(
                        [jnp.concatenate([iv_all[rs, hs[h]], zeros_t], axis=1), dec_rows], axis=0)
                    upd = _dot_tn(lhs.astype(BF16), rhs.astype(BF16))
                    s_ref[b_idx, h] = jnp.exp(upd[:, HG_DIM:]) * s_old + upd[:, :HG_DIM]
            o_inter.append(oi[0] if bpc == 1 else jnp.concatenate(oi, axis=0))

        yield
        nblk = len(att) // HG_HEADS
        for h in heads:
            sc = att[h * nblk:(h + 1) * nblk]
            sc = sc[0] if nblk == 1 else jnp.concatenate(sc, axis=0)
            sc = jnp.where(seg_mask, sc, 0.0).astype(BF16)
            o = _dot(sc, vb_all[:, hs[h]]) + o_inter[h]
            o = _rms(o, gnorm_ref[...])
            ogh = og_all[:, hs[h]]
            mix_scr[rows, S5_WIDTH + h * HG_DIM:S5_WIDTH + (h + 1) * HG_DIM] = (
                o * (ogh * _sigmoid(ogh))).astype(BF16)

    per_chunk = n_pieces // nchunks
    gens = [chunk_stages(c) for c in range(nchunks)]
    next(gens[0])
    for c in range(nchunks):
        pieces = range(c * per_chunk, (c + 1) * per_chunk)
        next(gens[c])
        for p in pieces[:per_chunk // 2]:
            readout_piece(p)
        if c + 1 < nchunks:
            next(gens[c + 1])
        next(gens[c])
        for p in pieces[per_chunk // 2:]:
            readout_piece(p)
        for _ in gens[c]:
            pass
    readout_finish()

    if state_t:
        @pl.when(tstep == pl.num_programs(1) - 1)
        def _untranspose_state():
            for b in range(nb):
                for h in heads:
                    s_ref[b, h] = s_ref[b, h].T

    x1 = x_ref[...].reshape(rows_tile, D_MODEL) + _dot(mix_scr[...], wout_ref[...])
    x1_ref[...] = x1.reshape(nb, tt, D_MODEL)


def _const_spec(shape):
    nd = len(shape)
    return pl.BlockSpec(shape, lambda i, j, _nd=nd: (0,) * _nd, pipeline_mode=pl.Buffered(1))


def _mixer(x, state, w, *, nb, tt):
    B, T, _ = x.shape
    has_state = state is not None
    grid = (B // nb, T // tt)
    bpc = CHUNK // tt
    nchunks = nb // bpc
    pitch = _seq_pitch(tt)
    scr_rows = nchunks * pitch * bpc

    consts = [w["norm_mix"], w["w_in"], w["wb"], w["a_re"], w["a_im"], w["wc"], w["d_skip"],
              w["w_glu"], w["lb"], w["g_norm"], w["w_out"], w["mask_" + str(tt)]]
    in_specs = [pl.BlockSpec((nb, tt, D_MODEL), lambda i, j: (i, j, 0))]
    args = [x]
    if has_state:
        in_specs += [pl.BlockSpec((nb, S5_FLAT), lambda i, j: (i, 0)),
                     pl.BlockSpec((nb, S5_FLAT), lambda i, j: (i, 0)),
                     pl.BlockSpec((None, nb, HG_HEADS, HG_DIM, HG_DIM),
                                  lambda i, j: (0, i, 0, 0, 0))]
        args += list(state)
    in_specs += [_const_spec(c.shape) for c in consts]
    args += consts

    out_shape = (jax.ShapeDtypeStruct((B, T, D_MODEL), F32),
                 jax.ShapeDtypeStruct((B, S5_FLAT), F32),
                 jax.ShapeDtypeStruct((B, S5_FLAT), F32),
                 jax.ShapeDtypeStruct((B, HG_HEADS, HG_DIM, HG_DIM), F32))
    out_specs = (pl.BlockSpec((nb, tt, D_MODEL), lambda i, j: (i, j, 0)),
                 pl.BlockSpec((nb, S5_FLAT), lambda i, j: (i, 0)),
                 pl.BlockSpec((nb, S5_FLAT), lambda i, j: (i, 0)),
                 pl.BlockSpec((nb, HG_HEADS, HG_DIM, HG_DIM), lambda i, j: (i, 0, 0, 0)))
    scratch = [pltpu.VMEM((nb * tt, IN_COLS), F32),
               pltpu.VMEM((N_SLABS, scr_rows, LANES), F32),
               pltpu.VMEM((N_SLABS, scr_rows, LANES), F32),
               pltpu.VMEM((nb * tt, D_MODEL), BF16)]
    return pl.pallas_call(
        functools.partial(_mixer_kernel, nb=nb, tt=tt, has_state=has_state),
        grid=grid, in_specs=in_specs, out_specs=out_specs, out_shape=out_shape,
        scratch_shapes=scratch,
        compiler_params=pltpu.CompilerParams(
            dimension_semantics=("arbitrary", "arbitrary"), vmem_limit_bytes=VMEM_LIMIT),
        name="mixer_t%d" % tt,
    )(*args)


FF_CHUNK = 256


W_BLOCK_CHUNKS = 3


def _ffn_kernel(xa_ref, xb_ref, nffn_ref, wg_hbm, wu_hbm, wd_hbm, nfin_ref, ya_ref, yb_ref,
                act_scr, wg_ref, wu_ref, wd_ref, sems, *, steps_a):
    d_ff = wg_ref.shape[1]
    n_chunks = d_ff // FF_CHUNK
    starts = list(range(0, n_chunks, W_BLOCK_CHUNKS))
    blocks = [slice(c0 * FF_CHUNK, min(c0 + W_BLOCK_CHUNKS, n_chunks) * FF_CHUNK) for c0 in starts]

    def block_copies(b):
        cols = blocks[b]
        return (pltpu.make_async_copy(wg_hbm.at[0, :, cols], wg_ref.at[:, cols], sems.at[0, b]),
                pltpu.make_async_copy(wu_hbm.at[0, :, cols], wu_ref.at[:, cols], sems.at[1, b]),
                pltpu.make_async_copy(wd_hbm.at[0, cols, :], wd_ref.at[cols, :], sems.at[2, b]))

    def tile(x_ref, y_ref, load_weights):
        if load_weights:
            for b in range(len(blocks)):
                for cp in block_copies(b):
                    cp.start()
        x = x_ref[...]
        xg = (x * nffn_ref[...]).astype(BF16)
        r = lax.rsqrt(jnp.mean(x * x, axis=-1, keepdims=True) + EPS)
        for k in range(n_chunks):
            cols = slice(k * FF_CHUNK, (k + 1) * FF_CHUNK)
            if load_weights and k in starts:
                gate_cp, up_cp, _ = block_copies(starts.index(k))
                gate_cp.wait()
                up_cp.wait()
            gate = r * _dot(xg, wg_ref[:, cols].astype(BF16))
            up = r * _dot(xg, wu_ref[:, cols].astype(BF16))
            act_scr[:, cols] = (gate * _sigmoid(gate) * up).astype(BF16)
        if load_weights:
            for b in range(len(blocks)):
                block_copies(b)[2].wait()
        acc = x + _dot(act_scr[...], wd_ref[...].astype(BF16))
        y_ref[...] = _rms(acc, nfin_ref[...])

    step = pl.program_id(0)

    @pl.when(step == 0)
    def _first_tile():
        tile(xa_ref, ya_ref, True)

    @pl.when(jnp.logical_and(step > 0, step < steps_a))
    def _first_array():
        tile(xa_ref, ya_ref, False)

    @pl.when(step >= steps_a)
    def _second_array():
        tile(xb_ref, yb_ref, False)


def _ffn(xa, xb, w, *, tm):
    steps_a, steps_b = xa.shape[0] // tm, xb.shape[0] // tm
    d_ff = w["w_gate"].shape[-1]

    def cspec(shape):
        return pl.BlockSpec(shape, lambda i: (0, 0), pipeline_mode=pl.Buffered(1))

    hbm = pl.BlockSpec(memory_space=pl.ANY)

    spec_a = pl.BlockSpec((tm, D_MODEL), lambda i: (jnp.minimum(i, steps_a - 1), 0))
    spec_b = pl.BlockSpec((tm, D_MODEL), lambda i: (jnp.maximum(i - steps_a, 0), 0))
    return pl.pallas_call(
        functools.partial(_ffn_kernel, steps_a=steps_a),
        grid=(steps_a + steps_b,),
        in_specs=[spec_a, spec_b,
                  cspec((1, D_MODEL)), hbm, hbm, hbm, cspec((1, D_MODEL))],
        out_specs=(spec_a, spec_b),
        out_shape=(jax.ShapeDtypeStruct(xa.shape, F32), jax.ShapeDtypeStruct(xb.shape, F32)),
        scratch_shapes=[pltpu.VMEM((tm, d_ff), BF16),
                        pltpu.VMEM((D_MODEL, d_ff), F32), pltpu.VMEM((D_MODEL, d_ff), F32),
                        pltpu.VMEM((d_ff, D_MODEL), F32),
                        pltpu.SemaphoreType.DMA((3, -(-d_ff // (FF_CHUNK * W_BLOCK_CHUNKS))))],
        compiler_params=pltpu.CompilerParams(
            dimension_semantics=("arbitrary",), vmem_limit_bytes=VMEM_LIMIT),
        name="ffn",
    )(xa, xb, w["norm_ffn"], w["w_gate"], w["w_up"], w["w_down"], w["norm_final"])


HALF_GROUPS = S5_GROUPS // 2


def _block_diag_halves(stacks):
    rows = HALF_GROUPS * S5_GROUP
    cols = HALF_GROUPS * S5_STATE
    x = jnp.concatenate([m.reshape(2 * rows, S5_STATE) for m in stacks], axis=0).astype(BF16)
    spread = jnp.asarray(np.tile(np.eye(S5_STATE), (1, HALF_GROUPS)), dtype=BF16)
    diag = (np.arange(rows)[:, None] // S5_GROUP) == (np.arange(cols)[None, :] // S5_STATE)
    y = jnp.dot(x, spread, preferred_element_type=BF16)
    y = jnp.where(np.tile(diag, (2 * len(stacks), 1)), y, jnp.zeros((), BF16))
    return y.reshape(len(stacks), 2, rows, cols)


def _segment_mask(tt):
    r = np.arange(CHUNK)
    same = (r[:, None] // tt) == (r[None, :] // tt)
    return jnp.asarray(same & (r[:, None] >= r[None, :]), dtype=BF16)


def _prepare(lb_param, norm_mix, w_in, s5_a_re, s5_a_im, s5_log_dt, s5_b_re, s5_b_im, s5_c_re,
             s5_c_im, s5_d, s5_w_glu, hg_norm, w_out, norm_ffn, w_gate, w_up, w_down, norm_final):
    l = 0
    a_re = s5_a_re[l].astype(F32)
    a_im = s5_a_im[l].astype(F32)
    dt = jnp.exp(s5_log_dt[l].astype(F32))[:, None]
    mag = jnp.exp(a_re * dt)
    ab_re = mag * jnp.cos(a_im * dt)
    ab_im = mag * jnp.sin(a_im * dt)
    den = a_re * a_re + a_im * a_im
    nr = ab_re - 1.0
    ni = ab_im
    f_re = (nr * a_re + ni * a_im) / den
    f_im = (ni * a_re - nr * a_im) / den
    b_re = s5_b_re[l].astype(F32).transpose(0, 2, 1)
    b_im = s5_b_im[l].astype(F32).transpose(0, 2, 1)
    bb_re = f_re[:, None, :] * b_re - f_im[:, None, :] * b_im
    bb_im = f_re[:, None, :] * b_im + f_im[:, None, :] * b_re
    blk = _block_diag_halves([bb_re, bb_im, s5_c_re[l].astype(F32), -s5_c_im[l].astype(F32)])
    wb = blk[0:2]
    wc = jnp.swapaxes(blk[2:4], 2, 3)
    lb_all = jnp.cumsum(jax.nn.softmax(lb_param.astype(F32), axis=0), axis=0)
    return {
        "norm_mix": norm_mix[l].reshape(1, D_MODEL).astype(F32),
        "w_in": w_in[l].astype(BF16),
        "wb": wb,
        "a_re": ab_re.reshape(1, S5_FLAT),
        "a_im": ab_im.reshape(1, S5_FLAT),
        "wc": wc,
        "d_skip": s5_d[l].reshape(1, S5_WIDTH).astype(F32),
        "w_glu": s5_w_glu[l].astype(BF16),
        "lb": lb_all[l].reshape(1, HG_WIDTH),
        "g_norm": hg_norm[l].reshape(1, HG_DIM).astype(F32),
        "w_out": w_out[l].astype(BF16),
        "mask_64": _segment_mask(64),
        "mask_8": _segment_mask(8),
        "norm_ffn": norm_ffn[l].reshape(1, D_MODEL).astype(F32),
        "w_gate": w_gate.astype(F32),
        "w_up": w_up.astype(F32),
        "w_down": w_down.astype(F32),
        "norm_final": norm_final.reshape(1, D_MODEL).astype(F32),
    }


def _states(h_re, h_im, s_new):
    b = h_re.shape[0]
    return (h_re.reshape(1, b, S5_GROUPS, S5_STATE), h_im.reshape(1, b, S5_GROUPS, S5_STATE),
            s_new.reshape(1, b, HG_HEADS, HG_DIM, HG_DIM))


def kernel(x_prompt, x_sample, state_s5_re, state_s5_im, state_hgrn, lb_param, norm_mix, w_in,
           s5_a_re, s5_a_im, s5_log_dt, s5_b_re, s5_b_im, s5_c_re, s5_c_im, s5_d, s5_w_glu,
           hg_norm, w_out, norm_ffn, w_gate, w_up, w_down, norm_final):
    assert norm_mix.shape[0] == 1, "single-layer trunk"
    w = _prepare(lb_param, norm_mix, w_in, s5_a_re, s5_a_im, s5_log_dt, s5_b_re, s5_b_im,
                 s5_c_re, s5_c_im, s5_d, s5_w_glu, hg_norm, w_out, norm_ffn, w_gate, w_up,
                 w_down, norm_final)
    bp, tp, _ = x_prompt.shape
    bs, ts, _ = x_sample.shape
    x1_p, *st_p = _mixer(x_prompt, None, w, nb=8, tt=64)
    st = (state_s5_re[0].reshape(bs, S5_FLAT), state_s5_im[0].reshape(bs, S5_FLAT), state_hgrn)
    x1_s, *st_s = _mixer(x_sample, st, w, nb=16, tt=8)
    y_p, y_s = _ffn(x1_p.reshape(bp * tp, D_MODEL), x1_s.reshape(bs * ts, D_MODEL), w, tm=512)
    return (y_p.reshape(bp, tp, D_MODEL), y_s.reshape(bs, ts, D_MODEL),
            *_states(*st_p), *_states(*st_s))
```

```python
import functools
import math

import jax
import jax.numpy as jnp
import numpy as np
from jax import lax
from jax.experimental import pallas as pl
from jax.experimental.pallas import tpu as pltpu

F32 = jnp.float32
BF16 = jnp.bfloat16

D_MODEL = 1024
S5_WIDTH = 512
S5_GROUP = 16
S5_GROUPS = 32
S5_STATE = 64
S5_FLAT = S5_GROUPS * S5_STATE
HG_WIDTH = 512
HG_DIM = 128
HG_HEADS = 4
IN_COLS = S5_WIDTH + 4 * HG_WIDTH
EPS = 1e-6

LANES = 128
SUBLANES = 8
CHUNK = 64
N_SLABS = S5_FLAT // LANES
HALF_U = S5_WIDTH // 2
HALF_SLABS = N_SLABS // 2
SUB = 16
DOWN_CAST_STEPS = 22
READ_K = 256
SCAN_COLS = 256
SCAN_PAD = 4
VMEM_LIMIT = 56 * 1024 * 1024


def _seq_pitch(tt):
    return tt + SCAN_PAD if tt == CHUNK else tt


def _dot(a, b):
    return jnp.dot(a, b, preferred_element_type=F32)


def _dot_nt(a, b):
    return lax.dot_general(a, b, (((1,), (1,)), ((), ())), preferred_element_type=F32)


def _dot_tn(a, b):
    return lax.dot_general(a, b, (((0,), (0,)), ((), ())), preferred_element_type=F32)


def _split3(x):
    hi = x.astype(BF16)
    r1 = x - hi.astype(F32)
    mid = r1.astype(BF16)
    lo = (r1 - mid.astype(F32)).astype(BF16)
    return hi, mid, lo


def _rms(x, g):
    return x * lax.rsqrt(jnp.mean(x * x, axis=-1, keepdims=True) + EPS) * g


def _sigmoid(x):
    return jax.nn.sigmoid(x)


def _gelu_tanh(x):
    c = math.sqrt(2.0 / math.pi)
    return 0.5 * x * (1.0 + jnp.tanh(c * (x + 0.044715 * (x * x * x))))


def _mixer_kernel(*refs, nb, tt, has_state, cast_ffn=False):
    it = iter(refs)
    x_ref = next(it)
    if has_state:
        h0re_ref, h0im_ref, s0_ref = next(it), next(it), next(it)
    (norm_ref, win_ref, wb_ref, are_ref, aim_ref, wc_ref, dskip_ref, wglu_ref, lb_ref,
     gnorm_ref, wout_ref, mask_ref) = (next(it) for _ in range(12))
    cast_in = tuple(next(it) for _ in range(3)) if cast_ffn else ()
    x1_ref, hre_ref, him_ref, s_ref = (next(it) for _ in range(4))
    cast_out = tuple(next(it) for _ in range(3)) if cast_ffn else ()
    proj_scr, bre_scr, bim_scr, mix_scr = (next(it) for _ in range(4))

    bpc = CHUNK // tt
    nchunks = nb // bpc
    pitch = _seq_pitch(tt)
    cpitch = pitch * bpc
    ngroups = nb // SUBLANES
    state_t = tt == CHUNK
    assert not (state_t and has_state)

    tstep = pl.program_id(1)

    @pl.when(tstep == 0)
    def _init():
        if has_state:
            hre_ref[...] = h0re_ref[...]
            him_ref[...] = h0im_ref[...]
        else:
            hre_ref[...] = jnp.zeros_like(hre_ref)
            him_ref[...] = jnp.zeros_like(him_ref)
            s_ref[...] = jnp.zeros_like(s_ref)

    rows_tile = nb * tt
    scr_row0 = [c * cpitch for c in range(nchunks)]

    x_in = x_ref[...].reshape(rows_tile, D_MODEL)
    hn = (x_in * norm_ref[...]).astype(BF16)
    r_in = lax.rsqrt(jnp.mean(x_in * x_in, axis=-1, keepdims=True) + EPS)
    proj_scr[:, 0:S5_WIDTH] = r_in * _dot(hn, win_ref[:, 0:S5_WIDTH])
    ub = proj_scr[:, 0:S5_WIDTH].astype(BF16)
    for hf in range(2):
        ubh = ub[:, hf * HALF_U:(hf + 1) * HALF_U]
        for part, scr in enumerate((bre_scr, bim_scr)):
            bu = _dot(ubh, wb_ref[part, hf])
            for jj in range(HALF_SLABS):
                for c in range(nchunks):
                    scr[hf * HALF_SLABS + jj, scr_row0[c]:scr_row0[c] + CHUNK, :] = (
                        bu[c * CHUNK:(c + 1) * CHUNK, jj * LANES:(jj + 1) * LANES])

    n_rest = (IN_COLS - S5_WIDTH) // SCAN_COLS
    steps_per_iter = tt // n_rest
    for i in range(n_rest):
        n0 = S5_WIDTH + i * SCAN_COLS
        proj_scr[:, n0:n0 + SCAN_COLS] = r_in * _dot(hn, win_ref[:, n0:n0 + SCAN_COLS])
        t0 = i * steps_per_iter
        for g in range(ngroups):
            base = g * SUBLANES * pitch
            grows = slice(g * SUBLANES, (g + 1) * SUBLANES)
            for j in range(N_SLABS):
                lanes = slice(j * LANES, (j + 1) * LANES)
                a_re = jnp.broadcast_to(are_ref[:, lanes], (SUBLANES, LANES))
                a_im = jnp.broadcast_to(aim_ref[:, lanes], (SUBLANES, LANES))
                h_re = hre_ref[grows, lanes]
                h_im = him_ref[grows, lanes]
                for dt in range(steps_per_iter):
                    idx = pl.ds(base + t0 + dt, SUBLANES, stride=pitch)
                    n_re = a_re * h_re - a_im * h_im + bre_scr[j, idx, :]
                    n_im = a_re * h_im + a_im * h_re + bim_scr[j, idx, :]
                    bre_scr[j, idx, :] = n_re
                    bim_scr[j, idx, :] = n_im
                    h_re, h_im = n_re, n_im
                hre_ref[grows, lanes] = h_re
                him_ref[grows, lanes] = h_im

    ys = [None, None]
    kb_per_half = 2 * HALF_SLABS * LANES // READ_K

    def readout_piece(p):
        hf, kb = divmod(p, kb_per_half)
        part, kb = divmod(kb, kb_per_half // 2)
        scr = (bre_scr, bim_scr)[part]
        j0 = hf * HALF_SLABS + (READ_K // LANES) * kb
        piece = jnp.concatenate(
            [jnp.concatenate([scr[j0 + d, scr_row0[c]:scr_row0[c] + CHUNK, :]
                              for c in range(nchunks)], axis=0)
             for d in range(READ_K // LANES)], axis=1).astype(BF16)
        term = _dot(piece, wc_ref[part, hf, kb * READ_K:(kb + 1) * READ_K, :])
        ys[hf] = term if ys[hf] is None else ys[hf] + term

    def readout_finish():
        y = jnp.concatenate(ys, axis=1) + dskip_ref[...] * proj_scr[:, 0:S5_WIDTH]
        gact = _gelu_tanh(y)
        out5 = gact * _sigmoid(_dot(gact.astype(BF16), wglu_ref[...]))
        mix_scr[:, 0:S5_WIDTH] = out5.astype(BF16)

    n_pieces = 2 * kb_per_half

    lb = lb_ref[...]
    seg_mask = mask_ref[...].astype(F32) > 0.5
    zeros_t = jnp.zeros((tt, HG_DIM), F32)
    dec_rows = jnp.concatenate(
        [jnp.zeros((3 * tt, HG_DIM), F32), jnp.ones((3 * tt, HG_DIM), F32)], axis=1)
    heads = range(HG_HEADS)

    def chunk_stages(c):
        rows = slice(c * CHUNK, (c + 1) * CHUNK)
        q_all = proj_scr[rows, S5_WIDTH:S5_WIDTH + HG_WIDTH]
        fz = proj_scr[rows, S5_WIDTH + HG_WIDTH:S5_WIDTH + 2 * HG_WIDTH]
        iv_all = proj_scr[rows, S5_WIDTH + 2 * HG_WIDTH:S5_WIDTH + 3 * HG_WIDTH]
        og_all = proj_scr[rows, S5_WIDTH + 3 * HG_WIDTH:S5_WIDTH + 4 * HG_WIDTH]
        f = lb + (1.0 - lb) * _sigmoid(fz)
        lf_all = jnp.log(f)
        k_all = 1.0 - f
        lf_pieces = _split3(lf_all)
        g_all = sum(_dot(mask_ref[...], p) for p in lf_pieces)
        vb_all = iv_all.astype(BF16)
        qg_all = (q_all * jnp.exp(g_all)).astype(BF16)
        hs = [slice(h * HG_DIM, (h + 1) * HG_DIM) for h in heads]

        qs, ks = [], []
        for h in heads:
            gh, qh, kh = g_all[:, hs[h]], q_all[:, hs[h]], k_all[:, hs[h]]
            if tt == CHUNK:
                nblk = CHUNK // SUB
                blocks = [slice(b * SUB, (b + 1) * SUB) for b in range(nblk)]
                refs = [gh[b * SUB + SUB // 2:b * SUB + SUB // 2 + 1, :] for b in range(nblk)]
                ref_rows = jnp.concatenate(
                    [jnp.broadcast_to(r, (SUB, HG_DIM)) for r in refs], axis=0)
                q_loc = (qh * jnp.exp(gh - ref_rows)).astype(BF16)
                k_loc = kh * jnp.exp(ref_rows - gh)
                no_keys = jnp.zeros((SUB, HG_DIM), BF16)
                for i in range(nblk):
                    qs.append(q_loc[blocks[i]])
                    parts = [(k_loc[blocks[j]] * jnp.exp(refs[i] - refs[j])).astype(BF16)
                             for j in range(i)]
                    parts.append(k_loc[blocks[i]].astype(BF16))
                    parts += [no_keys] * (nblk - 1 - i)
                    ks.append(jnp.concatenate(parts, axis=0))
            else:
                g3 = gh.reshape(bpc, tt, HG_DIM)
                ref3 = jnp.broadcast_to(g3[:, tt // 2:tt // 2 + 1, :], (bpc, tt, HG_DIM))
                ref_rows = ref3.reshape(CHUNK, HG_DIM)
                qs.append((qh * jnp.exp(gh - ref_rows)).astype(BF16))
                ks.append((kh * jnp.exp(ref_rows - gh)).astype(BF16))

        yield
        att = [_dot_nt(a, b) for a, b in zip(qs, ks)]

        yield
        o_inter = []
        for h in heads:
            gh, kh = g_all[:, hs[h]], k_all[:, hs[h]]
            oi = []
            for bb in range(bpc):
                rs = slice(bb * tt, (bb + 1) * tt)
                b_idx = c * bpc + bb
                s_old = s0_ref[b_idx, h] if has_state else s_ref[b_idx, h]
                g_last = gh[bb * tt + tt - 1:bb * tt + tt, :]
                kd = kh[rs] * jnp.exp(g_last - gh[rs])
                if state_t:
                    oi.append(_dot_nt(qg_all[rs, hs[h]], s_old.astype(BF16)))
                    s_ref[b_idx, h] = (jnp.exp(g_last) * s_old
                                       + _dot_tn(vb_all[rs, hs[h]], kd.astype(BF16)))
                else:
                    oi.append(_dot(qg_all[rs, hs[h]], s_old.astype(BF16)))
                    lhs = jnp.concatenate(
                        [kd] + [p[rs, hs[h]].astype(F32) for p in lf_pieces], axis=0)
                    rhs = jnp.concatenate(
                        [jnp.concatenate([iv_all[rs, hs[h]], zeros_t], axis=1), dec_rows], axis=0)
                    upd = _dot_tn(lhs.astype(BF16), rhs.astype(BF16))
                    s_ref[b_idx, h] = jnp.exp(upd[:, HG_DIM:]) * s_old + upd[:, :HG_DIM]
            o_inter.append(oi[0] if bpc == 1 else jnp.concatenate(oi, axis=0))

        yield
        nblk = len(att) // HG_HEADS
        for h in heads:
            sc = att[h * nblk:(h + 1) * nblk]
            sc = sc[0] if nblk == 1 else jnp.concatenate(sc, axis=0)
            sc = jnp.where(seg_mask, sc, 0.0).astype(BF16)
            o = _dot(sc, vb_all[:, hs[h]]) + o_inter[h]
            o = _rms(o, gnorm_ref[...])
            ogh = og_all[:, hs[h]]
            mix_scr[rows, S5_WIDTH + h * HG_DIM:S5_WIDTH + (h + 1) * HG_DIM] = (
                o * (ogh * _sigmoid(ogh))).astype(BF16)

    per_chunk = n_pieces // nchunks
    gens = [chunk_stages(c) for c in range(nchunks)]
    next(gens[0])
    for c in range(nchunks):
        pieces = range(c * per_chunk, (c + 1) * per_chunk)
        next(gens[c])
        for p in pieces[:per_chunk // 2]:
            readout_piece(p)
        if c + 1 < nchunks:
            next(gens[c + 1])
        next(gens[c])
        for p in pieces[per_chunk // 2:]:
            readout_piece(p)
        for _ in gens[c]:
            pass
    readout_finish()

    if state_t:
        @pl.when(tstep == pl.num_programs(1) - 1)
        def _untranspose_state():
            for b in range(nb):
                for h in heads:
                    s_ref[b, h] = s_ref[b, h].T

    if cast_ffn:
        for src, dst in zip(cast_in[:2], cast_out[:2]):
            dst[...] = src[...].astype(BF16)

        @pl.when(tstep < DOWN_CAST_STEPS)
        def _cast_down():
            cast_out[2][...] = cast_in[2][...].astype(BF16)

    x1 = x_ref[...].reshape(rows_tile, D_MODEL) + _dot(mix_scr[...], wout_ref[...])
    x1_ref[...] = x1.reshape(nb, tt, D_MODEL)


def _const_spec(shape):
    nd = len(shape)
    return pl.BlockSpec(shape, lambda i, j, _nd=nd: (0,) * _nd, pipeline_mode=pl.Buffered(1))


DOWN_CAST_ROWS = 128


def _mixer(x, state, w, *, nb, tt, ffn_weights=None):
    B, T, _ = x.shape
    has_state = state is not None
    grid = (B // nb, T // tt)
    bpc = CHUNK // tt
    nchunks = nb // bpc
    pitch = _seq_pitch(tt)
    scr_rows = nchunks * pitch * bpc

    consts = [w["norm_mix"], w["w_in"], w["wb"], w["a_re"], w["a_im"], w["wc"], w["d_skip"],
              w["w_glu"], w["lb"], w["g_norm"], w["w_out"], w["mask_" + str(tt)]]
    in_specs = [pl.BlockSpec((nb, tt, D_MODEL), lambda i, j: (i, j, 0))]
    args = [x]
    if has_state:
        in_specs += [pl.BlockSpec((nb, S5_FLAT), lambda i, j: (i, 0)),
                     pl.BlockSpec((nb, S5_FLAT), lambda i, j: (i, 0)),
                     pl.BlockSpec((None, nb, HG_HEADS, HG_DIM, HG_DIM),
                                  lambda i, j: (0, i, 0, 0, 0))]
        args += list(state)
    in_specs += [_const_spec(c.shape) for c in consts]
    args += consts
    cast_ffn = ffn_weights is not None
    cast_specs, cast_shapes = [], []
    if cast_ffn:
        w_gate, w_up, w_down = ffn_weights
        n_t = T // tt
        d_ff = w_gate.shape[-1]
        up_rows = D_MODEL // n_t
        down_steps = d_ff // DOWN_CAST_ROWS
        assert D_MODEL % n_t == 0 and up_rows % 16 == 0 and d_ff % DOWN_CAST_ROWS == 0
        assert down_steps <= n_t and down_steps == DOWN_CAST_STEPS
        cast_specs = [pl.BlockSpec((None, up_rows, d_ff), lambda i, j: (0, j, 0)),
                      pl.BlockSpec((None, up_rows, d_ff), lambda i, j: (0, j, 0)),
                      pl.BlockSpec((None, DOWN_CAST_ROWS, D_MODEL),
                                   lambda i, j: (0, jnp.minimum(j, down_steps - 1), 0))]
        cast_shapes = [jax.ShapeDtypeStruct(a.shape, BF16) for a in ffn_weights]
        in_specs += cast_specs
        args += list(ffn_weights)

    out_shape = (jax.ShapeDtypeStruct((B, T, D_MODEL), F32),
                 jax.ShapeDtypeStruct((B, S5_FLAT), F32),
                 jax.ShapeDtypeStruct((B, S5_FLAT), F32),
                 jax.ShapeDtypeStruct((B, HG_HEADS, HG_DIM, HG_DIM), F32))
    out_specs = (pl.BlockSpec((nb, tt, D_MODEL), lambda i, j: (i, j, 0)),
                 pl.BlockSpec((nb, S5_FLAT), lambda i, j: (i, 0)),
                 pl.BlockSpec((nb, S5_FLAT), lambda i, j: (i, 0)),
                 pl.BlockSpec((nb, HG_HEADS, HG_DIM, HG_DIM), lambda i, j: (i, 0, 0, 0)))
    out_shape += tuple(cast_shapes)
    out_specs += tuple(cast_specs)
    scratch = [pltpu.VMEM((nb * tt, IN_COLS), F32),
               pltpu.VMEM((N_SLABS, scr_rows, LANES), F32),
               pltpu.VMEM((N_SLABS, scr_rows, LANES), F32),
               pltpu.VMEM((nb * tt, D_MODEL), BF16)]
    return pl.pallas_call(
        functools.partial(_mixer_kernel, nb=nb, tt=tt, has_state=has_state, cast_ffn=cast_ffn),
        grid=grid, in_specs=in_specs, out_specs=out_specs, out_shape=out_shape,
        scratch_shapes=scratch,
        compiler_params=pltpu.CompilerParams(
            dimension_semantics=("arbitrary", "arbitrary"), vmem_limit_bytes=VMEM_LIMIT),
        name="mixer_t%d" % tt,
    )(*args)


FF_CHUNK = 256


def _ffn_kernel(xa_ref, xb_ref, nffn_ref, wg_ref, wu_ref, wd_ref, nfin_ref, ya_ref, yb_ref,
                act_scr, *, steps_a):
    def tile(x_ref, y_ref):
        x = x_ref[...]
        xg = (x * nffn_ref[...]).astype(BF16)
        r = lax.rsqrt(jnp.mean(x * x, axis=-1, keepdims=True) + EPS)
        d_ff = wg_ref.shape[1]
        for f0 in range(0, d_ff, FF_CHUNK):
            gate = r * _dot(xg, wg_ref[:, f0:f0 + FF_CHUNK])
            up = r * _dot(xg, wu_ref[:, f0:f0 + FF_CHUNK])
            act_scr[:, f0:f0 + FF_CHUNK] = (gate * _sigmoid(gate) * up).astype(BF16)
        acc = x + _dot(act_scr[...], wd_ref[...])
        y_ref[...] = _rms(acc, nfin_ref[...])

    step = pl.program_id(0)

    @pl.when(step < steps_a)
    def _first():
        tile(xa_ref, ya_ref)

    @pl.when(step >= steps_a)
    def _second():
        tile(xb_ref, yb_ref)


def _ffn(xa, xb, w, w_bf16, *, tm):
    steps_a, steps_b = xa.shape[0] // tm, xb.shape[0] // tm
    d_ff = w_bf16[0].shape[-1]

    def cspec(shape):
        return pl.BlockSpec(shape, lambda i: (0, 0), pipeline_mode=pl.Buffered(1))

    def layer_spec(shape):
        return pl.BlockSpec((None,) + shape, lambda i: (0, 0, 0), pipeline_mode=pl.Buffered(1))

    spec_a = pl.BlockSpec((tm, D_MODEL), lambda i: (jnp.minimum(i, steps_a - 1), 0))
    spec_b = pl.BlockSpec((tm, D_MODEL), lambda i: (jnp.maximum(i - steps_a, 0), 0))
    return pl.pallas_call(
        functools.partial(_ffn_kernel, steps_a=steps_a),
        grid=(steps_a + steps_b,),
        in_specs=[spec_a, spec_b,
                  cspec((1, D_MODEL)), layer_spec((D_MODEL, d_ff)), layer_spec((D_MODEL, d_ff)),
                  layer_spec((d_ff, D_MODEL)), cspec((1, D_MODEL))],
        out_specs=(spec_a, spec_b),
        out_shape=(jax.ShapeDtypeStruct(xa.shape, F32), jax.ShapeDtypeStruct(xb.shape, F32)),
        scratch_shapes=[pltpu.VMEM((tm, d_ff), BF16)],
        compiler_params=pltpu.CompilerParams(
            dimension_semantics=("arbitrary",), vmem_limit_bytes=VMEM_LIMIT),
        name="ffn",
    )(xa, xb, w["norm_ffn"], *w_bf16, w["norm_final"])


HALF_GROUPS = S5_GROUPS // 2


def _block_diag_halves(stacks):
    rows = HALF_GROUPS * S5_GROUP
    cols = HALF_GROUPS * S5_STATE
    x = jnp.concatenate([m.reshape(2 * rows, S5_STATE) for m in stacks], axis=0).astype(BF16)
    spread = jnp.asarray(np.tile(np.eye(S5_STATE), (1, HALF_GROUPS)), dtype=BF16)
    diag = (np.arange(rows)[:, None] // S5_GROUP) == (np.arange(cols)[None, :] // S5_STATE)
    y = jnp.dot(x, spread, preferred_element_type=BF16)
    y = jnp.where(np.tile(diag, (2 * len(stacks), 1)), y, jnp.zeros((), BF16))
    return y.reshape(len(stacks), 2, rows, cols)


def _segment_mask(tt):
    r = np.arange(CHUNK)
    same = (r[:, None] // tt) == (r[None, :] // tt)
    return jnp.asarray(same & (r[:, None] >= r[None, :]), dtype=BF16)


def _prepare(lb_param, norm_mix, w_in, s5_a_re, s5_a_im, s5_log_dt, s5_b_re, s5_b_im, s5_c_re,
             s5_c_im, s5_d, s5_w_glu, hg_norm, w_out, norm_ffn, w_gate, w_up, w_down, norm_final):
    l = 0
    a_re = s5_a_re[l].astype(F32)
    a_im = s5_a_im[l].astype(F32)
    dt = jnp.exp(s5_log_dt[l].astype(F32))[:, None]
    mag = jnp.exp(a_re * dt)
    ab_re = mag * jnp.cos(a_im * dt)
    ab_im = mag * jnp.sin(a_im * dt)
    den = a_re * a_re + a_im * a_im
    nr = ab_re - 1.0
    ni = ab_im
    f_re = (nr * a_re + ni * a_im) / den
    f_im = (ni * a_re - nr * a_im) / den
    b_re = s5_b_re[l].astype(F32).transpose(0, 2, 1)
    b_im = s5_b_im[l].astype(F32).transpose(0, 2, 1)
    bb_re = f_re[:, None, :] * b_re - f_im[:, None, :] * b_im
    bb_im = f_re[:, None, :] * b_im + f_im[:, None, :] * b_re
    blk = _block_diag_halves([bb_re, bb_im, s5_c_re[l].astype(F32), -s5_c_im[l].astype(F32)])
    wb = blk[0:2]
    wc = jnp.swapaxes(blk[2:4], 2, 3)
    lb_all = jnp.cumsum(jax.nn.softmax(lb_param.astype(F32), axis=0), axis=0)
    return {
        "norm_mix": norm_mix[l].reshape(1, D_MODEL).astype(F32),
        "w_in": w_in[l].astype(BF16),
        "wb": wb,
        "a_re": ab_re.reshape(1, S5_FLAT),
        "a_im": ab_im.reshape(1, S5_FLAT),
        "wc": wc,
        "d_skip": s5_d[l].reshape(1, S5_WIDTH).astype(F32),
        "w_glu": s5_w_glu[l].astype(BF16),
        "lb": lb_all[l].reshape(1, HG_WIDTH),
        "g_norm": hg_norm[l].reshape(1, HG_DIM).astype(F32),
        "w_out": w_out[l].astype(BF16),
        "mask_64": _segment_mask(64),
        "mask_8": _segment_mask(8),
        "norm_ffn": norm_ffn[l].reshape(1, D_MODEL).astype(F32),
        "w_gate": w_gate.astype(F32),
        "w_up": w_up.astype(F32),
        "w_down": w_down.astype(F32),
        "norm_final": norm_final.reshape(1, D_MODEL).astype(F32),
    }


def _states(h_re, h_im, s_new):
    b = h_re.shape[0]
    return (h_re.reshape(1, b, S5_GROUPS, S5_STATE), h_im.reshape(1, b, S5_GROUPS, S5_STATE),
            s_new.reshape(1, b, HG_HEADS, HG_DIM, HG_DIM))


def kernel(x_prompt, x_sample, state_s5_re, state_s5_im, state_hgrn, lb_param, norm_mix, w_in,
           s5_a_re, s5_a_im, s5_log_dt, s5_b_re, s5_b_im, s5_c_re, s5_c_im, s5_d, s5_w_glu,
           hg_norm, w_out, norm_ffn, w_gate, w_up, w_down, norm_final):
    assert norm_mix.shape[0] == 1, "single-layer trunk"
    w = _prepare(lb_param, norm_mix, w_in, s5_a_re, s5_a_im, s5_log_dt, s5_b_re, s5_b_im,
                 s5_c_re, s5_c_im, s5_d, s5_w_glu, hg_norm, w_out, norm_ffn, w_gate, w_up,
                 w_down, norm_final)
    bp, tp, _ = x_prompt.shape
    bs, ts, _ = x_sample.shape
    x1_p, *rest = _mixer(x_prompt, None, w, nb=8, tt=64,
                         ffn_weights=(w["w_gate"], w["w_up"], w["w_down"]))
    st_p, w_bf16 = rest[:3], rest[3:]
    st = (state_s5_re[0].reshape(bs, S5_FLAT), state_s5_im[0].reshape(bs, S5_FLAT), state_hgrn)
    x1_s, *st_s = _mixer(x_sample, st, w, nb=16, tt=8)
    y_p, y_s = _ffn(x1_p.reshape(bp * tp, D_MODEL), x1_s.reshape(bs * ts, D_MODEL), w, w_bf16,
                    tm=512)
    return (y_p.reshape(bp, tp, D_MODEL), y_s.reshape(bs, ts, D_MODEL),
            *_states(*st_p), *_states(*st_s))
```

```python
import functools
import math

import jax
import jax.numpy as jnp
import numpy as np
from jax import lax
from jax.experimental import pallas as pl
from jax.experimental.pallas import tpu as pltpu

F32 = jnp.float32
BF16 = jnp.bfloat16

D_MODEL = 1024
S5_WIDTH = 512
S5_GROUP = 16
S5_GROUPS = 32
S5_STATE = 64
S5_FLAT = S5_GROUPS * S5_STATE
HG_WIDTH = 512
HG_DIM = 128
HG_HEADS = 4
IN_COLS = S5_WIDTH + 4 * HG_WIDTH
EPS = 1e-6

LANES = 128
SUBLANES = 8
CHUNK = 64
N_SLABS = S5_FLAT // LANES
HALF_U = S5_WIDTH // 2
HALF_SLABS = N_SLABS // 2
SUB = 16
READ_K = 256
SCAN_COLS = 256
SCAN_PAD = 4
VMEM_LIMIT = 56 * 1024 * 1024


def _seq_pitch(tt):
    return tt + SCAN_PAD if tt == CHUNK else tt


def _dot(a, b):
    return jnp.dot(a, b, preferred_element_type=F32)


def _dot_nt(a, b):
    return lax.dot_general(a, b, (((1,), (1,)), ((), ())), preferred_element_type=F32)


def _dot_tn(a, b):
    return lax.dot_general(a, b, (((0,), (0,)), ((), ())), preferred_element_type=F32)


def _split3(x):
    hi = x.astype(BF16)
    r1 = x - hi.astype(F32)
    mid = r1.astype(BF16)
    lo = (r1 - mid.astype(F32)).astype(BF16)
    return hi, mid, lo


def _rms(x, g):
    return x * lax.rsqrt(jnp.mean(x * x, axis=-1, keepdims=True) + EPS) * g


def _sigmoid(x):
    return jax.nn.sigmoid(x)


def _gelu_tanh(x):
    c = math.sqrt(2.0 / math.pi)
    return 0.5 * x * (1.0 + jnp.tanh(c * (x + 0.044715 * (x * x * x))))


def _mixer_kernel(*refs, nb, tt, has_state, cast_down_steps=0):
    cast_ffn = cast_down_steps > 0
    it = iter(refs)
    x_ref = next(it)
    if has_state:
        h0re_ref, h0im_ref, s0_ref = next(it), next(it), next(it)
    (norm_ref, win_ref, wb_ref, are_ref, aim_ref, wc_ref, dskip_ref, wglu_ref, lb_ref,
     gnorm_ref, wout_ref, mask_ref) = (next(it) for _ in range(12))
    cast_in = tuple(next(it) for _ in range(3)) if cast_ffn else ()
    x1_ref, hre_ref, him_ref, s_ref = (next(it) for _ in range(4))
    cast_out = tuple(next(it) for _ in range(3)) if cast_ffn else ()
    proj_scr, bre_scr, bim_scr, mix_scr = (next(it) for _ in range(4))

    bpc = CHUNK // tt
    nchunks = nb // bpc
    pitch = _seq_pitch(tt)
    cpitch = pitch * bpc
    ngroups = nb // SUBLANES
    state_t = tt == CHUNK
    assert not (state_t and has_state)

    tstep = pl.program_id(1)

    @pl.when(tstep == 0)
    def _init():
        if has_state:
            hre_ref[...] = h0re_ref[...]
            him_ref[...] = h0im_ref[...]
        else:
            hre_ref[...] = jnp.zeros_like(hre_ref)
            him_ref[...] = jnp.zeros_like(him_ref)
            s_ref[...] = jnp.zeros_like(s_ref)

    rows_tile = nb * tt
    scr_row0 = [c * cpitch for c in range(nchunks)]

    x_in = x_ref[...].reshape(rows_tile, D_MODEL)
    hn = (x_in * norm_ref[...]).astype(BF16)
    r_in = lax.rsqrt(jnp.mean(x_in * x_in, axis=-1, keepdims=True) + EPS)
    proj_scr[:, 0:S5_WIDTH] = r_in * _dot(hn, win_ref[:, 0:S5_WIDTH])
    ub = proj_scr[:, 0:S5_WIDTH].astype(BF16)
    for hf in range(2):
        ubh = ub[:, hf * HALF_U:(hf + 1) * HALF_U]
        for part, scr in enumerate((bre_scr, bim_scr)):
            bu = _dot(ubh, wb_ref[part, hf])
            for jj in range(HALF_SLABS):
                for c in range(nchunks):
                    scr[hf * HALF_SLABS + jj, scr_row0[c]:scr_row0[c] + CHUNK, :] = (
                        bu[c * CHUNK:(c + 1) * CHUNK, jj * LANES:(jj + 1) * LANES])

    n_rest = (IN_COLS - S5_WIDTH) // SCAN_COLS
    steps_per_iter = tt // n_rest
    for i in range(n_rest):
        n0 = S5_WIDTH + i * SCAN_COLS
        proj_scr[:, n0:n0 + SCAN_COLS] = r_in * _dot(hn, win_ref[:, n0:n0 + SCAN_COLS])
        t0 = i * steps_per_iter
        for g in range(ngroups):
            base = g * SUBLANES * pitch
            grows = slice(g * SUBLANES, (g + 1) * SUBLANES)
            for j in range(N_SLABS):
                lanes = slice(j * LANES, (j + 1) * LANES)
                a_re = jnp.broadcast_to(are_ref[:, lanes], (SUBLANES, LANES))
                a_im = jnp.broadcast_to(aim_ref[:, lanes], (SUBLANES, LANES))
                h_re = hre_ref[grows, lanes]
                h_im = him_ref[grows, lanes]
                for dt in range(steps_per_iter):
                    idx = pl.ds(base + t0 + dt, SUBLANES, stride=pitch)
                    n_re = a_re * h_re - a_im * h_im + bre_scr[j, idx, :]
                    n_im = a_re * h_im + a_im * h_re + bim_scr[j, idx, :]
                    bre_scr[j, idx, :] = n_re
                    bim_scr[j, idx, :] = n_im
                    h_re, h_im = n_re, n_im
                hre_ref[grows, lanes] = h_re
                him_ref[grows, lanes] = h_im

    ys = [None, None]
    kb_per_half = 2 * HALF_SLABS * LANES // READ_K

    def readout_piece(p):
        hf, kb = divmod(p, kb_per_half)
        part, kb = divmod(kb, kb_per_half // 2)
        scr = (bre_scr, bim_scr)[part]
        j0 = hf * HALF_SLABS + (READ_K // LANES) * kb
        piece = jnp.concatenate(
            [jnp.concatenate([scr[j0 + d, scr_row0[c]:scr_row0[c] + CHUNK, :]
                              for c in range(nchunks)], axis=0)
             for d in range(READ_K // LANES)], axis=1).astype(BF16)
        term = _dot(piece, wc_ref[part, hf, kb * READ_K:(kb + 1) * READ_K, :])
        ys[hf] = term if ys[hf] is None else ys[hf] + term

    def readout_finish():
        y = jnp.concatenate(ys, axis=1) + dskip_ref[...] * proj_scr[:, 0:S5_WIDTH]
        gact = _gelu_tanh(y)
        out5 = gact * _sigmoid(_dot(gact.astype(BF16), wglu_ref[...]))
        mix_scr[:, 0:S5_WIDTH] = out5.astype(BF16)

    n_pieces = 2 * kb_per_half

    lb = lb_ref[...]
    seg_mask = mask_ref[...].astype(F32) > 0.5
    zeros_t = jnp.zeros((tt, HG_DIM), F32)
    dec_rows = jnp.concatenate(
        [jnp.zeros((3 * tt, HG_DIM), F32), jnp.ones((3 * tt, HG_DIM), F32)], axis=1)
    heads = range(HG_HEADS)

    def chunk_stages(c):
        rows = slice(c * CHUNK, (c + 1) * CHUNK)
        q_all = proj_scr[rows, S5_WIDTH:S5_WIDTH + HG_WIDTH]
        fz = proj_scr[rows, S5_WIDTH + HG_WIDTH:S5_WIDTH + 2 * HG_WIDTH]
        iv_all = proj_scr[rows, S5_WIDTH + 2 * HG_WIDTH:S5_WIDTH + 3 * HG_WIDTH]
        og_all = proj_scr[rows, S5_WIDTH + 3 * HG_WIDTH:S5_WIDTH + 4 * HG_WIDTH]
        f = lb + (1.0 - lb) * _sigmoid(fz)
        lf_all = jnp.log(f)
        k_all = 1.0 - f
        lf_pieces = _split3(lf_all)
        g_all = sum(_dot(mask_ref[...], p) for p in lf_pieces)
        vb_all = iv_all.astype(BF16)
        qg_all = (q_all * jnp.exp(g_all)).astype(BF16)
        hs = [slice(h * HG_DIM, (h + 1) * HG_DIM) for h in heads]

        qs, ks = [], []
        for h in heads:
            gh, qh, kh = g_all[:, hs[h]], q_all[:, hs[h]], k_all[:, hs[h]]
            if tt == CHUNK:
                nblk = CHUNK // SUB
                blocks = [slice(b * SUB, (b + 1) * SUB) for b in range(nblk)]
                refs = [gh[b * SUB + SUB // 2:b * SUB + SUB // 2 + 1, :] for b in range(nblk)]
                ref_rows = jnp.concatenate(
                    [jnp.broadcast_to(r, (SUB, HG_DIM)) for r in refs], axis=0)
                q_loc = (qh * jnp.exp(gh - ref_rows)).astype(BF16)
                k_loc = kh * jnp.exp(ref_rows - gh)
                no_keys = jnp.zeros((SUB, HG_DIM), BF16)
                for i in range(nblk):
                    qs.append(q_loc[blocks[i]])
                    parts = [(k_loc[blocks[j]] * jnp.exp(refs[i] - refs[j])).astype(BF16)
                             for j in range(i)]
                    parts.append(k_loc[blocks[i]].astype(BF16))
                    parts += [no_keys] * (nblk - 1 - i)
                    ks.append(jnp.concatenate(parts, axis=0))
            else:
                g3 = gh.reshape(bpc, tt, HG_DIM)
                ref3 = jnp.broadcast_to(g3[:, tt // 2:tt // 2 + 1, :], (bpc, tt, HG_DIM))
                ref_rows = ref3.reshape(CHUNK, HG_DIM)
                qs.append((qh * jnp.exp(gh - ref_rows)).astype(BF16))
                ks.append((kh * jnp.exp(ref_rows - gh)).astype(BF16))

        yield
        att = [_dot_nt(a, b) for a, b in zip(qs, ks)]

        yield
        o_inter = []
        for h in heads:
            gh, kh = g_all[:, hs[h]], k_all[:, hs[h]]
            oi = []
            for bb in range(bpc):
                rs = slice(bb * tt, (bb + 1) * tt)
                b_idx = c * bpc + bb
                s_old = s0_ref[b_idx, h] if has_state else s_ref[b_idx, h]
                g_last = gh[bb * tt + tt - 1:bb * tt + tt, :]
                kd = kh[rs] * jnp.exp(g_last - gh[rs])
                if state_t:
                    oi.append(_dot_nt(qg_all[rs, hs[h]], s_old.astype(BF16)))
                    s_ref[b_idx, h] = (jnp.exp(g_last) * s_old
                                       + _dot_tn(vb_all[rs, hs[h]], kd.astype(BF16)))
                else:
                    oi.append(_dot(qg_all[rs, hs[h]], s_old.astype(BF16)))
                    lhs = jnp.concatenate(
                        [kd] + [p[rs, hs[h]].astype(F32) for p in lf_pieces], axis=0)
                    rhs = jnp.concatenate(
                        [jnp.concatenate([iv_all[rs, hs[h]], zeros_t], axis=1), dec_rows], axis=0)
                    upd = _dot_tn(lhs.astype(BF16), rhs.astype(BF16))
                    s_ref[b_idx, h] = jnp.exp(upd[:, HG_DIM:]) * s_old + upd[:, :HG_DIM]
            o_inter.append(oi[0] if bpc == 1 else jnp.concatenate(oi, axis=0))

        yield
        nblk = len(att) // HG_HEADS
        for h in heads:
            sc = att[h * nblk:(h + 1) * nblk]
            sc = sc[0] if nblk == 1 else jnp.concatenate(sc, axis=0)
            sc = jnp.where(seg_mask, sc, 0.0).astype(BF16)
            o = _dot(sc, vb_all[:, hs[h]]) + o_inter[h]
            o = _rms(o, gnorm_ref[...])
            ogh = og_all[:, hs[h]]
            mix_scr[rows, S5_WIDTH + h * HG_DIM:S5_WIDTH + (h + 1) * HG_DIM] = (
                o * (ogh * _sigmoid(ogh))).astype(BF16)

    per_chunk = n_pieces // nchunks
    gens = [chunk_stages(c) for c in range(nchunks)]
    next(gens[0])
    for c in range(nchunks):
        pieces = range(c * per_chunk, (c + 1) * per_chunk)
        next(gens[c])
        for p in pieces[:per_chunk // 2]:
            readout_piece(p)
        if c + 1 < nchunks:
            next(gens[c + 1])
        next(gens[c])
        for p in pieces[per_chunk // 2:]:
            readout_piece(p)
        for _ in gens[c]:
            pass
    readout_finish()

    if state_t:
        @pl.when(tstep == pl.num_programs(1) - 1)
        def _untranspose_state():
            for b in range(nb):
                for h in heads:
                    s_ref[b, h] = s_ref[b, h].T

    if cast_ffn:
        for src, dst in zip(cast_in[:2], cast_out[:2]):
            dst[...] = src[...].astype(BF16)

        @pl.when(pl.program_id(0) * pl.num_programs(1) + tstep < cast_down_steps)
        def _cast_down():
            cast_out[2][...] = cast_in[2][...].astype(BF16)

    x1 = x_ref[...].reshape(rows_tile, D_MODEL) + _dot(mix_scr[...], wout_ref[...])
    x1_ref[...] = x1.reshape(nb, tt, D_MODEL)


def _const_spec(shape):
    nd = len(shape)
    return pl.BlockSpec(shape, lambda i, j, _nd=nd: (0,) * _nd, pipeline_mode=pl.Buffered(1))


def _mixer(x, state, w, *, nb, tt, ffn_weights=None):
    B, T, _ = x.shape
    has_state = state is not None
    grid = (B // nb, T // tt)
    bpc = CHUNK // tt
    nchunks = nb // bpc
    pitch = _seq_pitch(tt)
    scr_rows = nchunks * pitch * bpc

    consts = [w["norm_mix"], w["w_in"], w["wb"], w["a_re"], w["a_im"], w["wc"], w["d_skip"],
              w["w_glu"], w["lb"], w["g_norm"], w["w_out"], w["mask_" + str(tt)]]
    in_specs = [pl.BlockSpec((nb, tt, D_MODEL), lambda i, j: (i, j, 0))]
    args = [x]
    if has_state:
        in_specs += [pl.BlockSpec((nb, S5_FLAT), lambda i, j: (i, 0)),
                     pl.BlockSpec((nb, S5_FLAT), lambda i, j: (i, 0)),
                     pl.BlockSpec((None, nb, HG_HEADS, HG_DIM, HG_DIM),
                                  lambda i, j: (0, i, 0, 0, 0))]
        args += list(state)
    in_specs += [_const_spec(c.shape) for c in consts]
    args += consts
    cast_ffn = ffn_weights is not None
    cast_specs, cast_shapes, down_steps = [], [], 0
    if cast_ffn:
        w_gate, w_up, w_down = ffn_weights
        n_t = grid[1]
        n_steps = grid[0] * n_t
        d_ff = w_gate.shape[-1]
        up_rows = D_MODEL // n_steps
        down_rows = next(r for r in range(16, d_ff + 1, 16) if d_ff % r == 0 and d_ff // r <= n_steps)
        down_steps = d_ff // down_rows
        assert D_MODEL % n_steps == 0 and up_rows % 16 == 0
        cast_specs = [pl.BlockSpec((None, up_rows, d_ff), lambda i, j: (0, i * n_t + j, 0)),
                      pl.BlockSpec((None, up_rows, d_ff), lambda i, j: (0, i * n_t + j, 0)),
                      pl.BlockSpec((None, down_rows, D_MODEL),
                                   lambda i, j: (0, jnp.minimum(i * n_t + j, down_steps - 1), 0))]
        cast_shapes = [jax.ShapeDtypeStruct(a.shape, BF16) for a in ffn_weights]
        in_specs += cast_specs
        args += list(ffn_weights)

    out_shape = (jax.ShapeDtypeStruct((B, T, D_MODEL), F32),
                 jax.ShapeDtypeStruct((B, S5_FLAT), F32),
                 jax.ShapeDtypeStruct((B, S5_FLAT), F32),
                 jax.ShapeDtypeStruct((B, HG_HEADS, HG_DIM, HG_DIM), F32))
    out_specs = (pl.BlockSpec((nb, tt, D_MODEL), lambda i, j: (i, j, 0)),
                 pl.BlockSpec((nb, S5_FLAT), lambda i, j: (i, 0)),
                 pl.BlockSpec((nb, S5_FLAT), lambda i, j: (i, 0)),
                 pl.BlockSpec((nb, HG_HEADS, HG_DIM, HG_DIM), lambda i, j: (i, 0, 0, 0)))
    out_shape += tuple(cast_shapes)
    out_specs += tuple(cast_specs)
    scratch = [pltpu.VMEM((nb * tt, IN_COLS), F32),
               pltpu.VMEM((N_SLABS, scr_rows, LANES), F32),
               pltpu.VMEM((N_SLABS, scr_rows, LANES), F32),
               pltpu.VMEM((nb * tt, D_MODEL), BF16)]
    return pl.pallas_call(
        functools.partial(_mixer_kernel, nb=nb, tt=tt, has_state=has_state,
                          cast_down_steps=down_steps),
        grid=grid, in_specs=in_specs, out_specs=out_specs, out_shape=out_shape,
        scratch_shapes=scratch,
        compiler_params=pltpu.CompilerParams(
            dimension_semantics=("arbitrary", "arbitrary"), vmem_limit_bytes=VMEM_LIMIT),
        name="mixer_t%d" % tt,
    )(*args)


FF_CHUNK = 256


def _ffn_kernel(xa_ref, xb_ref, nffn_ref, wg_ref, wu_ref, wd_ref, nfin_ref, ya_ref, yb_ref,
                act_scr, *, steps_a):
    def tile(x_ref, y_ref):
        x = x_ref[...]
        xg = (x * nffn_ref[...]).astype(BF16)
        r = lax.rsqrt(jnp.mean(x * x, axis=-1, keepdims=True) + EPS)
        d_ff = wg_ref.shape[1]
        for f0 in range(0, d_ff, FF_CHUNK):
            gate = r * _dot(xg, wg_ref[:, f0:f0 + FF_CHUNK])
            up = r * _dot(xg, wu_ref[:, f0:f0 + FF_CHUNK])
            act_scr[:, f0:f0 + FF_CHUNK] = (gate * _sigmoid(gate) * up).astype(BF16)
        acc = x + _dot(act_scr[...], wd_ref[...])
        y_ref[...] = _rms(acc, nfin_ref[...])

    step = pl.program_id(0)

    @pl.when(step < steps_a)
    def _first():
        tile(xa_ref, ya_ref)

    @pl.when(step >= steps_a)
    def _second():
        tile(xb_ref, yb_ref)


def _ffn(xa, xb, w, w_bf16, *, tm):
    steps_a, steps_b = xa.shape[0] // tm, xb.shape[0] // tm
    d_ff = w_bf16[0].shape[-1]

    def cspec(shape):
        return pl.BlockSpec(shape, lambda i: (0, 0), pipeline_mode=pl.Buffered(1))

    def layer_spec(shape):
        return pl.BlockSpec((None,) + shape, lambda i: (0, 0, 0), pipeline_mode=pl.Buffered(1))

    spec_a = pl.BlockSpec((tm, D_MODEL), lambda i: (jnp.minimum(i, steps_a - 1), 0))
    spec_b = pl.BlockSpec((tm, D_MODEL), lambda i: (jnp.maximum(i - steps_a, 0), 0))
    return pl.pallas_call(
        functools.partial(_ffn_kernel, steps_a=steps_a),
        grid=(steps_a + steps_b,),
        in_specs=[spec_a, spec_b,
                  cspec((1, D_MODEL)), layer_spec((D_MODEL, d_ff)), layer_spec((D_MODEL, d_ff)),
                  layer_spec((d_ff, D_MODEL)), cspec((1, D_MODEL))],
        out_specs=(spec_a, spec_b),
        out_shape=(jax.ShapeDtypeStruct(xa.shape, F32), jax.ShapeDtypeStruct(xb.shape, F32)),
        scratch_shapes=[pltpu.VMEM((tm, d_ff), BF16)],
        compiler_params=pltpu.CompilerParams(
            dimension_semantics=("arbitrary",), vmem_limit_bytes=VMEM_LIMIT),
        name="ffn",
    )(xa, xb, w["norm_ffn"], *w_bf16, w["norm_final"])


HALF_GROUPS = S5_GROUPS // 2


def _block_diag_halves(stacks):
    rows = HALF_GROUPS * S5_GROUP
    cols = HALF_GROUPS * S5_STATE
    x = jnp.concatenate([m.reshape(2 * rows, S5_STATE) for m in stacks], axis=0).astype(BF16)
    spread = jnp.asarray(np.tile(np.eye(S5_STATE), (1, HALF_GROUPS)), dtype=BF16)
    diag = (np.arange(rows)[:, None] // S5_GROUP) == (np.arange(cols)[None, :] // S5_STATE)
    y = jnp.dot(x, spread, preferred_element_type=BF16)
    y = jnp.where(np.tile(diag, (2 * len(stacks), 1)), y, jnp.zeros((), BF16))
    return y.reshape(len(stacks), 2, rows, cols)


def _segment_mask(tt):
    r = np.arange(CHUNK)
    same = (r[:, None] // tt) == (r[None, :] // tt)
    return jnp.asarray(same & (r[:, None] >= r[None, :]), dtype=BF16)


def _prepare(lb_param, norm_mix, w_in, s5_a_re, s5_a_im, s5_log_dt, s5_b_re, s5_b_im, s5_c_re,
             s5_c_im, s5_d, s5_w_glu, hg_norm, w_out, norm_ffn, w_gate, w_up, w_down, norm_final):
    l = 0
    a_re = s5_a_re[l].astype(F32)
    a_im = s5_a_im[l].astype(F32)
    dt = jnp.exp(s5_log_dt[l].astype(F32))[:, None]
    mag = jnp.exp(a_re * dt)
    ab_re = mag * jnp.cos(a_im * dt)
    ab_im = mag * jnp.sin(a_im * dt)
    den = a_re * a_re + a_im * a_im
    nr = ab_re - 1.0
    ni = ab_im
    f_re = (nr * a_re + ni * a_im) / den
    f_im = (ni * a_re - nr * a_im) / den
    b_re = s5_b_re[l].astype(F32).transpose(0, 2, 1)
    b_im = s5_b_im[l].astype(F32).transpose(0, 2, 1)
    bb_re = f_re[:, None, :] * b_re - f_im[:, None, :] * b_im
    bb_im = f_re[:, None, :] * b_im + f_im[:, None, :] * b_re
    blk = _block_diag_halves([bb_re, bb_im, s5_c_re[l].astype(F32), -s5_c_im[l].astype(F32)])
    wb = blk[0:2]
    wc = jnp.swapaxes(blk[2:4], 2, 3)
    lb_all = jnp.cumsum(jax.nn.softmax(lb_param.astype(F32), axis=0), axis=0)
    return {
        "norm_mix": norm_mix[l].reshape(1, D_MODEL).astype(F32),
        "w_in": w_in[l].astype(BF16),
        "wb": wb,
        "a_re": ab_re.reshape(1, S5_FLAT),
        "a_im": ab_im.reshape(1, S5_FLAT),
        "wc": wc,
        "d_skip": s5_d[l].reshape(1, S5_WIDTH).astype(F32),
        "w_glu": s5_w_glu[l].astype(BF16),
        "lb": lb_all[l].reshape(1, HG_WIDTH),
        "g_norm": hg_norm[l].reshape(1, HG_DIM).astype(F32),
        "w_out": w_out[l].astype(BF16),
        "mask_64": _segment_mask(64),
        "mask_8": _segment_mask(8),
        "norm_ffn": norm_ffn[l].reshape(1, D_MODEL).astype(F32),
        "w_gate": w_gate.astype(F32),
        "w_up": w_up.astype(F32),
        "w_down": w_down.astype(F32),
        "norm_final": norm_final.reshape(1, D_MODEL).astype(F32),
    }


def _states(h_re, h_im, s_new):
    b = h_re.shape[0]
    return (h_re.reshape(1, b, S5_GROUPS, S5_STATE), h_im.reshape(1, b, S5_GROUPS, S5_STATE),
            s_new.reshape(1, b, HG_HEADS, HG_DIM, HG_DIM))


def kernel(x_prompt, x_sample, state_s5_re, state_s5_im, state_hgrn, lb_param, norm_mix, w_in,
           s5_a_re, s5_a_im, s5_log_dt, s5_b_re, s5_b_im, s5_c_re, s5_c_im, s5_d, s5_w_glu,
           hg_norm, w_out, norm_ffn, w_gate, w_up, w_down, norm_final):
    assert norm_mix.shape[0] == 1, "single-layer trunk"
    w = _prepare(lb_param, norm_mix, w_in, s5_a_re, s5_a_im, s5_log_dt, s5_b_re, s5_b_im,
                 s5_c_re, s5_c_im, s5_d, s5_w_glu, hg_norm, w_out, norm_ffn, w_gate, w_up,
                 w_down, norm_final)
    bp, tp, _ = x_prompt.shape
    bs, ts, _ = x_sample.shape
    x1_p, *st_p = _mixer(x_prompt, None, w, nb=8, tt=64)
    st = (state_s5_re[0].reshape(bs, S5_FLAT), state_s5_im[0].reshape(bs, S5_FLAT), state_hgrn)
    x1_s, *rest = _mixer(x_sample, st, w, nb=16, tt=8,
                         ffn_weights=(w["w_gate"], w["w_up"], w["w_down"]))
    st_s, w_bf16 = rest[:3], rest[3:]
    y_p, y_s = _ffn(x1_p.reshape(bp * tp, D_MODEL), x1_s.reshape(bs * ts, D_MODEL), w, w_bf16,
                    tm=512)
    return (y_p.reshape(bp, tp, D_MODEL), y_s.reshape(bs, ts, D_MODEL),
            *_states(*st_p), *_states(*st_s))
```

```python
import functools
import math

import jax
import jax.numpy as jnp
import numpy as np
from jax import lax
from jax.experimental import pallas as pl
from jax.experimental.pallas import tpu as pltpu

F32 = jnp.float32
BF16 = jnp.bfloat16

D_MODEL = 1024
S5_WIDTH = 512
S5_GROUP = 16
S5_GROUPS = 32
S5_STATE = 64
S5_FLAT = S5_GROUPS * S5_STATE
HG_WIDTH = 512
HG_DIM = 128
HG_HEADS = 4
IN_COLS = S5_WIDTH + 4 * HG_WIDTH
EPS = 1e-6

LANES = 128
SUBLANES = 8
CHUNK = 64
N_SLABS = S5_FLAT // LANES
HALF_U = S5_WIDTH // 2
HALF_SLABS = N_SLABS // 2
SUB = 16
DOWN_CAST_STEPS = 22
READ_K = 256
SCAN_COLS = 256
SCAN_PAD = 4
VMEM_LIMIT = 56 * 1024 * 1024


def _seq_pitch(tt):
    return tt + SCAN_PAD if tt == CHUNK else tt


def _dot(a, b):
    return jnp.dot(a, b, preferred_element_type=F32)


def _dot_nt(a, b):
    return lax.dot_general(a, b, (((1,), (1,)), ((), ())), preferred_element_type=F32)


def _dot_tn(a, b):
    return lax.dot_general(a, b, (((0,), (0,)), ((), ())), preferred_element_type=F32)


def _split3(x):
    hi = x.astype(BF16)
    r1 = x - hi.astype(F32)
    mid = r1.astype(BF16)
    lo = (r1 - mid.astype(F32)).astype(BF16)
    return hi, mid, lo


def _rms(x, g):
    return x * lax.rsqrt(jnp.mean(x * x, axis=-1, keepdims=True) + EPS) * g


def _sigmoid(x):
    return jax.nn.sigmoid(x)


def _gelu_tanh(x):
    c = math.sqrt(2.0 / math.pi)
    return 0.5 * x * (1.0 + jnp.tanh(c * (x + 0.044715 * (x * x * x))))


def _mixer_kernel(*refs, nb, tt, has_state, cast_ffn=False):
    it = iter(refs)
    x_ref = next(it)
    if has_state:
        h0re_ref, h0im_ref, s0_ref = next(it), next(it), next(it)
    (norm_ref, win_ref, wb_ref, are_ref, aim_ref, wc_ref, dskip_ref, wglu_ref, lb_ref,
     gnorm_ref, wout_ref, mask_ref) = (next(it) for _ in range(12))
    cast_in = tuple(next(it) for _ in range(3)) if cast_ffn else ()
    x1_ref, hre_ref, him_ref, s_ref = (next(it) for _ in range(4))
    cast_out = tuple(next(it) for _ in range(3)) if cast_ffn else ()
    proj_scr, bre_scr, bim_scr, mix_scr = (next(it) for _ in range(4))

    bpc = CHUNK // tt
    nchunks = nb // bpc
    pitch = _seq_pitch(tt)
    cpitch = pitch * bpc
    ngroups = nb // SUBLANES
    state_t = tt == CHUNK
    assert not (state_t and has_state)

    tstep = pl.program_id(1)

    @pl.when(tstep == 0)
    def _init():
        if has_state:
            hre_ref[...] = h0re_ref[...]
            him_ref[...] = h0im_ref[...]
        else:
            hre_ref[...] = jnp.zeros_like(hre_ref)
            him_ref[...] = jnp.zeros_like(him_ref)
            s_ref[...] = jnp.zeros_like(s_ref)

    rows_tile = nb * tt
    scr_row0 = [c * cpitch for c in range(nchunks)]

    x_in = x_ref[...].reshape(rows_tile, D_MODEL)
    hn = (x_in * norm_ref[...]).astype(BF16)
    r_in = lax.rsqrt(jnp.mean(x_in * x_in, axis=-1, keepdims=True) + EPS)
    proj_scr[:, 0:S5_WIDTH] = r_in * _dot(hn, win_ref[:, 0:S5_WIDTH])
    ub = proj_scr[:, 0:S5_WIDTH].astype(BF16)
    for hf in range(2):
        ubh = ub[:, hf * HALF_U:(hf + 1) * HALF_U]
        for part, scr in enumerate((bre_scr, bim_scr)):
            bu = _dot(ubh, wb_ref[part, hf])
            for jj in range(HALF_SLABS):
                for c in range(nchunks):
                    scr[hf * HALF_SLABS + jj, scr_row0[c]:scr_row0[c] + CHUNK, :] = (
                        bu[c * CHUNK:(c + 1) * CHUNK, jj * LANES:(jj + 1) * LANES])

    n_rest = (IN_COLS - S5_WIDTH) // SCAN_COLS
    steps_per_iter = tt // n_rest
    for i in range(n_rest):
        n0 = S5_WIDTH + i * SCAN_COLS
        proj_scr[:, n0:n0 + SCAN_COLS] = r_in * _dot(hn, win_ref[:, n0:n0 + SCAN_COLS])
        t0 = i * steps_per_iter
        for g in range(ngroups):
            base = g * SUBLANES * pitch
            grows = slice(g * SUBLANES, (g + 1) * SUBLANES)
            for j in range(N_SLABS):
                lanes = slice(j * LANES, (j + 1) * LANES)
                a_re = jnp.broadcast_to(are_ref[:, lanes], (SUBLANES, LANES))
                a_im = jnp.broadcast_to(aim_ref[:, lanes], (SUBLANES, LANES))
                h_re = hre_ref[grows, lanes]
                h_im = him_ref[grows, lanes]
                for dt in range(steps_per_iter):
                    idx = pl.ds(base + t0 + dt, SUBLANES, stride=pitch)
                    n_re = a_re * h_re - a_im * h_im + bre_scr[j, idx, :]
                    n_im = a_re * h_im + a_im * h_re + bim_scr[j, idx, :]
                    bre_scr[j, idx, :] = n_re
                    bim_scr[j, idx, :] = n_im
                    h_re, h_im = n_re, n_im
                hre_ref[grows, lanes] = h_re
                him_ref[grows, lanes] = h_im

    ys = [None, None]
    kb_per_half = 2 * HALF_SLABS * LANES // READ_K

    def readout_piece(p):
        hf, kb = divmod(p, kb_per_half)
        part, kb = divmod(kb, kb_per_half // 2)
        scr = (bre_scr, bim_scr)[part]
        j0 = hf * HALF_SLABS + (READ_K // LANES) * kb
        piece = jnp.concatenate(
            [jnp.concatenate([scr[j0 + d, scr_row0[c]:scr_row0[c] + CHUNK, :]
                              for c in range(nchunks)], axis=0)
             for d in range(READ_K // LANES)], axis=1).astype(BF16)
        term = _dot(piece, wc_ref[part, hf, kb * READ_K:(kb + 1) * READ_K, :])
        ys[hf] = term if ys[hf] is None else ys[hf] + term

    def readout_finish():
        y = jnp.concatenate(ys, axis=1) + dskip_ref[...] * proj_scr[:, 0:S5_WIDTH]
        gact = _gelu_tanh(y)
        out5 = gact * _sigmoid(_dot(gact.astype(BF16), wglu_ref[...]))
        mix_scr[:, 0:S5_WIDTH] = out5.astype(BF16)

    n_pieces = 2 * kb_per_half

    lb = lb_ref[...]
    seg_mask = mask_ref[...].astype(F32) > 0.5
    zeros_t = jnp.zeros((tt, HG_DIM), F32)
    dec_rows = jnp.concatenate(
        [jnp.zeros((3 * tt, HG_DIM), F32), jnp.ones((3 * tt, HG_DIM), F32)], axis=1)
    heads = range(HG_HEADS)

    def chunk_stages(c):
        rows = slice(c * CHUNK, (c + 1) * CHUNK)
        q_all = proj_scr[rows, S5_WIDTH:S5_WIDTH + HG_WIDTH]
        fz = proj_scr[rows, S5_WIDTH + HG_WIDTH:S5_WIDTH + 2 * HG_WIDTH]
        iv_all = proj_scr[rows, S5_WIDTH + 2 * HG_WIDTH:S5_WIDTH + 3 * HG_WIDTH]
        og_all = proj_scr[rows, S5_WIDTH + 3 * HG_WIDTH:S5_WIDTH + 4 * HG_WIDTH]
        f = lb + (1.0 - lb) * _sigmoid(fz)
        lf_all = jnp.log(f)
        k_all = 1.0 - f
        lf_pieces = _split3(lf_all)
        g_all = sum(_dot(mask_ref[...], p) for p in lf_pieces)
        vb_all = iv_all.astype(BF16)
        qg_all = (q_all * jnp.exp(g_all)).astype(BF16)
        hs = [slice(h * HG_DIM, (h + 1) * HG_DIM) for h in heads]

        qs, ks = [], []
        for h in heads:
            gh, qh, kh = g_all[:, hs[h]], q_all[:, hs[h]], k_all[:, hs[h]]
            if tt == CHUNK:
                nblk = CHUNK // SUB
                blocks = [slice(b * SUB, (b + 1) * SUB) for b in range(nblk)]
                refs = [gh[b * SUB + SUB // 2:b * SUB + SUB // 2 + 1, :] for b in range(nblk)]
                ref_rows = jnp.concatenate(
                    [jnp.broadcast_to(r, (SUB, HG_DIM)) for r in refs], axis=0)
                q_loc = (qh * jnp.exp(gh - ref_rows)).astype(BF16)
                k_loc = kh * jnp.exp(ref_rows - gh)
                no_keys = jnp.zeros((SUB, HG_DIM), BF16)
                for i in range(nblk):
                    qs.append(q_loc[blocks[i]])
                    parts = [(k_loc[blocks[j]] * jnp.exp(refs[i] - refs[j])).astype(BF16)
                             for j in range(i)]
                    parts.append(k_loc[blocks[i]].astype(BF16))
                    parts += [no_keys] * (nblk - 1 - i)
                    ks.append(jnp.concatenate(parts, axis=0))
            else:
                g3 = gh.reshape(bpc, tt, HG_DIM)
                ref3 = jnp.broadcast_to(g3[:, tt // 2:tt // 2 + 1, :], (bpc, tt, HG_DIM))
                ref_rows = ref3.reshape(CHUNK, HG_DIM)
                qs.append((qh * jnp.exp(gh - ref_rows)).astype(BF16))
                ks.append((kh * jnp.exp(ref_rows - gh)).astype(BF16))

        yield
        att = [_dot_nt(a, b) for a, b in zip(qs, ks)]

        yield
        o_inter = []
        for h in heads:
            gh, kh = g_all[:, hs[h]], k_all[:, hs[h]]
            oi = []
            for bb in range(bpc):
                rs = slice(bb * tt, (bb + 1) * tt)
                b_idx = c * bpc + bb
                s_old = s0_ref[b_idx, h] if has_state else s_ref[b_idx, h]
                g_last = gh[bb * tt + tt - 1:bb * tt + tt, :]
                kd = kh[rs] * jnp.exp(g_last - gh[rs])
                if state_t:
                    oi.append(_dot_nt(qg_all[rs, hs[h]], s_old.astype(BF16)))
                    s_ref[b_idx, h] = (jnp.exp(g_last) * s_old
                                       + _dot_tn(vb_all[rs, hs[h]], kd.astype(BF16)))
                else:
                    oi.append(_dot(qg_all[rs, hs[h]], s_old.astype(BF16)))
                    lhs = jnp.concatenate(
                        [kd] + [p[rs, hs[h]].astype(F32) for p in lf_pieces], axis=0)
                    rhs = jnp.concatenate(
                        [jnp.concatenate([iv_all[rs, hs[h]], zeros_t], axis=1), dec_rows], axis=0)
                    upd = _dot_tn(lhs.astype(BF16), rhs.astype(BF16))
                    s_ref[b_idx, h] = jnp.exp(upd[:, HG_DIM:]) * s_old + upd[:, :HG_DIM]
            o_inter.append(oi[0] if bpc == 1 else jnp.concatenate(oi, axis=0))

        yield
        nblk = len(att) // HG_HEADS
        for h in heads:
            sc = att[h * nblk:(h + 1) * nblk]
            sc = sc[0] if nblk == 1 else jnp.concatenate(sc, axis=0)
            sc = jnp.where(seg_mask, sc, 0.0).astype(BF16)
            o = _dot(sc, vb_all[:, hs[h]]) + o_inter[h]
            o = _rms(o, gnorm_ref[...])
            ogh = og_all[:, hs[h]]
            mix_scr[rows, S5_WIDTH + h * HG_DIM:S5_WIDTH + (h + 1) * HG_DIM] = (
                o * (ogh * _sigmoid(ogh))).astype(BF16)

    per_chunk = n_pieces // nchunks
    gens = [chunk_stages(c) for c in range(nchunks)]
    next(gens[0])
    for c in range(nchunks):
        pieces = range(c * per_chunk, (c + 1) * per_chunk)
        next(gens[c])
        for p in pieces[:per_chunk // 2]:
            readout_piece(p)
        if c + 1 < nchunks:
            next(gens[c + 1])
        next(gens[c])
        for p in pieces[per_chunk // 2:]:
            readout_piece(p)
        for _ in gens[c]:
            pass
    readout_finish()

    if state_t:
        @pl.when(tstep == pl.num_programs(1) - 1)
        def _untranspose_state():
            for b in range(nb):
                for h in heads:
                    s_ref[b, h] = s_ref[b, h].T

    if cast_ffn:
        for src, dst in zip(cast_in[:2], cast_out[:2]):
            dst[...] = src[...].astype(BF16)

        @pl.when(tstep < DOWN_CAST_STEPS)
        def _cast_down():
            cast_out[2][...] = cast_in[2][...].astype(BF16)

    x1 = x_ref[...].reshape(rows_tile, D_MODEL) + _dot(mix_scr[...], wout_ref[...])
    x1_ref[...] = x1.reshape(nb, tt, D_MODEL)


def _const_spec(shape):
    nd = len(shape)
    return pl.BlockSpec(shape, lambda i, j, _nd=nd: (0,) * _nd, pipeline_mode=pl.Buffered(1))


DOWN_CAST_ROWS = 128


def _mixer(x, state, w, *, nb, tt, ffn_weights=None):
    B, T, _ = x.shape
    has_state = state is not None
    grid = (B // nb, T // tt)
    bpc = CHUNK // tt
    nchunks = nb // bpc
    pitch = _seq_pitch(tt)
    scr_rows = nchunks * pitch * bpc

    consts = [w["norm_mix"], w["w_in"], w["wb"], w["a_re"], w["a_im"], w["wc"], w["d_skip"],
              w["w_glu"], w["lb"], w["g_norm"], w["w_out"], w["mask_" + str(tt)]]
    in_specs = [pl.BlockSpec((nb, tt, D_MODEL), lambda i, j: (i, j, 0))]
    args = [x]
    if has_state:
        in_specs += [pl.BlockSpec((nb, S5_FLAT), lambda i, j: (i, 0)),
                     pl.BlockSpec((nb, S5_FLAT), lambda i, j: (i, 0)),
                     pl.BlockSpec((None, nb, HG_HEADS, HG_DIM, HG_DIM),
                                  lambda i, j: (0, i, 0, 0, 0))]
        args += list(state)
    in_specs += [_const_spec(c.shape) for c in consts]
    args += consts
    cast_ffn = ffn_weights is not None
    cast_specs, cast_shapes = [], []
    if cast_ffn:
        w_gate, w_up, w_down = ffn_weights
        n_t = T // tt
        d_ff = w_gate.shape[-1]
        up_rows = D_MODEL // n_t
        down_steps = d_ff // DOWN_CAST_ROWS
        assert D_MODEL % n_t == 0 and up_rows % 16 == 0 and d_ff % DOWN_CAST_ROWS == 0
        assert down_steps <= n_t and down_steps == DOWN_CAST_STEPS
        cast_specs = [pl.BlockSpec((None, up_rows, d_ff), lambda i, j: (0, j, 0)),
                      pl.BlockSpec((None, up_rows, d_ff), lambda i, j: (0, j, 0)),
                      pl.BlockSpec((None, DOWN_CAST_ROWS, D_MODEL),
                                   lambda i, j: (0, jnp.minimum(j, down_steps - 1), 0))]
        cast_shapes = [jax.ShapeDtypeStruct(a.shape, BF16) for a in ffn_weights]
        in_specs += cast_specs
        args += list(ffn_weights)

    out_shape = (jax.ShapeDtypeStruct((B, T, D_MODEL), F32),
                 jax.ShapeDtypeStruct((B, S5_FLAT), F32),
                 jax.ShapeDtypeStruct((B, S5_FLAT), F32),
                 jax.ShapeDtypeStruct((B, HG_HEADS, HG_DIM, HG_DIM), F32))
    out_specs = (pl.BlockSpec((nb, tt, D_MODEL), lambda i, j: (i, j, 0)),
                 pl.BlockSpec((nb, S5_FLAT), lambda i, j: (i, 0)),
                 pl.BlockSpec((nb, S5_FLAT), lambda i, j: (i, 0)),
                 pl.BlockSpec((nb, HG_HEADS, HG_DIM, HG_DIM), lambda i, j: (i, 0, 0, 0)))
    out_shape += tuple(cast_shapes)
    out_specs += tuple(cast_specs)
    scratch = [pltpu.VMEM((nb * tt, IN_COLS), F32),
               pltpu.VMEM((N_SLABS, scr_rows, LANES), F32),
               pltpu.VMEM((N_SLABS, scr_rows, LANES), F32),
               pltpu.VMEM((nb * tt, D_MODEL), BF16)]
    return pl.pallas_call(
        functools.partial(_mixer_kernel, nb=nb, tt=tt, has_state=has_state, cast_ffn=cast_ffn),
        grid=grid, in_specs=in_specs, out_specs=out_specs, out_shape=out_shape,
        scratch_shapes=scratch,
        compiler_params=pltpu.CompilerParams(
            dimension_semantics=("arbitrary", "arbitrary"), vmem_limit_bytes=VMEM_LIMIT),
        name="mixer_t%d" % tt,
    )(*args)


FF_CHUNK = 256


def _ffn_kernel(xa_ref, xb_ref, nffn_ref, wg_ref, wu_ref, wd_ref, nfin_ref, ya_ref, yb_ref,
                act_scr, *, steps_a):
    def tile(x_ref, y_ref):
        x = x_ref[...]
        xg = (x * nffn_ref[...]).astype(BF16)
        r = lax.rsqrt(jnp.mean(x * x, axis=-1, keepdims=True) + EPS)
        d_ff = wg_ref.shape[1]
        for f0 in range(0, d_ff, FF_CHUNK):
            gate = r * _dot(xg, wg_ref[:, f0:f0 + FF_CHUNK])
            up = r * _dot(xg, wu_ref[:, f0:f0 + FF_CHUNK])
            act_scr[:, f0:f0 + FF_CHUNK] = (gate * _sigmoid(gate) * up).astype(BF16)
        acc = x + _dot(act_scr[...], wd_ref[...])
        y_ref[...] = _rms(acc, nfin_ref[...])

    step = pl.program_id(0)

    @pl.when(step < steps_a)
    def _first():
        tile(xa_ref, ya_ref)

    @pl.when(step >= steps_a)
    def _second():
        tile(xb_ref, yb_ref)


def _ffn(xa, xb, w, w_bf16, *, tm):
    steps_a, steps_b = xa.shape[0] // tm, xb.shape[0] // tm
    d_ff = w_bf16[0].shape[-1]

    def cspec(shape):
        return pl.BlockSpec(shape, lambda i: (0, 0), pipeline_mode=pl.Buffered(1))

    def layer_spec(shape):
        return pl.BlockSpec((None,) + shape, lambda i: (0, 0, 0), pipeline_mode=pl.Buffered(1))

    spec_a = pl.BlockSpec((tm, D_MODEL), lambda i: (jnp.minimum(i, steps_a - 1), 0))
    spec_b = pl.BlockSpec((tm, D_MODEL), lambda i: (jnp.maximum(i - steps_a, 0), 0),
                          pipeline_mode=pl.Buffered(1))
    return pl.pallas_call(
        functools.partial(_ffn_kernel, steps_a=steps_a),
        grid=(steps_a + steps_b,),
        in_specs=[spec_a, spec_b,
                  cspec((1, D_MODEL)), layer_spec((D_MODEL, d_ff)), layer_spec((D_MODEL, d_ff)),
                  layer_spec((d_ff, D_MODEL)), cspec((1, D_MODEL))],
        out_specs=(spec_a, spec_b),
        out_shape=(jax.ShapeDtypeStruct(xa.shape, F32), jax.ShapeDtypeStruct(xb.shape, F32)),
        scratch_shapes=[pltpu.VMEM((tm, d_ff), BF16)],
        compiler_params=pltpu.CompilerParams(
            dimension_semantics=("arbitrary",), vmem_limit_bytes=VMEM_LIMIT),
        name="ffn",
    )(xa, xb, w["norm_ffn"], *w_bf16, w["norm_final"])


HALF_GROUPS = S5_GROUPS // 2


def _block_diag_halves(stacks):
    rows = HALF_GROUPS * S5_GROUP
    cols = HALF_GROUPS * S5_STATE
    x = jnp.concatenate([m.reshape(2 * rows, S5_STATE) for m in stacks], axis=0).astype(BF16)
    spread = jnp.asarray(np.tile(np.eye(S5_STATE), (1, HALF_GROUPS)), dtype=BF16)
    diag = (np.arange(rows)[:, None] // S5_GROUP) == (np.arange(cols)[None, :] // S5_STATE)
    y = jnp.dot(x, spread, preferred_element_type=BF16)
    y = jnp.where(np.tile(diag, (2 * len(stacks), 1)), y, jnp.zeros((), BF16))
    return y.reshape(len(stacks), 2, rows, cols)


def _segment_mask(tt):
    r = np.arange(CHUNK)
    same = (r[:, None] // tt) == (r[None, :] // tt)
    return jnp.asarray(same & (r[:, None] >= r[None, :]), dtype=BF16)


def _prepare(lb_param, norm_mix, w_in, s5_a_re, s5_a_im, s5_log_dt, s5_b_re, s5_b_im, s5_c_re,
             s5_c_im, s5_d, s5_w_glu, hg_norm, w_out, norm_ffn, w_gate, w_up, w_down, norm_final):
    l = 0
    a_re = s5_a_re[l].astype(F32)
    a_im = s5_a_im[l].astype(F32)
    dt = jnp.exp(s5_log_dt[l].astype(F32))[:, None]
    mag = jnp.exp(a_re * dt)
    ab_re = mag * jnp.cos(a_im * dt)
    ab_im = mag * jnp.sin(a_im * dt)
    den = a_re * a_re + a_im * a_im
    nr = ab_re - 1.0
    ni = ab_im
    f_re = (nr * a_re + ni * a_im) / den
    f_im = (ni * a_re - nr * a_im) / den
    b_re = s5_b_re[l].astype(F32).transpose(0, 2, 1)
    b_im = s5_b_im[l].astype(F32).transpose(0, 2, 1)
    bb_re = f_re[:, None, :] * b_re - f_im[:, None, :] * b_im
    bb_im = f_re[:, None, :] * b_im + f_im[:, None, :] * b_re
    blk = _block_diag_halves([bb_re, bb_im, s5_c_re[l].astype(F32), -s5_c_im[l].astype(F32)])
    wb = blk[0:2]
    wc = jnp.swapaxes(blk[2:4], 2, 3)
    lb_all = jnp.cumsum(jax.nn.softmax(lb_param.astype(F32), axis=0), axis=0)
    return {
        "norm_mix": norm_mix[l].reshape(1, D_MODEL).astype(F32),
        "w_in": w_in[l].astype(BF16),
        "wb": wb,
        "a_re": ab_re.reshape(1, S5_FLAT),
        "a_im": ab_im.reshape(1, S5_FLAT),
        "wc": wc,
        "d_skip": s5_d[l].reshape(1, S5_WIDTH).astype(F32),
        "w_glu": s5_w_glu[l].astype(BF16),
        "lb": lb_all[l].reshape(1, HG_WIDTH),
        "g_norm": hg_norm[l].reshape(1, HG_DIM).astype(F32),
        "w_out": w_out[l].astype(BF16),
        "mask_64": _segment_mask(64),
        "mask_8": _segment_mask(8),
        "norm_ffn": norm_ffn[l].reshape(1, D_MODEL).astype(F32),
        "w_gate": w_gate.astype(F32),
        "w_up": w_up.astype(F32),
        "w_down": w_down.astype(F32),
        "norm_final": norm_final.reshape(1, D_MODEL).astype(F32),
    }


def _states(h_re, h_im, s_new):
    b = h_re.shape[0]
    return (h_re.reshape(1, b, S5_GROUPS, S5_STATE), h_im.reshape(1, b, S5_GROUPS, S5_STATE),
            s_new.reshape(1, b, HG_HEADS, HG_DIM, HG_DIM))


def kernel(x_prompt, x_sample, state_s5_re, state_s5_im, state_hgrn, lb_param, norm_mix, w_in,
           s5_a_re, s5_a_im, s5_log_dt, s5_b_re, s5_b_im, s5_c_re, s5_c_im, s5_d, s5_w_glu,
           hg_norm, w_out, norm_ffn, w_gate, w_up, w_down, norm_final):
    assert norm_mix.shape[0] == 1, "single-layer trunk"
    w = _prepare(lb_param, norm_mix, w_in, s5_a_re, s5_a_im, s5_log_dt, s5_b_re, s5_b_im,
                 s5_c_re, s5_c_im, s5_d, s5_w_glu, hg_norm, w_out, norm_ffn, w_gate, w_up,
                 w_down, norm_final)
    bp, tp, _ = x_prompt.shape
    bs, ts, _ = x_sample.shape
    x1_p, *rest = _mixer(x_prompt, None, w, nb=8, tt=64,
                         ffn_weights=(w["w_gate"], w["w_up"], w["w_down"]))
    st_p, w_bf16 = rest[:3], rest[3:]
    st = (state_s5_re[0].reshape(bs, S5_FLAT), state_s5_im[0].reshape(bs, S5_FLAT), state_hgrn)
    x1_s, *st_s = _mixer(x_sample, st, w, nb=16, tt=8)
    y_p, y_s = _ffn(x1_p.reshape(bp * tp, D_MODEL), x1_s.reshape(bs * ts, D_MODEL), w, w_bf16,
                    tm=1024)
    return (y_p.reshape(bp, tp, D_MODEL), y_s.reshape(bs, ts, D_MODEL),
            *_states(*st_p), *_states(*st_s))
```
